```python
import jax
import jax.numpy as jnp
from jax import lax
import numpy as np

D_MODEL = 1024
BATCH = 16
SEQ = 256
DEPTH = 2
DEC_BATCH = 4
DEC_SEQ = 4096
PAST_LEN = 256

GRID_W = 64
HEAD_DIM = 64
BR_WIDTH = 256
N_BRANCH = 4
A_GROUPS = 4
A_GDIM = BR_WIDTH // A_GROUPS
CHUNK = 128
CONV_W = 31
N_HEADS = 4
KV_HEADS = 2
GQA_G = N_HEADS // KV_HEADS
KV_WIDTH = KV_HEADS * HEAD_DIM
WINDOW = 128
WBLK = WINDOW
QBLK = 128
ROPE_THETA = 10000.0
EPS = 1e-6
N_EXPERTS = 32
TOP_K = 4
D_FF = D_MODEL
SWIGLU_LIMIT = 7.0
SWIGLU_ALPHA = 1.702
MOE_BLK = 128
IN_SIZES = (BR_WIDTH, BR_WIDTH, 2 * BR_WIDTH, BR_WIDTH, KV_WIDTH, KV_WIDTH, BR_WIDTH, KV_WIDTH, KV_WIDTH, N_BRANCH * D_MODEL)
IN_WIDTH = sum(IN_SIZES)
IN_OFFSETS = tuple(int(v) for v in np.cumsum(IN_SIZES)[:-1])

kernel_name = "hybrid_prefix_diffusion_step"


def _rmsnorm(x, g):
    xf = x.astype(jnp.float32)
    y = xf * lax.rsqrt(jnp.mean(xf * xf, axis=-1, keepdims=True) + EPS) * g.astype(jnp.float32)
    return y.astype(x.dtype)


def _layernorm(x, g, b):
    xf = x.astype(jnp.float32)
    mu = jnp.mean(xf, axis=-1, keepdims=True)
    xc = xf - mu
    var = jnp.mean(xc * xc, axis=-1, keepdims=True)
    return (xc * lax.rsqrt(var + EPS) * g.astype(jnp.float32) + b.astype(jnp.float32)).astype(x.dtype)


def _rope_1d(x, pos):
    half = x.shape[-1] // 2
    freqs = ROPE_THETA ** (-jnp.arange(half, dtype=jnp.float32) / half)
    ang = pos.astype(jnp.float32)[:, None] * freqs
    cos = jnp.cos(ang)[:, None, :]
    sin = jnp.sin(ang)[:, None, :]
    xf = x.astype(jnp.float32)
    x1, x2 = xf[..., :half], xf[..., half:]
    return jnp.concatenate([x1 * cos - x2 * sin, x1 * sin + x2 * cos], axis=-1).astype(x.dtype)


def _rope_2d(x):
    rows = x.shape[1] // GRID_W
    t = jnp.arange(rows * GRID_W)
    h = x.shape[-1] // 2
    return jnp.concatenate([_rope_1d(x[..., :h], t // GRID_W), _rope_1d(x[..., h:], t % GRID_W)], axis=-1)


def _softmax(s, sink):
    if sink is None:
        return jax.nn.softmax(s, axis=-1)
    sk = jnp.broadcast_to(sink.astype(jnp.float32)[:, :, None, None], s.shape[:-1] + (1,))
    return jax.nn.softmax(jnp.concatenate([s, sk], axis=-1), axis=-1)[..., :-1]


def _ctx_attention(q, k, v, sink):
    B, L = q.shape[0], q.shape[1]
    nb = L // QBLK
    scale = HEAD_DIM ** -0.5
    qb = jnp.moveaxis(q.reshape(B, nb, QBLK, KV_HEADS, GQA_G, HEAD_DIM), 1, 0)

    def one(qblk):
        s = jnp.einsum("bqkgd,bskd->bkgqs", qblk, k, preferred_element_type=jnp.float32) * scale
        p = _softmax(s, sink).astype(v.dtype)
        return jnp.einsum("bkgqs,bskd->bqkgd", p, v)

    o = lax.map(one, qb)
    return jnp.moveaxis(o, 0, 1).reshape(B, L, N_HEADS * HEAD_DIM)


def _window_attention(q, k, v, kc, vc, sink):
    B, L = q.shape[0], q.shape[1]
    nb = L // WBLK
    scale = HEAD_DIM ** -0.5
    pad = ((0, 0), (WBLK, WBLK), (0, 0), (0, 0))
    kp = jnp.pad(k, pad).reshape(B, nb + 2, WBLK, KV_HEADS, HEAD_DIM)
    vp = jnp.pad(v, pad).reshape(B, nb + 2, WBLK, KV_HEADS, HEAD_DIM)
    kband = jnp.concatenate([kp[:, :-2], kp[:, 1:-1], kp[:, 2:]], axis=2)
    vband = jnp.concatenate([vp[:, :-2], vp[:, 1:-1], vp[:, 2:]], axis=2)
    qb = q.reshape(B, nb, WBLK, KV_HEADS, GQA_G, HEAD_DIM)
    s_band = jnp.einsum("bnqkgd,bnskd->bnkgqs", qb, kband, preferred_element_type=jnp.float32) * scale
    i = jnp.arange(WBLK)[:, None]
    j = jnp.arange(3 * WBLK)[None, :]
    in_window = jnp.abs(i + WBLK - j) <= WINDOW
    kpos = jnp.arange(nb)[:, None] * WBLK - WBLK + jnp.arange(3 * WBLK)[None, :]
    valid = (kpos >= 0) & (kpos < L)
    mask = in_window[None] & valid[:, None, :]
    s_band = jnp.where(mask[None, :, None, None], s_band, -jnp.inf)
    s_ctx = jnp.einsum("bnqkgd,bskd->bnkgqs", qb, kc, preferred_element_type=jnp.float32) * scale
    p = _softmax(jnp.concatenate([s_band, s_ctx], axis=-1), sink).astype(v.dtype)
    o = (jnp.einsum("bnkgqs,bnskd->bnqkgd", p[..., :3 * WBLK], vband)
         + jnp.einsum("bnkgqs,bskd->bnqkgd", p[..., 3 * WBLK:], vc))
    return o.reshape(B, L, N_HEADS * HEAD_DIM)


def _full_attention(q, k, v, kc, vc):
    B, L = q.shape[0], q.shape[1]
    nb = L // QBLK
    scale = HEAD_DIM ** -0.5
    qb = jnp.moveaxis(q.reshape(B, nb, QBLK, KV_HEADS, GQA_G, HEAD_DIM), 1, 0)

    def one(qblk):
        s_lat = jnp.einsum("bqkgd,bskd->bkgqs", qblk, k, preferred_element_type=jnp.float32)
        s_ctx = jnp.einsum("bqkgd,bskd->bkgqs", qblk, kc, preferred_element_type=jnp.float32)
        p = jax.nn.softmax(jnp.concatenate([s_lat, s_ctx], axis=-1) * scale, axis=-1).astype(v.dtype)
        return (jnp.einsum("bkgqs,bskd->bqkgd", p[..., :L], v)
                + jnp.einsum("bkgqs,bskd->bqkgd", p[..., L:], vc))

    o = lax.map(one, qb)
    return jnp.moveaxis(o, 0, 1).reshape(B, L, N_HEADS * HEAD_DIM)


def _token_mixers(n, lp, ctx_kv):
    B, L, _ = n.shape
    z = n @ lp["w_in"]
    a_u, a_v, b_in, cq, ck, cv, dq, dk, dv, gate_logits = jnp.split(z, IN_OFFSETS, axis=-1)
    u = jax.nn.gelu(a_u, approximate=False)
    va = _layernorm(jax.nn.gelu(a_v, approximate=False), lp["a_ln_g"], lp["a_ln_b"])
    vch = va.reshape(B, L // CHUNK, CHUNK, A_GROUPS, A_GDIM)
    sp = jnp.einsum("gpq,bnqgc->bnpgc", lp["w_sp"], vch) + lp["b_sp"].T[:, :, None]
    o_a = u * sp.reshape(B, L, BR_WIDTH)
    glu = b_in[..., :BR_WIDTH] * jax.nn.sigmoid(b_in[..., BR_WIDTH:])
    dw = lax.conv_general_dilated(glu, lp["conv_w"][:, None, :], (1,), [(CONV_W // 2, CONV_W // 2)],
                                  dimension_numbers=("NWC", "WIO", "NWC"),
                                  feature_group_count=BR_WIDTH) + lp["conv_b"]
    o_b = jax.nn.silu(_layernorm(dw, lp["b_ln_g"], lp["b_ln_b"]))
    cq = cq.reshape(B, L, N_HEADS, HEAD_DIM)
    ck = ck.reshape(B, L, KV_HEADS, HEAD_DIM)
    cv = cv.reshape(B, L, KV_HEADS, HEAD_DIM)
    dq = _rmsnorm(dq.reshape(B, L, N_HEADS, HEAD_DIM), lp["qn_g"])
    dk = _rmsnorm(dk.reshape(B, L, KV_HEADS, HEAD_DIM), lp["kn_g"])
    dv = dv.reshape(B, L, KV_HEADS, HEAD_DIM)
    sink = lp["sink"].reshape(KV_HEADS, GQA_G)
    if ctx_kv is None:
        o_c = _ctx_attention(cq, ck, cv, sink)
        o_d = _ctx_attention(dq, dk, dv, None)
        new_kv = (ck, cv, dk, dv)
    else:
        wk, wv, fk, fv = ctx_kv
        o_c = _window_attention(_rope_2d(cq), _rope_2d(ck), cv, wk, wv, sink)
        o_d = _full_attention(_rope_2d(dq), _rope_2d(dk), dv, fk, fv)
        new_kv = None
    branches = jnp.stack([o_a, o_b, o_c, o_d], axis=2)
    proj = jnp.einsum("blkc,kcd->blkd", branches, lp["w_br"])
    gates = jax.nn.sigmoid(gate_logits.reshape(B, L, N_BRANCH, D_MODEL))
    out = jnp.sum(gates * proj, axis=2) @ lp["w_o"]
    return out, new_kv


def _moe(x, w_router, b_router, w1, b1, w2, b2):
    T, D = x.shape
    TK = T * TOP_K
    logits = jnp.matmul(x, w_router, preferred_element_type=jnp.float32) + b_router.astype(jnp.float32)
    top_val, top_idx = lax.top_k(logits, TOP_K)
    gate = jax.nn.softmax(top_val, axis=-1).reshape(TK)
    flat_e = top_idx.reshape(TK)
    order = jnp.argsort(flat_e)
    sorted_e = flat_e[order]
    counts = jnp.bincount(flat_e, length=N_EXPERTS)
    padded = (counts + MOE_BLK - 1) // MOE_BLK * MOE_BLK
    pad_end = jnp.cumsum(padded)
    pad_start = pad_end - padded
    grp_start = jnp.cumsum(counts) - counts
    dest = pad_start[sorted_e] + jnp.arange(TK) - grp_start[sorted_e]
    n_rows = -(-TK // MOE_BLK) * MOE_BLK + N_EXPERTS * MOE_BLK
    n_blocks = n_rows // MOE_BLK
    tok = jnp.zeros((n_rows,), jnp.int32).at[dest].set((order // TOP_K).astype(jnp.int32))
    wgt = jnp.zeros((n_rows,), jnp.float32).at[dest].set(gate[order])
    blk_e = jnp.minimum(jnp.searchsorted(pad_end, jnp.arange(n_blocks) * MOE_BLK, side="right"), N_EXPERTS - 1)
    xb = x[tok].reshape(n_blocks, MOE_BLK, D)

    def expert_block(args):
        xblk, e = args
        hh = xblk @ w1[e] + b1[e]
        x_glu = jnp.minimum(hh[:, :D_FF], SWIGLU_LIMIT)
        x_lin = jnp.clip(hh[:, D_FF:], -SWIGLU_LIMIT, SWIGLU_LIMIT)
        act = x_glu * jax.nn.sigmoid(SWIGLU_ALPHA * x_glu) * (x_lin + 1.0)
        return act @ w2[e] + b2[e]

    yb = lax.map(expert_block, (xb, blk_e)).reshape(n_rows, D)
    return jnp.zeros_like(x).at[tok].add(yb * wgt[:, None].astype(x.dtype))


def _layer(h, cvec, lp, ctx_kv):
    mod = jax.nn.silu(cvec) @ lp["w_mod"] + lp["b_mod"]
    sh1, sc1, g1, sh2, sc2, g2 = jnp.split(mod[:, None, :], 6, axis=-1)
    n = _rmsnorm(h, lp["norm_g"][0]) * (1.0 + sc1) + sh1
    mix, new_kv = _token_mixers(n, lp, ctx_kv)
    h = h + g1 * _rmsnorm(mix, lp["norm_g"][1])
    n = _rmsnorm(h, lp["norm_g"][2]) * (1.0 + sc2) + sh2
    f = _moe(n.reshape(-1, D_MODEL), lp["w_router"], lp["b_router"], lp["w1"], lp["b1"],
             lp["w2"], lp["b2"]).reshape(h.shape)
    h = h + g2 * _rmsnorm(f, lp["norm_g"][3])
    return h, new_kv


def setup_inputs(seed: int = 0) -> dict:
    key = jax.random.key(seed)
    ks = iter(jax.random.split(key, 40))
    f32 = jnp.float32

    def nrm(shape, scale):
        return jax.random.normal(next(ks), shape, f32) * scale

    def gain(shape):
        return 1.0 + nrm(shape, 0.05)

    cshape = (DEC_BATCH, DEPTH, PAST_LEN, KV_HEADS, HEAD_DIM)
    return {
        "x_prompt": nrm((BATCH, SEQ, D_MODEL), 1.0),
        "x_sample": nrm((DEC_BATCH, DEC_SEQ, D_MODEL), 1.0),
        "cache_win_k": nrm(cshape, 1.0),
        "cache_win_v": nrm(cshape, 1.0),
        "cache_full_k": nrm(cshape, 1.0),
        "cache_full_v": nrm(cshape, 1.0),
        "c": nrm((DEC_BATCH, D_MODEL), 1.0),
        "c_ctx": nrm((D_MODEL,), 1.0),
        "w_mod": nrm((DEPTH, D_MODEL, 6 * D_MODEL), 0.2 * D_MODEL ** -0.5),
        "b_mod": nrm((DEPTH, 6 * D_MODEL), 0.01),
        "norm_g": gain((DEPTH, 4, D_MODEL)),
        "w_in": nrm((DEPTH, D_MODEL, IN_WIDTH), D_MODEL ** -0.5),
        "a_ln_g": gain((DEPTH, BR_WIDTH)),
        "a_ln_b": nrm((DEPTH, BR_WIDTH), 0.01),
        "w_sp": nrm((DEPTH, A_GROUPS, CHUNK, CHUNK), CHUNK ** -0.5),
        "b_sp": nrm((DEPTH, A_GROUPS, CHUNK), 0.01),
        "conv_w": nrm((DEPTH, CONV_W, BR_WIDTH), CONV_W ** -0.5),
        "conv_b": nrm((DEPTH, BR_WIDTH), 0.01),
        "b_ln_g": gain((DEPTH, BR_WIDTH)),
        "b_ln_b": nrm((DEPTH, BR_WIDTH), 0.01),
        "sink": nrm((DEPTH, N_HEADS), 0.5),
        "qn_g": gain((DEPTH, HEAD_DIM)),
        "kn_g": gain((DEPTH, HEAD_DIM)),
        "w_br": nrm((DEPTH, N_BRANCH, BR_WIDTH, D_MODEL), BR_WIDTH ** -0.5),
        "w_o": nrm((DEPTH, D_MODEL, D_MODEL), D_MODEL ** -0.5),
        "w_router": nrm((DEPTH, D_MODEL, N_EXPERTS), D_MODEL ** -0.5),
        "b_router": nrm((DEPTH, N_EXPERTS), 0.01),
        "w1": nrm((DEPTH, N_EXPERTS, D_MODEL, 2 * D_FF), D_MODEL ** -0.5),
        "b1": nrm((DEPTH, N_EXPERTS, 2 * D_FF), 0.01),
        "w2": nrm((DEPTH, N_EXPERTS, D_FF, D_MODEL), D_FF ** -0.5),
        "b2": nrm((DEPTH, N_EXPERTS, D_MODEL), 0.01),
    }


def reference(x_prompt, x_sample, cache_win_k, cache_win_v, cache_full_k, cache_full_v, c, c_ctx,
              w_mod, b_mod, norm_g, w_in, a_ln_g, a_ln_b, w_sp, b_sp, conv_w, conv_b, b_ln_g, b_ln_b,
              sink, qn_g, kn_g, w_br, w_o, w_router, b_router, w1, b1, w2, b2):
    h_p = x_prompt
    h_s = x_sample
    wk_list, wv_list, fk_list, fv_list = [], [], [], []
    for l in range(DEPTH):
        lp = {"w_mod": w_mod[l], "b_mod": b_mod[l], "norm_g": norm_g[l], "w_in": w_in[l],
              "a_ln_g": a_ln_g[l], "a_ln_b": a_ln_b[l], "w_sp": w_sp[l], "b_sp": b_sp[l],
              "conv_w": conv_w[l], "conv_b": conv_b[l], "b_ln_g": b_ln_g[l], "b_ln_b": b_ln_b[l],
              "sink": sink[l], "qn_g": qn_g[l], "kn_g": kn_g[l], "w_br": w_br[l], "w_o": w_o[l],
              "w_router": w_router[l], "b_router": b_router[l], "w1": w1[l], "b1": b1[l],
              "w2": w2[l], "b2": b2[l]}
        h_p, (wk, wv, fk, fv) = _layer(h_p, c_ctx[None, :], lp, None)
        wk_list.append(wk)
        wv_list.append(wv)
        fk_list.append(fk)
        fv_list.append(fv)
        ctx = (cache_win_k[:, l], cache_win_v[:, l], cache_full_k[:, l], cache_full_v[:, l])
        h_s, _ = _layer(h_s, c, lp, ctx)
    state_win_k = jnp.stack(wk_list, axis=1)
    state_win_v = jnp.stack(wv_list, axis=1)
    state_full_k = jnp.stack(fk_list, axis=1)
    state_full_v = jnp.stack(fv_list, axis=1)
    return (h_p, h_s, state_win_k, state_win_v, state_full_k, state_full_v)
```

```python
import functools

import numpy as np
import jax
import jax.numpy as jnp
from jax import lax
from jax.experimental import pallas as pl
from jax.experimental.pallas import tpu as pltpu

D_MODEL = 1024
BATCH = 16
SEQ = 256
DEPTH = 2
DEC_BATCH = 4
DEC_SEQ = 4096
PAST_LEN = 256
GRID_W = 64
HEAD_DIM = 64
BR_WIDTH = 256
N_BRANCH = 4
A_GROUPS = 4
CHUNK = 128
CONV_W = 31
N_HEADS = 4
KV_HEADS = 2
KV_WIDTH = KV_HEADS * HEAD_DIM
WINDOW = 128
ROPE_THETA = 10000.0
EPS = 1e-6
N_EXPERTS = 32
TOP_K = 4
D_FF = D_MODEL
SWIGLU_LIMIT = 7.0
SWIGLU_ALPHA = 1.702
MAIN_WIDTH = 2048

LANES = 128
T_CTX = BATCH * SEQ
T_LAT = DEC_BATCH * DEC_SEQ
T_ALL = T_CTX + T_LAT
TM = 256
N_TILES = T_ALL // TM
CTX_TILES = T_CTX // TM
LAT_TILES_PER_SEQ = DEC_SEQ // TM
HALO = 16
MOE_BLK = 256
N_MOE_BLOCKS = (T_ALL * TOP_K) // MOE_BLK + N_EXPERTS
N_MOE_ROWS = N_MOE_BLOCKS * MOE_BLK
WIN_TQ = 512
FULL_TQ = 256
FULL_TK = 512
NEG_BIG = -1e30
VMEM_LIMIT = 56 * 1024 * 1024

_F32 = jnp.float32
_BF16 = jnp.bfloat16


def _params(n_axes, vmem=None):
    return pltpu.CompilerParams(
        dimension_semantics=("arbitrary",) * n_axes,
        vmem_limit_bytes=vmem if vmem is not None else VMEM_LIMIT)


def _split_dot(a, b_hi, b_lo):
    a_hi = a.astype(_BF16)
    a_lo = (a - a_hi.astype(_F32)).astype(_BF16)
    return (jnp.dot(a_hi, b_hi, preferred_element_type=_F32)
            + jnp.dot(a_lo, b_hi, preferred_element_type=_F32)
            + jnp.dot(a_hi, b_lo, preferred_element_type=_F32))


def _hi_lo(w):
    hi = w.astype(_BF16)
    lo = (w - hi.astype(_F32)).astype(_BF16)
    return hi, lo


def _mod_kernel(c_ref, w_ref, b_ref, o_ref):
    c = c_ref[...]
    s = c * jax.nn.sigmoid(c)
    w = w_ref[...]
    w_hi = w.astype(_BF16)
    w_lo = (w - w_hi.astype(_F32)).astype(_BF16)
    o_ref[...] = _split_dot(s, w_hi, w_lo) + b_ref[...]


def _mod_call(cvec, w_mod, b_mod):
    tn = 1024
    return pl.pallas_call(
        _mod_kernel,
        grid=(DEPTH, 6 * D_MODEL // tn),
        in_specs=[
            pl.BlockSpec((8, D_MODEL), lambda l, j: (0, 0)),
            pl.BlockSpec((None, D_MODEL, tn), lambda l, j: (l, 0, j)),
            pl.BlockSpec((None, 1, tn), lambda l, j: (l, 0, j)),
        ],
        out_specs=pl.BlockSpec((None, 8, tn), lambda l, j: (l, 0, j)),
        out_shape=jax.ShapeDtypeStruct((DEPTH, 8, 6 * D_MODEL), _F32),
        compiler_params=_params(2),
        name="mod",
    )(cvec, w_mod, b_mod.reshape(DEPTH, 1, 6 * D_MODEL))


def _rms(x, g):
    ms = jnp.mean(x * x, axis=-1, keepdims=True)
    return x * lax.rsqrt(ms + EPS) * g


def _head_rms(x, ones_bd, g):
    xx = x * x
    hi = xx.astype(_BF16)
    lo = (xx - hi.astype(_F32)).astype(_BF16)
    ss = (jnp.dot(hi, ones_bd, preferred_element_type=_F32)
          + jnp.dot(lo, ones_bd, preferred_element_type=_F32))
    return x * lax.rsqrt(ss * (1.0 / HEAD_DIM) + EPS) * g


def _rope(x, cos, sin_signed):
    w = x.shape[-1]
    lane = lax.broadcasted_iota(jnp.int32, x.shape, 1)
    first = (lane % 32) < 16
    partner = jnp.where(first, pltpu.roll(x, w - 16, 1), pltpu.roll(x, 16, 1))
    return x * cos + partner * sin_signed


def _inproj_kernel(modrow_ref, ropeblk_ref, stblk_ref,
                   x_ref, mod_ref, g0_ref, w_ref, qn_ref, kn_ref, ones_ref,
                   cos_ref, sin_ref,
                   ab_ref, qc_ref, kc_ref, vc_ref, qd_ref, kd_ref, vd_ref, st_ref):
    x = x_ref[...]
    mod = mod_ref[...]
    sh1 = mod[:, 0:D_MODEL]
    sc1 = mod[:, D_MODEL:2 * D_MODEL]
    n = _rms(x, g0_ref[...]) * (1.0 + sc1) + sh1
    z = jnp.dot(n.astype(_BF16), w_ref[...], preferred_element_type=_F32)
    ab_ref[...] = z[:, 0:1024]
    cq = z[:, 1024:1280]
    ck = z[:, 1280:1408]
    cv = z[:, 1408:1536]
    dq = z[:, 1536:1792]
    dk = z[:, 1792:1920]
    dv = z[:, 1920:2048]
    ones = ones_ref[...]
    dq = _head_rms(dq, ones, qn_ref[...])
    dk = _head_rms(dk, ones[0:KV_WIDTH, 0:KV_WIDTH], kn_ref[...])
    cos = cos_ref[...]
    sin = sin_ref[...]
    scale = HEAD_DIM ** -0.5
    qc_ref[...] = (_rope(cq, cos, sin) * scale).astype(_BF16)
    qd_ref[...] = (_rope(dq, cos, sin) * scale).astype(_BF16)
    kc = _rope(ck, cos[:, 0:KV_WIDTH], sin[:, 0:KV_WIDTH])
    kd = _rope(dk, cos[:, 0:KV_WIDTH], sin[:, 0:KV_WIDTH])
    kc_ref[...] = kc.astype(_BF16)
    kd_ref[...] = kd.astype(_BF16)
    vc_ref[...] = cv.astype(_BF16)
    vd_ref[...] = dv.astype(_BF16)
    st_ref[:, 0:128] = kc
    st_ref[:, 128:256] = cv
    st_ref[:, 256:384] = kd
    st_ref[:, 384:512] = dv


def _inproj_call(x, mod3, g0, w_main, qn, kn, ones_bd, cos_t, sin_t, tables):
    modrow, ropeblk, stblk = tables
    row = lambda i, a, b, c: (i, 0)
    full = lambda i, a, b, c: (0, 0)
    bf = lambda w: jax.ShapeDtypeStruct((T_ALL, w), _BF16)
    grid_spec = pltpu.PrefetchScalarGridSpec(
        num_scalar_prefetch=3,
        grid=(N_TILES,),
        in_specs=[
            pl.BlockSpec((TM, D_MODEL), row),
            pl.BlockSpec((None, 1, 6 * D_MODEL), lambda i, a, b, c: (a[i], 0, 0)),
            pl.BlockSpec((1, D_MODEL), full),
            pl.BlockSpec((D_MODEL, MAIN_WIDTH), full),
            pl.BlockSpec((1, BR_WIDTH), full),
            pl.BlockSpec((1, KV_WIDTH), full),
            pl.BlockSpec((BR_WIDTH, BR_WIDTH), full),
            pl.BlockSpec((TM, BR_WIDTH), lambda i, a, b, c: (b[i], 0)),
            pl.BlockSpec((TM, BR_WIDTH), lambda i, a, b, c: (b[i], 0)),
        ],
        out_specs=[
            pl.BlockSpec((TM, 1024), row),
            pl.BlockSpec((TM, BR_WIDTH), row),
            pl.BlockSpec((TM, KV_WIDTH), row),
            pl.BlockSpec((TM, KV_WIDTH), row),
            pl.BlockSpec((TM, BR_WIDTH), row),
            pl.BlockSpec((TM, KV_WIDTH), row),
            pl.BlockSpec((TM, KV_WIDTH), row),
            pl.BlockSpec((TM, 512), lambda i, a, b, c: (c[i], 0)),
        ],
    )
    return pl.pallas_call(
        _inproj_kernel,
        grid_spec=grid_spec,
        out_shape=[
            jax.ShapeDtypeStruct((T_ALL, 1024), _F32),
            bf(BR_WIDTH), bf(KV_WIDTH), bf(KV_WIDTH),
            bf(BR_WIDTH), bf(KV_WIDTH), bf(KV_WIDTH),
            jax.ShapeDtypeStruct((T_CTX + TM, 512), _F32),
        ],
        compiler_params=_params(1),
        name="inproj",
    )(modrow, ropeblk, stblk, x, mod3, g0, w_main, qn, kn, ones_bd, cos_t, sin_t)


def _layernorm(x, g, b):
    mu = jnp.mean(x, axis=-1, keepdims=True)
    xc = x - mu
    var = jnp.mean(xc * xc, axis=-1, keepdims=True)
    return xc * lax.rsqrt(var + EPS) * g + b


def _gelu(x):
    return 0.5 * x * (1.0 + lax.erf(x * (2.0 ** -0.5)))


def _mix_kernel(hasprev_ref, hasnext_ref,
                ab_ref, prev_ref, next_ref, alg_ref, alb_ref, wsp_ref, bsp_ref,
                cw_ref, cb_ref, blg_ref, blb_ref,
                oa_ref, ob_ref, buf_ref):
    i = pl.program_id(0)
    ab = ab_ref[...]
    u = _gelu(ab[:, 0:256])
    va = _layernorm(_gelu(ab[:, 256:512]), alg_ref[...], alb_ref[...])
    vab = va.astype(_BF16)
    lane = lax.broadcasted_iota(jnp.int32, (CHUNK, BR_WIDTH), 1)
    group = lane // (BR_WIDTH // A_GROUPS)
    bsp = bsp_ref[...]
    for c in range(TM // CHUNK):
        v_c = vab[c * CHUNK:(c + 1) * CHUNK, :]
        sp = bsp
        for g in range(A_GROUPS):
            full = jnp.dot(wsp_ref[g], v_c, preferred_element_type=_F32)
            sp = sp + jnp.where(group == g, full, 0.0)
        oa_ref[c * CHUNK:(c + 1) * CHUNK, :] = (u[c * CHUNK:(c + 1) * CHUNK, :] * sp).astype(_BF16)
    glu = ab[:, 512:768] * jax.nn.sigmoid(ab[:, 768:1024])
    pv = prev_ref[...]
    nx = next_ref[...]
    hp = hasprev_ref[i].astype(_F32)
    hn = hasnext_ref[i].astype(_F32)
    buf_ref[0:HALO, :] = pv[:, 0:256] * jax.nn.sigmoid(pv[:, 256:512]) * hp
    buf_ref[HALO:HALO + TM, :] = glu
    buf_ref[HALO + TM:HALO + TM + HALO, :] = nx[:, 0:256] * jax.nn.sigmoid(nx[:, 256:512]) * hn
    cw = cw_ref[...]
    acc = jnp.zeros((TM, BR_WIDTH), _F32) + cb_ref[...]
    off = HALO - CONV_W // 2
    for k in range(CONV_W):
        acc = acc + buf_ref[off + k:off + k + TM, :] * cw[k:k + 1, :]
    y = _layernorm(acc, blg_ref[...], blb_ref[...])
    ob_ref[...] = (y * jax.nn.sigmoid(y)).astype(_BF16)


def _mix_call(ab, a_ln_g, a_ln_b, wsp, bsp, conv_w, conv_b, b_ln_g, b_ln_b, tables):
    hasprev, hasnext = tables
    n_halo_blocks = T_ALL // HALO
    per = TM // HALO
    full2 = lambda i, a, b: (0, 0)
    grid_spec = pltpu.PrefetchScalarGridSpec(
        num_scalar_prefetch=2,
        grid=(N_TILES,),
        in_specs=[
            pl.BlockSpec((TM, 1024), lambda i, a, b: (i, 0)),
            pl.BlockSpec((HALO, 512), lambda i, a, b: (jnp.maximum(i * per - 1, 0), 1)),
            pl.BlockSpec((HALO, 512), lambda i, a, b: (jnp.minimum((i + 1) * per, n_halo_blocks - 1), 1)),
            pl.BlockSpec((1, BR_WIDTH), full2),
            pl.BlockSpec((1, BR_WIDTH), full2),
            pl.BlockSpec((A_GROUPS, CHUNK, CHUNK), lambda i, a, b: (0, 0, 0)),
            pl.BlockSpec((CHUNK, BR_WIDTH), full2),
            pl.BlockSpec((32, BR_WIDTH), full2),
            pl.BlockSpec((1, BR_WIDTH), full2),
            pl.BlockSpec((1, BR_WIDTH), full2),
            pl.BlockSpec((1, BR_WIDTH), full2),
        ],
        out_specs=[
            pl.BlockSpec((TM, BR_WIDTH), lambda i, a, b: (i, 0)),
            pl.BlockSpec((TM, BR_WIDTH), lambda i, a, b: (i, 0)),
        ],
        scratch_shapes=[pltpu.VMEM((TM + 2 * HALO, BR_WIDTH), _F32)],
    )
    return pl.pallas_call(
        _mix_kernel,
        grid_spec=grid_spec,
        out_shape=[jax.ShapeDtypeStruct((T_ALL, BR_WIDTH), _BF16)] * 2,
        compiler_params=_params(1),
        name="mixers_ab",
    )(hasprev, hasnext, ab, ab, ab, a_ln_g, a_ln_b, wsp, bsp, conv_w, conv_b, b_ln_g, b_ln_b)


def _lane_masks(n_rows):
    lane = lax.broadcasted_iota(jnp.int32, (n_rows, LANES), 1)
    return lane < HEAD_DIM, lane >= HEAD_DIM


def _group_queries(q, kv):
    tq = q.shape[0]
    lo, hi = _lane_masks(tq)
    keep = lo if kv == 0 else hi
    zero = jnp.zeros((), q.dtype)
    return jnp.concatenate([jnp.where(keep, q[:, 0:LANES], zero),
                            jnp.where(keep, q[:, LANES:2 * LANES], zero)], axis=0)


def _scores(qs, k):
    return lax.dot_general(qs, k, (((1,), (1,)), ((), ())), preferred_element_type=_F32)


def _attend_once(q, chunks, sinks):
    tq = q.shape[0]
    lo, hi = _lane_masks(tq)
    outs = [jnp.zeros((tq, LANES), _F32), jnp.zeros((tq, LANES), _F32)]
    for kv in range(KV_HEADS):
        qs = _group_queries(q, kv)
        ss = []
        for k, _, mask in chunks:
            s = _scores(qs, k)
            if mask is not None:
                s = jnp.where(mask, s, -jnp.inf)
            ss.append(s)
        m = ss[0].max(axis=-1, keepdims=True)
        for s in ss[1:]:
            m = jnp.maximum(m, s.max(axis=-1, keepdims=True))
        if sinks is not None:
            row = lax.broadcasted_iota(jnp.int32, (2 * tq, 1), 0)
            sink_col = jnp.where(row < tq, sinks[2 * kv], sinks[2 * kv + 1])
            m = jnp.maximum(m, sink_col)
            l = jnp.exp(sink_col - m)
        else:
            l = jnp.zeros((2 * tq, 1), _F32)
        acc = jnp.zeros((2 * tq, LANES), _F32)
        for s, (_, v, _) in zip(ss, chunks):
            p = jnp.exp(s - m)
            l = l + p.sum(axis=-1, keepdims=True)
            acc = acc + jnp.dot(p.astype(_BF16), v, preferred_element_type=_F32)
        o = acc / l
        keep = lo if kv == 0 else hi
        outs[0] = outs[0] + jnp.where(keep, o[0:tq], 0.0)
        outs[1] = outs[1] + jnp.where(keep, o[tq:2 * tq], 0.0)
    return jnp.concatenate(outs, axis=1)


def _ctx_attn_kernel(sink_ref, qc_ref, kc_ref, vc_ref, qd_ref, kd_ref, vd_ref, oc_ref, od_ref):
    sinks = [sink_ref[h] for h in range(N_HEADS)]
    oc_ref[...] = _attend_once(qc_ref[...], [(kc_ref[...], vc_ref[...], None)], sinks).astype(_BF16)
    od_ref[...] = _attend_once(qd_ref[...], [(kd_ref[...], vd_ref[...], None)], None).astype(_BF16)


def _ctx_attn_call(sink, qc, kc, vc, qd, kd, vd):
    seq = lambda b: (b, 0)
    q_spec = pl.BlockSpec((SEQ, BR_WIDTH), seq)
    kv_spec = pl.BlockSpec((SEQ, KV_WIDTH), seq)
    return pl.pallas_call(
        _ctx_attn_kernel,
        grid=(BATCH,),
        in_specs=[pl.BlockSpec(memory_space=pltpu.SMEM),
                  q_spec, kv_spec, kv_spec, q_spec, kv_spec, kv_spec],
        out_specs=[q_spec, q_spec],
        out_shape=[jax.ShapeDtypeStruct((T_ALL, BR_WIDTH), _BF16)] * 2,
        compiler_params=_params(1),
        name="ctx_attention",
    )(sink, qc, kc, vc, qd, kd, vd)


def _win_attn_kernel(sink_ref, q_ref, kcur_ref, kprev_ref, knext_ref, vcur_ref, vprev_ref,
                     vnext_ref, kctx_ref, vctx_ref, oc_in_ref, oc_ref):
    del oc_in_ref
    j = pl.program_id(1)
    nq = pl.num_programs(1)
    sinks = [sink_ref[h] for h in range(N_HEADS)]
    tq = WIN_TQ
    def rel(n_keys):
        qi = lax.broadcasted_iota(jnp.int32, (2 * tq, n_keys), 0) % tq
        kj = lax.broadcasted_iota(jnp.int32, (2 * tq, n_keys), 1)
        return qi, kj
    qi, kj = rel(tq)
    m_cur = jnp.abs(qi - kj) <= WINDOW
    qi, kj = rel(WINDOW)
    m_prev = ((qi - (kj - WINDOW)) <= WINDOW) & (j > 0)
    m_next = (((kj + tq) - qi) <= WINDOW) & (j < nq - 1)
    chunks = [
        (kcur_ref[...], vcur_ref[...], m_cur),
        (kprev_ref[...], vprev_ref[...], m_prev),
        (knext_ref[...], vnext_ref[...], m_next),
        (kctx_ref[...], vctx_ref[...], None),
    ]
    oc_ref[...] = _attend_once(q_ref[...], chunks, sinks).astype(_BF16)


def _win_attn_call(sink, qc, kc, vc, kctx, vctx, oc):
    nq = DEC_SEQ // WIN_TQ
    ctx_q = T_CTX // WIN_TQ
    sub = WIN_TQ // WINDOW
    ctx_w = T_CTX // WINDOW
    per_seq_w = DEC_SEQ // WINDOW
    cur = lambda b, j: (ctx_q + b * nq + j, 0)
    prev = lambda b, j: (ctx_w + b * per_seq_w + jnp.maximum(j * sub - 1, 0), 0)
    nxt = lambda b, j: (ctx_w + b * per_seq_w + jnp.minimum((j + 1) * sub, per_seq_w - 1), 0)
    cache = lambda b, j: (b, 0, 0)
    return pl.pallas_call(
        _win_attn_kernel,
        grid=(DEC_BATCH, nq),
        in_specs=[
            pl.BlockSpec(memory_space=pltpu.SMEM),
            pl.BlockSpec((WIN_TQ, BR_WIDTH), cur),
            pl.BlockSpec((WIN_TQ, KV_WIDTH), cur),
            pl.BlockSpec((WINDOW, KV_WIDTH), prev),
            pl.BlockSpec((WINDOW, KV_WIDTH), nxt),
            pl.BlockSpec((WIN_TQ, KV_WIDTH), cur),
            pl.BlockSpec((WINDOW, KV_WIDTH), prev),
            pl.BlockSpec((WINDOW, KV_WIDTH), nxt),
            pl.BlockSpec((None, PAST_LEN, KV_WIDTH), cache),
            pl.BlockSpec((None, PAST_LEN, KV_WIDTH), cache),
            pl.BlockSpec(memory_space=pl.ANY),
        ],
        out_specs=pl.BlockSpec((WIN_TQ, BR_WIDTH), cur),
        out_shape=jax.ShapeDtypeStruct((T_ALL, BR_WIDTH), _BF16),
        input_output_aliases={10: 0},
        compiler_params=_params(2),
        name="window_attention",
    )(sink, qc, kc, kc, kc, vc, vc, vc, kctx, vctx, oc)


def _full_attn_kernel(q_ref, k_ref, v_ref, kctx_ref, vctx_ref, od_in_ref, od_ref):
    del od_in_ref
    tq = FULL_TQ
    q = q_ref[...]
    lo, hi = _lane_masks(tq)
    outs = [jnp.zeros((tq, LANES), _F32), jnp.zeros((tq, LANES), _F32)]
    for kv in range(KV_HEADS):
        qs = _group_queries(q, kv)

        def update(carry, k, v):
            m, l, acc = carry
            s = _scores(qs, k)
            m_new = jnp.maximum(m, s.max(axis=-1, keepdims=True))
            a = jnp.exp(m - m_new)
            p = jnp.exp(s - m_new)
            l = a * l + p.sum(axis=-1, keepdims=True)
            acc = a * acc + jnp.dot(p.astype(_BF16), v, preferred_element_type=_F32)
            return m_new, l, acc

        def body(c, carry):
            start = pl.multiple_of(c * FULL_TK, FULL_TK)
            return update(carry, k_ref[pl.ds(start, FULL_TK), :], v_ref[pl.ds(start, FULL_TK), :])

        init = (jnp.full((2 * tq, 1), -jnp.inf, _F32), jnp.zeros((2 * tq, 1), _F32),
                jnp.zeros((2 * tq, LANES), _F32))
        carry = lax.fori_loop(0, DEC_SEQ // FULL_TK, body, init)
        m, l, acc = update(carry, kctx_ref[...], vctx_ref[...])
        o = acc / l
        keep = lo if kv == 0 else hi
        outs[0] = outs[0] + jnp.where(keep, o[0:tq], 0.0)
        outs[1] = outs[1] + jnp.where(keep, o[tq:2 * tq], 0.0)
    od_ref[...] = jnp.concatenate(outs, axis=1).astype(_BF16)


def _full_attn_call(qd, kd, vd, kctx, vctx, od):
    nq = DEC_SEQ // FULL_TQ
    ctx_q = T_CTX // FULL_TQ
    ctx_seq = T_CTX // DEC_SEQ
    cur = lambda b, j: (ctx_q + b * nq + j, 0)
    seq = lambda b, j: (ctx_seq + b, 0)
    cache = lambda b, j: (b, 0, 0)
    return pl.pallas_call(
        _full_attn_kernel,
        grid=(DEC_BATCH, nq),
        in_specs=[
            pl.BlockSpec((FULL_TQ, BR_WIDTH), cur),
            pl.BlockSpec((DEC_SEQ, KV_WIDTH), seq),
            pl.BlockSpec((DEC_SEQ, KV_WIDTH), seq),
            pl.BlockSpec((None, PAST_LEN, KV_WIDTH), cache),
            pl.BlockSpec((None, PAST_LEN, KV_WIDTH), cache),
            pl.BlockSpec(memory_space=pl.ANY),
        ],
        out_specs=pl.BlockSpec((FULL_TQ, BR_WIDTH), cur),
        out_shape=jax.ShapeDtypeStruct((T_ALL, BR_WIDTH), _BF16),
        input_output_aliases={5: 0},
        compiler_params=_params(2),
        name="full_attention",
    )(qd, kd, vd, kctx, vctx, od)


def _merge_kernel(modrow_ref,
                  x_ref, mod_ref, ng_ref, oa_ref, ob_ref, oc_ref, od_ref,
                  wg_ref, wbr_ref, wo_ref, wrh_ref, wrl_ref, br_ref, ltri_ref,
                  h1_ref, n2_ref, topi_ref, topw_ref, rank_ref, cnt_ref, base_ref):
    i = pl.program_id(0)

    @pl.when(i == 0)
    def _():
        base_ref[...] = jnp.zeros_like(base_ref)

    x = x_ref[...]
    mod = mod_ref[...]
    sh1 = mod[:, 0:D_MODEL]
    sc1 = mod[:, D_MODEL:2 * D_MODEL]
    g1 = mod[:, 2 * D_MODEL:3 * D_MODEL]
    sh2 = mod[:, 3 * D_MODEL:4 * D_MODEL]
    sc2 = mod[:, 4 * D_MODEL:5 * D_MODEL]
    ng = ng_ref[...]
    nb = (_rms(x, ng[0:1, :]) * (1.0 + sc1) + sh1).astype(_BF16)
    branches = (oa_ref, ob_ref, oc_ref, od_ref)
    mixed = jnp.zeros((TM, D_MODEL), _F32)
    for k in range(N_BRANCH):
        gate = jax.nn.sigmoid(jnp.dot(nb, wg_ref[:, k * D_MODEL:(k + 1) * D_MODEL],
                                      preferred_element_type=_F32))
        proj = jnp.dot(branches[k][...], wbr_ref[k], preferred_element_type=_F32)
        mixed = mixed + gate * proj
    mix = jnp.dot(mixed.astype(_BF16), wo_ref[...], preferred_element_type=_F32)
    h1 = x + g1 * _rms(mix, ng[1:2, :])
    h1_ref[...] = h1
    n2 = _rms(h1, ng[2:3, :]) * (1.0 + sc2) + sh2
    n2_ref[...] = n2.astype(_BF16)
    logits = _split_dot(n2, wrh_ref[...], wrl_ref[...]) + br_ref[...]
    lane = lax.broadcasted_iota(jnp.int32, (TM, LANES), 1).astype(_F32)
    work = logits
    vals, idxs = [], []
    for _ in range(TOP_K):
        m = work.max(axis=-1, keepdims=True)
        idx = jnp.where(work == m, lane, float(LANES)).min(axis=-1, keepdims=True)
        vals.append(m)
        idxs.append(idx)
        work = jnp.where(lane == idx, -jnp.inf, work)
    es = [jnp.exp(v - vals[0]) for v in vals]
    denom = es[0] + es[1] + es[2] + es[3]
    ltri = ltri_ref[...]
    running = base_ref[0:1, :]
    topi = jnp.zeros((TM, LANES), _F32)
    topw = jnp.zeros((TM, LANES), _F32)
    rank = jnp.zeros((TM, LANES), _F32)
    for k in range(TOP_K):
        onehot = (lane == idxs[k]).astype(_F32)
        before = jnp.dot(ltri, onehot.astype(_BF16), preferred_element_type=_F32)
        r = (onehot * (before + running)).sum(axis=-1, keepdims=True)
        running = running + onehot.sum(axis=0, keepdims=True)
        topi = jnp.where(lane == k, idxs[k], topi)
        topw = jnp.where(lane == k, es[k] / denom, topw)
        rank = jnp.where(lane == k, r, rank)
    base_ref[0:1, :] = running
    topi_ref[...] = topi.astype(jnp.int32)
    topw_ref[...] = topw
    rank_ref[...] = rank.astype(jnp.int32)
    cnt_ref[...] = jnp.broadcast_to(running, (8, LANES)).astype(jnp.int32)


def _merge_call(x, mod3, norm_g, oa, ob, oc, od, wg, wbr, wo, wr_hi, wr_lo, br, ltri, modrow):
    row = lambda i, a: (i, 0)
    full = lambda i, a: (0, 0)
    act = pl.BlockSpec((TM, BR_WIDTH), row)
    wide = pl.BlockSpec((TM, LANES), row)
    grid_spec = pltpu.PrefetchScalarGridSpec(
        num_scalar_prefetch=1,
        grid=(N_TILES,),
        in_specs=[
            pl.BlockSpec((TM, D_MODEL), row),
            pl.BlockSpec((None, 1, 6 * D_MODEL), lambda i, a: (a[i], 0, 0)),
            pl.BlockSpec((4, D_MODEL), full),
            act, act, act, act,
            pl.BlockSpec((D_MODEL, N_BRANCH * D_MODEL), full),
            pl.BlockSpec((N_BRANCH, BR_WIDTH, D_MODEL), lambda i, a: (0, 0, 0)),
            pl.BlockSpec((D_MODEL, D_MODEL), full),
            pl.BlockSpec((D_MODEL, LANES), full),
            pl.BlockSpec((D_MODEL, LANES), full),
            pl.BlockSpec((1, LANES), full),
            pl.BlockSpec((TM, TM), full),
        ],
        out_specs=[
            pl.BlockSpec((TM, D_MODEL), row),
            pl.BlockSpec((TM, D_MODEL), row),
            wide, wide, wide,
            pl.BlockSpec((8, LANES), full),
        ],
        scratch_shapes=[pltpu.VMEM((8, LANES), _F32)],
    )
    return pl.pallas_call(
        _merge_kernel,
        grid_spec=grid_spec,
        out_shape=[
            jax.ShapeDtypeStruct((T_ALL, D_MODEL), _F32),
            jax.ShapeDtypeStruct((T_ALL, D_MODEL), _BF16),
            jax.ShapeDtypeStruct((T_ALL, LANES), jnp.int32),
            jax.ShapeDtypeStruct((T_ALL, LANES), _F32),
            jax.ShapeDtypeStruct((T_ALL, LANES), jnp.int32),
            jax.ShapeDtypeStruct((8, LANES), jnp.int32),
        ],
        compiler_params=_params(1),
        name="merge_router",
    )(modrow, x, mod3, norm_g, oa, ob, oc, od, wg, wbr, wo, wr_hi, wr_lo, br, ltri)


def _expert_kernel(blk_e_ref, nvalid_ref, x_ref, w1_ref, b1_ref, w2_ref, b2_ref, y_ref,
                   w1b_ref, w2b_ref):
    i = pl.program_id(0)
    prev = blk_e_ref[jnp.maximum(i - 1, 0)]
    fresh = (i == 0) | (blk_e_ref[i] != prev)

    @pl.when(fresh)
    def _():
        w1b_ref[...] = w1_ref[...].astype(_BF16)
        w2b_ref[...] = w2_ref[...].astype(_BF16)

    @pl.when(i < nvalid_ref[0])
    def _():
        hh = jnp.dot(x_ref[...], w1b_ref[...], preferred_element_type=_F32) + b1_ref[...]
        x_glu = jnp.minimum(hh[:, 0:D_FF], SWIGLU_LIMIT)
        x_lin = jnp.clip(hh[:, D_FF:2 * D_FF], -SWIGLU_LIMIT, SWIGLU_LIMIT)
        act = x_glu * jax.nn.sigmoid(SWIGLU_ALPHA * x_glu) * (x_lin + 1.0)
        y = jnp.dot(act.astype(_BF16), w2b_ref[...], preferred_element_type=_F32) + b2_ref[...]
        y_ref[...] = y.astype(_BF16)


def _expert_call(blk_e, nvalid, xs, w1, b1, w2, b2):
    rows = lambda i, e, n: (jnp.minimum(i, n[0] - 1), 0)
    wsel = lambda i, e, n: (e[i], 0, 0)
    grid_spec = pltpu.PrefetchScalarGridSpec(
        num_scalar_prefetch=2,
        grid=(N_MOE_BLOCKS,),
        in_specs=[
            pl.BlockSpec((MOE_BLK, D_MODEL), rows),
            pl.BlockSpec((None, D_MODEL, 2 * D_FF), wsel),
            pl.BlockSpec((None, 1, 2 * D_FF), wsel),
            pl.BlockSpec((None, D_FF, D_MODEL), wsel),
            pl.BlockSpec((None, 1, D_MODEL), wsel),
        ],
        out_specs=pl.BlockSpec((MOE_BLK, D_MODEL), rows),
        scratch_shapes=[pltpu.VMEM((D_MODEL, 2 * D_FF), _BF16),
                        pltpu.VMEM((D_FF, D_MODEL), _BF16)],
    )
    return pl.pallas_call(
        _expert_kernel,
        grid_spec=grid_spec,
        out_shape=jax.ShapeDtypeStruct((N_MOE_ROWS, D_MODEL), _BF16),
        compiler_params=_params(1),
        name="experts",
    )(blk_e, nvalid, xs, w1, b1.reshape(N_EXPERTS, 1, 2 * D_FF), w2, b2.reshape(N_EXPERTS, 1, D_MODEL))


def _combine_kernel(modrow_ref, h1_ref, mod_ref, ng_ref, yg_ref, w_ref, o_ref):
    mod = mod_ref[...]
    g2 = mod[:, 5 * D_MODEL:6 * D_MODEL]
    w = w_ref[...]
    f = jnp.zeros((TM, D_MODEL), _F32)
    for k in range(TOP_K):
        f = f + yg_ref[:, k * D_MODEL:(k + 1) * D_MODEL].astype(_F32) * w[:, k:k + 1]
    o_ref[...] = h1_ref[...] + g2 * _rms(f, ng_ref[3:4, :])


def _combine_call(h1, mod3, norm_g, yg, topw, modrow):
    row = lambda i, a: (i, 0)
    grid_spec = pltpu.PrefetchScalarGridSpec(
        num_scalar_prefetch=1,
        grid=(N_TILES,),
        in_specs=[
            pl.BlockSpec((TM, D_MODEL), row),
            pl.BlockSpec((None, 1, 6 * D_MODEL), lambda i, a: (a[i], 0, 0)),
            pl.BlockSpec((4, D_MODEL), lambda i, a: (0, 0)),
            pl.BlockSpec((TM, TOP_K * D_MODEL), row),
            pl.BlockSpec((TM, LANES), row),
        ],
        out_specs=pl.BlockSpec((TM, D_MODEL), row),
    )
    return pl.pallas_call(
        _combine_kernel,
        grid_spec=grid_spec,
        out_shape=jax.ShapeDtypeStruct((T_ALL, D_MODEL), _F32),
        compiler_params=_params(1),
        name="moe_combine",
    )(modrow, h1, mod3, norm_g, yg, topw)


def _tile_tables():
    t = np.arange(N_TILES)
    lat = t >= CTX_TILES
    j = (t - CTX_TILES) % LAT_TILES_PER_SEQ
    modrow = np.where(lat, 1 + (t - CTX_TILES) // LAT_TILES_PER_SEQ, 0)
    ropeblk = np.where(lat, j, LAT_TILES_PER_SEQ)
    stblk = np.where(lat, CTX_TILES, t)
    hasprev = np.where(lat & (j > 0), 1, 0)
    hasnext = np.where(lat & (j < LAT_TILES_PER_SEQ - 1), 1, 0)
    as_i32 = lambda a: jnp.asarray(a, jnp.int32)
    return as_i32(modrow), as_i32(ropeblk), as_i32(stblk), as_i32(hasprev), as_i32(hasnext)


def _rope_tables():
    half = HEAD_DIM // 4
    freqs = ROPE_THETA ** (-jnp.arange(half, dtype=_F32) / half)
    t = jnp.arange(DEC_SEQ)
    pos = jnp.stack([t // GRID_W, t % GRID_W], axis=1).astype(_F32)
    ang = pos[:, :, None] * freqs[None, None, :]
    cos = jnp.cos(ang)
    sin = jnp.sin(ang)
    cos_h = jnp.concatenate([cos, cos], axis=-1).reshape(DEC_SEQ, HEAD_DIM)
    sin_h = jnp.concatenate([-sin, sin], axis=-1).reshape(DEC_SEQ, HEAD_DIM)
    cos_t = jnp.tile(cos_h, (1, N_HEADS))
    sin_t = jnp.tile(sin_h, (1, N_HEADS))
    cos_t = jnp.concatenate([cos_t, jnp.ones((TM, BR_WIDTH), _F32)], axis=0)
    sin_t = jnp.concatenate([sin_t, jnp.zeros((TM, BR_WIDTH), _F32)], axis=0)
    return cos_t, sin_t


def _permute_heads_cols(w):
    lead = w.shape[:-1]
    return w.reshape(lead + (N_HEADS, HEAD_DIM))[..., jnp.array([0, 2, 1, 3]), :].reshape(lead + (BR_WIDTH,))


def _cache_rows(cache, l):
    return cache[:, l].reshape(DEC_BATCH, PAST_LEN, KV_WIDTH).astype(_BF16)


def kernel(x_prompt, x_sample, cache_win_k, cache_win_v, cache_full_k, cache_full_v, c, c_ctx,
           w_mod, b_mod, norm_g, w_in, a_ln_g, a_ln_b, w_sp, b_sp, conv_w, conv_b, b_ln_g, b_ln_b,
           sink, qn_g, kn_g, w_br, w_o, w_router, b_router, w1, b1, w2, b2):
    modrow, ropeblk, stblk, hasprev, hasnext = _tile_tables()
    cos_t, sin_t = _rope_tables()
    head_id = np.arange(BR_WIDTH) // HEAD_DIM
    ones_bd = jnp.asarray(head_id[:, None] == head_id[None, :], _BF16)
    ltri = jnp.asarray(np.tril(np.ones((TM, TM)), -1), _BF16)
    row2 = lambda v: v.reshape(1, -1)

    cvec = jnp.concatenate([c_ctx[None, :], c, jnp.zeros((8 - 1 - DEC_BATCH, D_MODEL), _F32)], axis=0)
    mod_all = _mod_call(cvec, w_mod, b_mod)

    h = jnp.concatenate([x_prompt.reshape(T_CTX, D_MODEL), x_sample.reshape(T_LAT, D_MODEL)], axis=0)
    states = []
    for l in range(DEPTH):
        mod3 = mod_all[l].reshape(8, 1, 6 * D_MODEL)
        wl = w_in[l]
        w_main = jnp.concatenate([
            wl[:, 0:1024],
            _permute_heads_cols(wl[:, 1024:1280]), wl[:, 1280:1536],
            _permute_heads_cols(wl[:, 1536:1792]), wl[:, 1792:2048]], axis=1).astype(_BF16)
        wg = wl[:, MAIN_WIDTH:].astype(_BF16)
        qn = jnp.tile(qn_g[l], N_HEADS).reshape(1, BR_WIDTH)
        kn = jnp.tile(kn_g[l], KV_HEADS).reshape(1, KV_WIDTH)

        ab, qc, kc, vc, qd, kd, vd, st = _inproj_call(
            h, mod3, row2(norm_g[l, 0]), w_main, qn, kn, ones_bd, cos_t, sin_t,
            (modrow, ropeblk, stblk))
        states.append(st[:T_CTX].reshape(BATCH, SEQ, 4, KV_HEADS, HEAD_DIM))

        bsp = jnp.repeat(b_sp[l].T, BR_WIDTH // A_GROUPS, axis=1)
        cw = jnp.concatenate([conv_w[l], jnp.zeros((1, BR_WIDTH), _F32)], axis=0)
        oa, ob = _mix_call(ab, row2(a_ln_g[l]), row2(a_ln_b[l]), w_sp[l].astype(_BF16), bsp, cw,
                           row2(conv_b[l]), row2(b_ln_g[l]), row2(b_ln_b[l]), (hasprev, hasnext))

        oc, od = _ctx_attn_call(sink[l], qc, kc, vc, qd, kd, vd)
        oc = _win_attn_call(sink[l], qc, kc, vc, _cache_rows(cache_win_k, l),
                            _cache_rows(cache_win_v, l), oc)
        od = _full_attn_call(qd, kd, vd, _cache_rows(cache_full_k, l),
                             _cache_rows(cache_full_v, l), od)

        perm = jnp.array([0, 2, 1, 3])
        wbr = w_br[l].reshape(N_BRANCH, N_HEADS, HEAD_DIM, D_MODEL)
        wbr = jnp.concatenate([wbr[0:2], wbr[2:4][:, perm]], axis=0)
        wbr = wbr.reshape(N_BRANCH, BR_WIDTH, D_MODEL).astype(_BF16)
        wr = jnp.concatenate([w_router[l], jnp.zeros((D_MODEL, LANES - N_EXPERTS), _F32)], axis=1)
        wr_hi, wr_lo = _hi_lo(wr)
        br = jnp.concatenate([b_router[l], jnp.full((LANES - N_EXPERTS,), NEG_BIG, _F32)]).reshape(1, LANES)
        h1, n2, topi, topw, rank, cnt = _merge_call(
            h, mod3, norm_g[l], oa, ob, oc, od, wg, wbr, w_o[l].astype(_BF16),
            wr_hi, wr_lo, br, ltri, modrow)

        counts = cnt[0, :N_EXPERTS]
        padded = (counts + MOE_BLK - 1) // MOE_BLK * MOE_BLK
        pad_end = jnp.cumsum(padded)
        pad_start = pad_end - padded
        blk_e = jnp.minimum(jnp.searchsorted(pad_end, jnp.arange(N_MOE_BLOCKS) * MOE_BLK, side="right"),
                            N_EXPERTS - 1).astype(jnp.int32)
        nvalid = (pad_end[-1:] // MOE_BLK).astype(jnp.int32)
        idx4 = topi[:, :TOP_K]
        dest = pad_start[idx4] + rank[:, :TOP_K]
        tok_of_row = jnp.zeros((N_MOE_ROWS,), jnp.int32).at[dest.reshape(-1)].set(
            jnp.arange(T_ALL * TOP_K, dtype=jnp.int32) // TOP_K)
        xs = n2[tok_of_row]
        ys = _expert_call(blk_e, nvalid, xs, w1[l], b1[l], w2[l], b2[l])
        yg = ys[dest.reshape(-1)].reshape(T_ALL, TOP_K * D_MODEL)
        h = _combine_call(h1, mod3, norm_g[l], yg, topw, modrow)

    y_prompt = h[:T_CTX].reshape(BATCH, SEQ, D_MODEL)
    y_sample = h[T_CTX:].reshape(DEC_BATCH, DEC_SEQ, D_MODEL)
    st = jnp.stack(states, axis=1)
    return (y_prompt, y_sample, st[:, :, :, 0], st[:, :, :, 1], st[:, :, :, 2], st[:, :, :, 3])
```

```python
import functools

import numpy as np
import jax
import jax.numpy as jnp
from jax import lax
from jax.experimental import pallas as pl
from jax.experimental.pallas import tpu as pltpu

D_MODEL = 1024
BATCH = 16
SEQ = 256
DEPTH = 2
DEC_BATCH = 4
DEC_SEQ = 4096
PAST_LEN = 256
GRID_W = 64
HEAD_DIM = 64
BR_WIDTH = 256
N_BRANCH = 4
A_GROUPS = 4
CHUNK = 128
CONV_W = 31
N_HEADS = 4
KV_HEADS = 2
KV_WIDTH = KV_HEADS * HEAD_DIM
WINDOW = 128
ROPE_THETA = 10000.0
EPS = 1e-6
N_EXPERTS = 32
TOP_K = 4
D_FF = D_MODEL
SWIGLU_LIMIT = 7.0
SWIGLU_ALPHA = 1.702
MAIN_WIDTH = 2048

LANES = 128
T_CTX = BATCH * SEQ
T_LAT = DEC_BATCH * DEC_SEQ
T_ALL = T_CTX + T_LAT
TM = 256
N_TILES = T_ALL // TM
CTX_TILES = T_CTX // TM
LAT_TILES_PER_SEQ = DEC_SEQ // TM
HALO = 16
MOE_BLK = 256
N_SLOTS = T_ALL * TOP_K
N_MOE_BLOCKS = N_SLOTS // MOE_BLK + N_EXPERTS
N_MOE_ROWS = N_MOE_BLOCKS * MOE_BLK
WIN_TQ = 512
FULL_TQ = 256
FULL_TK = 512
NEG_BIG = -1e30
VMEM_LIMIT = 56 * 1024 * 1024

_F32 = jnp.float32
_BF16 = jnp.bfloat16


def _params(n_axes, vmem=None):
    return pltpu.CompilerParams(
        dimension_semantics=("arbitrary",) * n_axes,
        vmem_limit_bytes=vmem if vmem is not None else VMEM_LIMIT)


def _split_dot(a, b_hi, b_lo):
    a_hi = a.astype(_BF16)
    a_lo = (a - a_hi.astype(_F32)).astype(_BF16)
    return (jnp.dot(a_hi, b_hi, preferred_element_type=_F32)
            + jnp.dot(a_lo, b_hi, preferred_element_type=_F32)
            + jnp.dot(a_hi, b_lo, preferred_element_type=_F32))


def _hi_lo(w):
    hi = w.astype(_BF16)
    lo = (w - hi.astype(_F32)).astype(_BF16)
    return hi, lo


def _mod_kernel(c_ref, w_ref, b_ref, o_ref):
    c = c_ref[...]
    s = c * jax.nn.sigmoid(c)
    w = w_ref[...]
    w_hi = w.astype(_BF16)
    w_lo = (w - w_hi.astype(_F32)).astype(_BF16)
    o_ref[...] = _split_dot(s, w_hi, w_lo) + b_ref[...]


def _mod_call(cvec, w_mod, b_mod):
    tn = 1024
    return pl.pallas_call(
        _mod_kernel,
        grid=(DEPTH, 6 * D_MODEL // tn),
        in_specs=[
            pl.BlockSpec((8, D_MODEL), lambda l, j: (0, 0)),
            pl.BlockSpec((None, D_MODEL, tn), lambda l, j: (l, 0, j)),
            pl.BlockSpec((None, 1, tn), lambda l, j: (l, 0, j)),
        ],
        out_specs=pl.BlockSpec((None, 8, tn), lambda l, j: (l, 0, j)),
        out_shape=jax.ShapeDtypeStruct((DEPTH, 8, 6 * D_MODEL), _F32),
        compiler_params=_params(2),
        name="mod",
    )(cvec, w_mod, b_mod.reshape(DEPTH, 1, 6 * D_MODEL))


def _rms(x, g):
    ms = jnp.mean(x * x, axis=-1, keepdims=True)
    return x * lax.rsqrt(ms + EPS) * g


def _head_rms(x, ones_bd, g):
    xx = x * x
    hi = xx.astype(_BF16)
    lo = (xx - hi.astype(_F32)).astype(_BF16)
    ss = (jnp.dot(hi, ones_bd, preferred_element_type=_F32)
          + jnp.dot(lo, ones_bd, preferred_element_type=_F32))
    return x * lax.rsqrt(ss * (1.0 / HEAD_DIM) + EPS) * g


def _rope(x, cos, sin_signed):
    w = x.shape[-1]
    lane = lax.broadcasted_iota(jnp.int32, x.shape, 1)
    first = (lane % 32) < 16
    partner = jnp.where(first, pltpu.roll(x, w - 16, 1), pltpu.roll(x, 16, 1))
    return x * cos + partner * sin_signed


def _inproj_kernel(modrow_ref, ropeblk_ref, stblk_ref,
                   x_ref, mod_ref, g0_ref, w_ref, qn_ref, kn_ref, ones_ref,
                   cos_ref, sin_ref,
                   ab_ref, qc_ref, kc_ref, vc_ref, qd_ref, kd_ref, vd_ref, st_ref):
    x = x_ref[...]
    mod = mod_ref[...]
    sh1 = mod[:, 0:D_MODEL]
    sc1 = mod[:, D_MODEL:2 * D_MODEL]
    n = _rms(x, g0_ref[...]) * (1.0 + sc1) + sh1
    z = jnp.dot(n.astype(_BF16), w_ref[...], preferred_element_type=_F32)
    ab_ref[...] = z[:, 0:1024]
    cq = z[:, 1024:1280]
    ck = z[:, 1280:1408]
    cv = z[:, 1408:1536]
    dq = z[:, 1536:1792]
    dk = z[:, 1792:1920]
    dv = z[:, 1920:2048]
    ones = ones_ref[...]
    dq = _head_rms(dq, ones, qn_ref[...])
    dk = _head_rms(dk, ones[0:KV_WIDTH, 0:KV_WIDTH], kn_ref[...])
    cos = cos_ref[...]
    sin = sin_ref[...]
    scale = HEAD_DIM ** -0.5
    qc_ref[...] = (_rope(cq, cos, sin) * scale).astype(_BF16)
    qd_ref[...] = (_rope(dq, cos, sin) * scale).astype(_BF16)
    kc = _rope(ck, cos[:, 0:KV_WIDTH], sin[:, 0:KV_WIDTH])
    kd = _rope(dk, cos[:, 0:KV_WIDTH], sin[:, 0:KV_WIDTH])
    kc_ref[...] = kc.astype(_BF16)
    kd_ref[...] = kd.astype(_BF16)
    vc_ref[...] = cv.astype(_BF16)
    vd_ref[...] = dv.astype(_BF16)
    st_ref[:, 0:128] = kc
    st_ref[:, 128:256] = cv
    st_ref[:, 256:384] = kd
    st_ref[:, 384:512] = dv


def _inproj_call(x, mod3, g0, w_main, qn, kn, ones_bd, cos_t, sin_t, tables):
    modrow, ropeblk, stblk = tables
    row = lambda i, a, b, c: (i, 0)
    full = lambda i, a, b, c: (0, 0)
    bf = lambda w: jax.ShapeDtypeStruct((T_ALL, w), _BF16)
    grid_spec = pltpu.PrefetchScalarGridSpec(
        num_scalar_prefetch=3,
        grid=(N_TILES,),
        in_specs=[
            pl.BlockSpec((TM, D_MODEL), row),
            pl.BlockSpec((None, 1, 6 * D_MODEL), lambda i, a, b, c: (a[i], 0, 0)),
            pl.BlockSpec((1, D_MODEL), full),
            pl.BlockSpec((D_MODEL, MAIN_WIDTH), full),
            pl.BlockSpec((1, BR_WIDTH), full),
            pl.BlockSpec((1, KV_WIDTH), full),
            pl.BlockSpec((BR_WIDTH, BR_WIDTH), full),
            pl.BlockSpec((TM, BR_WIDTH), lambda i, a, b, c: (b[i], 0)),
            pl.BlockSpec((TM, BR_WIDTH), lambda i, a, b, c: (b[i], 0)),
        ],
        out_specs=[
            pl.BlockSpec((TM, 1024), row),
            pl.BlockSpec((TM, BR_WIDTH), row),
            pl.BlockSpec((TM, KV_WIDTH), row),
            pl.BlockSpec((TM, KV_WIDTH), row),
            pl.BlockSpec((TM, BR_WIDTH), row),
            pl.BlockSpec((TM, KV_WIDTH), row),
            pl.BlockSpec((TM, KV_WIDTH), row),
            pl.BlockSpec((TM, 512), lambda i, a, b, c: (c[i], 0)),
        ],
    )
    return pl.pallas_call(
        _inproj_kernel,
        grid_spec=grid_spec,
        out_shape=[
            jax.ShapeDtypeStruct((T_ALL, 1024), _F32),
            bf(BR_WIDTH), bf(KV_WIDTH), bf(KV_WIDTH),
            bf(BR_WIDTH), bf(KV_WIDTH), bf(KV_WIDTH),
            jax.ShapeDtypeStruct((T_CTX + TM, 512), _F32),
        ],
        compiler_params=_params(1),
        name="inproj",
    )(modrow, ropeblk, stblk, x, mod3, g0, w_main, qn, kn, ones_bd, cos_t, sin_t)


def _layernorm(x, g, b):
    mu = jnp.mean(x, axis=-1, keepdims=True)
    xc = x - mu
    var = jnp.mean(xc * xc, axis=-1, keepdims=True)
    return xc * lax.rsqrt(var + EPS) * g + b


def _gelu(x):
    return 0.5 * x * (1.0 + lax.erf(x * (2.0 ** -0.5)))


def _mix_kernel(hasprev_ref, hasnext_ref,
                ab_ref, prev_ref, next_ref, alg_ref, alb_ref, wsp_ref, bsp_ref,
                cw_ref, cb_ref, blg_ref, blb_ref,
                oa_ref, ob_ref, buf_ref):
    i = pl.program_id(0)
    ab = ab_ref[...]
    u = _gelu(ab[:, 0:256])
    va = _layernorm(_gelu(ab[:, 256:512]), alg_ref[...], alb_ref[...])
    vab = va.astype(_BF16)
    lane = lax.broadcasted_iota(jnp.int32, (CHUNK, BR_WIDTH), 1)
    group = lane // (BR_WIDTH // A_GROUPS)
    bsp = bsp_ref[...]
    for c in range(TM // CHUNK):
        v_c = vab[c * CHUNK:(c + 1) * CHUNK, :]
        sp = bsp
        for g in range(A_GROUPS):
            full = jnp.dot(wsp_ref[g], v_c, preferred_element_type=_F32)
            sp = sp + jnp.where(group == g, full, 0.0)
        oa_ref[c * CHUNK:(c + 1) * CHUNK, :] = (u[c * CHUNK:(c + 1) * CHUNK, :] * sp).astype(_BF16)
    glu = ab[:, 512:768] * jax.nn.sigmoid(ab[:, 768:1024])
    pv = prev_ref[...]
    nx = next_ref[...]
    hp = hasprev_ref[i].astype(_F32)
    hn = hasnext_ref[i].astype(_F32)
    buf_ref[0:HALO, :] = pv[:, 0:256] * jax.nn.sigmoid(pv[:, 256:512]) * hp
    buf_ref[HALO:HALO + TM, :] = glu
    buf_ref[HALO + TM:HALO + TM + HALO, :] = nx[:, 0:256] * jax.nn.sigmoid(nx[:, 256:512]) * hn
    cw = cw_ref[...]
    acc = jnp.zeros((TM, BR_WIDTH), _F32) + cb_ref[...]
    off = HALO - CONV_W // 2
    for k in range(CONV_W):
        acc = acc + buf_ref[off + k:off + k + TM, :] * cw[k:k + 1, :]
    y = _layernorm(acc, blg_ref[...], blb_ref[...])
    ob_ref[...] = (y * jax.nn.sigmoid(y)).astype(_BF16)


def _mix_call(ab, a_ln_g, a_ln_b, wsp, bsp, conv_w, conv_b, b_ln_g, b_ln_b, tables):
    hasprev, hasnext = tables
    n_halo_blocks = T_ALL // HALO
    per = TM // HALO
    full2 = lambda i, a, b: (0, 0)
    grid_spec = pltpu.PrefetchScalarGridSpec(
        num_scalar_prefetch=2,
        grid=(N_TILES,),
        in_specs=[
            pl.BlockSpec((TM, 1024), lambda i, a, b: (i, 0)),
            pl.BlockSpec((HALO, 512), lambda i, a, b: (jnp.maximum(i * per - 1, 0), 1)),
            pl.BlockSpec((HALO, 512), lambda i, a, b: (jnp.minimum((i + 1) * per, n_halo_blocks - 1), 1)),
            pl.BlockSpec((1, BR_WIDTH), full2),
            pl.BlockSpec((1, BR_WIDTH), full2),
            pl.BlockSpec((A_GROUPS, CHUNK, CHUNK), lambda i, a, b: (0, 0, 0)),
            pl.BlockSpec((CHUNK, BR_WIDTH), full2),
            pl.BlockSpec((32, BR_WIDTH), full2),
            pl.BlockSpec((1, BR_WIDTH), full2),
            pl.BlockSpec((1, BR_WIDTH), full2),
            pl.BlockSpec((1, BR_WIDTH), full2),
        ],
        out_specs=[
            pl.BlockSpec((TM, BR_WIDTH), lambda i, a, b: (i, 0)),
            pl.BlockSpec((TM, BR_WIDTH), lambda i, a, b: (i, 0)),
        ],
        scratch_shapes=[pltpu.VMEM((TM + 2 * HALO, BR_WIDTH), _F32)],
    )
    return pl.pallas_call(
        _mix_kernel,
        grid_spec=grid_spec,
        out_shape=[jax.ShapeDtypeStruct((T_ALL, BR_WIDTH), _BF16)] * 2,
        compiler_params=_params(1),
        name="mixers_ab",
    )(hasprev, hasnext, ab, ab, ab, a_ln_g, a_ln_b, wsp, bsp, conv_w, conv_b, b_ln_g, b_ln_b)


def _lane_masks(n_rows):
    lane = lax.broadcasted_iota(jnp.int32, (n_rows, LANES), 1)
    return lane < HEAD_DIM, lane >= HEAD_DIM


def _group_queries(q, kv):
    tq = q.shape[0]
    lo, hi = _lane_masks(tq)
    keep = lo if kv == 0 else hi
    zero = jnp.zeros((), q.dtype)
    return jnp.concatenate([jnp.where(keep, q[:, 0:LANES], zero),
                            jnp.where(keep, q[:, LANES:2 * LANES], zero)], axis=0)


def _scores(qs, k):
    return lax.dot_general(qs, k, (((1,), (1,)), ((), ())), preferred_element_type=_F32)


def _attend_once(q, chunks, sinks):
    tq = q.shape[0]
    lo, hi = _lane_masks(tq)
    outs = [jnp.zeros((tq, LANES), _F32), jnp.zeros((tq, LANES), _F32)]
    for kv in range(KV_HEADS):
        qs = _group_queries(q, kv)
        ss = []
        for k, _, mask in chunks:
            s = _scores(qs, k)
            if mask is not None:
                s = jnp.where(mask, s, -jnp.inf)
            ss.append(s)
        m = ss[0].max(axis=-1, keepdims=True)
        for s in ss[1:]:
            m = jnp.maximum(m, s.max(axis=-1, keepdims=True))
        if sinks is not None:
            row = lax.broadcasted_iota(jnp.int32, (2 * tq, 1), 0)
            sink_col = jnp.where(row < tq, sinks[2 * kv], sinks[2 * kv + 1])
            m = jnp.maximum(m, sink_col)
            l = jnp.exp(sink_col - m)
        else:
            l = jnp.zeros((2 * tq, 1), _F32)
        acc = jnp.zeros((2 * tq, LANES), _F32)
        for s, (_, v, _) in zip(ss, chunks):
            p = jnp.exp(s - m)
            l = l + p.sum(axis=-1, keepdims=True)
            acc = acc + jnp.dot(p.astype(_BF16), v, preferred_element_type=_F32)
        o = acc / l
        keep = lo if kv == 0 else hi
        outs[0] = outs[0] + jnp.where(keep, o[0:tq], 0.0)
        outs[1] = outs[1] + jnp.where(keep, o[tq:2 * tq], 0.0)
    return jnp.concatenate(outs, axis=1)


def _ctx_attn_kernel(sink_ref, qc_ref, kc_ref, vc_ref, qd_ref, kd_ref, vd_ref, oc_ref, od_ref):
    sinks = [sink_ref[h] for h in range(N_HEADS)]
    oc_ref[...] = _attend_once(qc_ref[...], [(kc_ref[...], vc_ref[...], None)], sinks).astype(_BF16)
    od_ref[...] = _attend_once(qd_ref[...], [(kd_ref[...], vd_ref[...], None)], None).astype(_BF16)


def _ctx_attn_call(sink, qc, kc, vc, qd, kd, vd):
    seq = lambda b: (b, 0)
    q_spec = pl.BlockSpec((SEQ, BR_WIDTH), seq)
    kv_spec = pl.BlockSpec((SEQ, KV_WIDTH), seq)
    return pl.pallas_call(
        _ctx_attn_kernel,
        grid=(BATCH,),
        in_specs=[pl.BlockSpec(memory_space=pltpu.SMEM),
                  q_spec, kv_spec, kv_spec, q_spec, kv_spec, kv_spec],
        out_specs=[q_spec, q_spec],
        out_shape=[jax.ShapeDtypeStruct((T_CTX, BR_WIDTH), _BF16)] * 2,
        compiler_params=_params(1),
        name="ctx_attention",
    )(sink, qc, kc, vc, qd, kd, vd)


def _win_attn_kernel(sink_ref, q_ref, kcur_ref, kprev_ref, knext_ref, vcur_ref, vprev_ref,
                     vnext_ref, kctx_ref, vctx_ref, oc_ref):
    j = pl.program_id(1)
    nq = pl.num_programs(1)
    sinks = [sink_ref[h] for h in range(N_HEADS)]
    tq = WIN_TQ
    def rel(n_keys):
        qi = lax.broadcasted_iota(jnp.int32, (2 * tq, n_keys), 0) % tq
        kj = lax.broadcasted_iota(jnp.int32, (2 * tq, n_keys), 1)
        return qi, kj
    qi, kj = rel(tq)
    m_cur = jnp.abs(qi - kj) <= WINDOW
    qi, kj = rel(WINDOW)
    m_prev = ((qi - (kj - WINDOW)) <= WINDOW) & (j > 0)
    m_next = (((kj + tq) - qi) <= WINDOW) & (j < nq - 1)
    chunks = [
        (kcur_ref[...], vcur_ref[...], m_cur),
        (kprev_ref[...], vprev_ref[...], m_prev),
        (knext_ref[...], vnext_ref[...], m_next),
        (kctx_ref[...], vctx_ref[...], None),
    ]
    oc_ref[...] = _attend_once(q_ref[...], chunks, sinks).astype(_BF16)


def _win_attn_call(sink, qc, kc, vc, kctx, vctx):
    nq = DEC_SEQ // WIN_TQ
    ctx_q = T_CTX // WIN_TQ
    sub = WIN_TQ // WINDOW
    ctx_w = T_CTX // WINDOW
    per_seq_w = DEC_SEQ // WINDOW
    cur = lambda b, j: (ctx_q + b * nq + j, 0)
    prev = lambda b, j: (ctx_w + b * per_seq_w + jnp.maximum(j * sub - 1, 0), 0)
    nxt = lambda b, j: (ctx_w + b * per_seq_w + jnp.minimum((j + 1) * sub, per_seq_w - 1), 0)
    cache = lambda b, j: (b, 0, 0)
    return pl.pallas_call(
        _win_attn_kernel,
        grid=(DEC_BATCH, nq),
        in_specs=[
            pl.BlockSpec(memory_space=pltpu.SMEM),
            pl.BlockSpec((WIN_TQ, BR_WIDTH), cur),
            pl.BlockSpec((WIN_TQ, KV_WIDTH), cur),
            pl.BlockSpec((WINDOW, KV_WIDTH), prev),
            pl.BlockSpec((WINDOW, KV_WIDTH), nxt),
            pl.BlockSpec((WIN_TQ, KV_WIDTH), cur),
            pl.BlockSpec((WINDOW, KV_WIDTH), prev),
            pl.BlockSpec((WINDOW, KV_WIDTH), nxt),
            pl.BlockSpec((None, PAST_LEN, KV_WIDTH), cache),
            pl.BlockSpec((None, PAST_LEN, KV_WIDTH), cache),
        ],
        out_specs=pl.BlockSpec((WIN_TQ, BR_WIDTH), lambda b, j: (b * nq + j, 0)),
        out_shape=jax.ShapeDtypeStruct((T_LAT, BR_WIDTH), _BF16),
        compiler_params=_params(2),
        name="window_attention",
    )(sink, qc, kc, kc, kc, vc, vc, vc, kctx, vctx)


def _full_attn_kernel(q_ref, k_ref, v_ref, kctx_ref, vctx_ref, od_ref):
    tq = FULL_TQ
    q = q_ref[...]
    lo, hi = _lane_masks(tq)
    outs = [jnp.zeros((tq, LANES), _F32), jnp.zeros((tq, LANES), _F32)]
    for kv in range(KV_HEADS):
        qs = _group_queries(q, kv)

        def update(carry, k, v):
            m, l, acc = carry
            s = _scores(qs, k)
            m_new = jnp.maximum(m, s.max(axis=-1, keepdims=True))
            a = jnp.exp(m - m_new)
            p = jnp.exp(s - m_new)
            l = a * l + p.sum(axis=-1, keepdims=True)
            acc = a * acc + jnp.dot(p.astype(_BF16), v, preferred_element_type=_F32)
            return m_new, l, acc

        def body(c, carry):
            start = pl.multiple_of(c * FULL_TK, FULL_TK)
            return update(carry, k_ref[pl.ds(start, FULL_TK), :], v_ref[pl.ds(start, FULL_TK), :])

        init = (jnp.full((2 * tq, 1), -jnp.inf, _F32), jnp.zeros((2 * tq, 1), _F32),
                jnp.zeros((2 * tq, LANES), _F32))
        carry = lax.fori_loop(0, DEC_SEQ // FULL_TK, body, init)
        m, l, acc = update(carry, kctx_ref[...], vctx_ref[...])
        o = acc / l
        keep = lo if kv == 0 else hi
        outs[0] = outs[0] + jnp.where(keep, o[0:tq], 0.0)
        outs[1] = outs[1] + jnp.where(keep, o[tq:2 * tq], 0.0)
    od_ref[...] = jnp.concatenate(outs, axis=1).astype(_BF16)


def _full_attn_call(qd, kd, vd, kctx, vctx):
    nq = DEC_SEQ // FULL_TQ
    ctx_q = T_CTX // FULL_TQ
    ctx_seq = T_CTX // DEC_SEQ
    cur = lambda b, j: (ctx_q + b * nq + j, 0)
    seq = lambda b, j: (ctx_seq + b, 0)
    cache = lambda b, j: (b, 0, 0)
    return pl.pallas_call(
        _full_attn_kernel,
        grid=(DEC_BATCH, nq),
        in_specs=[
            pl.BlockSpec((FULL_TQ, BR_WIDTH), cur),
            pl.BlockSpec((DEC_SEQ, KV_WIDTH), seq),
            pl.BlockSpec((DEC_SEQ, KV_WIDTH), seq),
            pl.BlockSpec((None, PAST_LEN, KV_WIDTH), cache),
            pl.BlockSpec((None, PAST_LEN, KV_WIDTH), cache),
        ],
        out_specs=pl.BlockSpec((FULL_TQ, BR_WIDTH), lambda b, j: (b * nq + j, 0)),
        out_shape=jax.ShapeDtypeStruct((T_LAT, BR_WIDTH), _BF16),
        compiler_params=_params(2),
        name="full_attention",
    )(qd, kd, vd, kctx, vctx)


def _merge_kernel(modrow_ref,
                  x_ref, mod_ref, ng_ref, oa_ref, ob_ref, occ_ref, ocl_ref, odc_ref, odl_ref,
                  wg_ref, wbr_ref, wo_ref, wrh_ref, wrl_ref, br_ref, ltri_ref,
                  h1_ref, n2_ref, topi_ref, topw_ref, rank_ref, cnt_ref, base_ref):
    i = pl.program_id(0)

    @pl.when(i == 0)
    def _():
        base_ref[...] = jnp.zeros_like(base_ref)

    x = x_ref[...]
    mod = mod_ref[...]
    sh1 = mod[:, 0:D_MODEL]
    sc1 = mod[:, D_MODEL:2 * D_MODEL]
    g1 = mod[:, 2 * D_MODEL:3 * D_MODEL]
    sh2 = mod[:, 3 * D_MODEL:4 * D_MODEL]
    sc2 = mod[:, 4 * D_MODEL:5 * D_MODEL]
    ng = ng_ref[...]
    nb = (_rms(x, ng[0:1, :]) * (1.0 + sc1) + sh1).astype(_BF16)
    is_ctx = i < CTX_TILES
    oc = jnp.where(is_ctx, occ_ref[...], ocl_ref[...])
    od = jnp.where(is_ctx, odc_ref[...], odl_ref[...])
    branches = (oa_ref[...], ob_ref[...], oc, od)
    mixed = jnp.zeros((TM, D_MODEL), _F32)
    for k in range(N_BRANCH):
        gate = jax.nn.sigmoid(jnp.dot(nb, wg_ref[:, k * D_MODEL:(k + 1) * D_MODEL],
                                      preferred_element_type=_F32))
        proj = jnp.dot(branches[k], wbr_ref[k], preferred_element_type=_F32)
        mixed = mixed + gate * proj
    mix = jnp.dot(mixed.astype(_BF16), wo_ref[...], preferred_element_type=_F32)
    h1 = x + g1 * _rms(mix, ng[1:2, :])
    h1_ref[...] = h1
    n2 = _rms(h1, ng[2:3, :]) * (1.0 + sc2) + sh2
    n2_ref[...] = n2
    logits = _split_dot(n2, wrh_ref[...], wrl_ref[...]) + br_ref[...]
    lane = lax.broadcasted_iota(jnp.int32, (TM, LANES), 1).astype(_F32)
    work = logits
    vals, idxs = [], []
    for _ in range(TOP_K):
        m = work.max(axis=-1, keepdims=True)
        idx = jnp.where(work == m, lane, float(LANES)).min(axis=-1, keepdims=True)
        vals.append(m)
        idxs.append(idx)
        work = jnp.where(lane == idx, -jnp.inf, work)
    es = [jnp.exp(v - vals[0]) for v in vals]
    denom = es[0] + es[1] + es[2] + es[3]
    ltri = ltri_ref[...]
    running = base_ref[0:1, :]
    topi = jnp.zeros((TM, LANES), _F32)
    topw = jnp.zeros((TM, LANES), _F32)
    rank = jnp.zeros((TM, LANES), _F32)
    for k in range(TOP_K):
        onehot = (lane == idxs[k]).astype(_F32)
        before = jnp.dot(ltri, onehot.astype(_BF16), preferred_element_type=_F32)
        r = (onehot * (before + running)).sum(axis=-1, keepdims=True)
        running = running + onehot.sum(axis=0, keepdims=True)
        topi = jnp.where(lane == k, idxs[k], topi)
        topw = jnp.where(lane == k, es[k] / denom, topw)
        rank = jnp.where(lane == k, r, rank)
    base_ref[0:1, :] = running
    topi_ref[...] = topi.astype(jnp.int32)
    topw_ref[...] = topw
    rank_ref[...] = rank.astype(jnp.int32)
    cnt_ref[...] = jnp.broadcast_to(running, (8, LANES)).astype(jnp.int32)


def _merge_call(x, mod3, norm_g, oa, ob, occ, ocl, odc, odl, wg, wbr, wo, wr_hi, wr_lo, br, ltri, modrow):
    row = lambda i, a: (i, 0)
    full = lambda i, a: (0, 0)
    act = pl.BlockSpec((TM, BR_WIDTH), row)
    act_ctx = pl.BlockSpec((TM, BR_WIDTH), lambda i, a: (jnp.minimum(i, CTX_TILES - 1), 0))
    act_lat = pl.BlockSpec((TM, BR_WIDTH), lambda i, a: (jnp.maximum(i - CTX_TILES, 0), 0))
    wide = pl.BlockSpec((TM, LANES), row)
    grid_spec = pltpu.PrefetchScalarGridSpec(
        num_scalar_prefetch=1,
        grid=(N_TILES,),
        in_specs=[
            pl.BlockSpec((TM, D_MODEL), row),
            pl.BlockSpec((None, 1, 6 * D_MODEL), lambda i, a: (a[i], 0, 0)),
            pl.BlockSpec((4, D_MODEL), full),
            act, act, act_ctx, act_lat, act_ctx, act_lat,
            pl.BlockSpec((D_MODEL, N_BRANCH * D_MODEL), full),
            pl.BlockSpec((N_BRANCH, BR_WIDTH, D_MODEL), lambda i, a: (0, 0, 0)),
            pl.BlockSpec((D_MODEL, D_MODEL), full),
            pl.BlockSpec((D_MODEL, LANES), full),
            pl.BlockSpec((D_MODEL, LANES), full),
            pl.BlockSpec((1, LANES), full),
            pl.BlockSpec((TM, TM), full),
        ],
        out_specs=[
            pl.BlockSpec((TM, D_MODEL), row),
            pl.BlockSpec((TM, D_MODEL), row),
            wide, wide, wide,
            pl.BlockSpec((8, LANES), full),
        ],
        scratch_shapes=[pltpu.VMEM((8, LANES), _F32)],
    )
    return pl.pallas_call(
        _merge_kernel,
        grid_spec=grid_spec,
        out_shape=[
            jax.ShapeDtypeStruct((T_ALL, D_MODEL), _F32),
            jax.ShapeDtypeStruct((T_ALL, D_MODEL), _F32),
            jax.ShapeDtypeStruct((T_ALL, LANES), jnp.int32),
            jax.ShapeDtypeStruct((T_ALL, LANES), _F32),
            jax.ShapeDtypeStruct((T_ALL, LANES), jnp.int32),
            jax.ShapeDtypeStruct((8, LANES), jnp.int32),
        ],
        compiler_params=_params(1),
        name="merge_router",
    )(modrow, x, mod3, norm_g, oa, ob, occ, ocl, odc, odl, wg, wbr, wo, wr_hi, wr_lo, br, ltri)


def _expert_kernel(blk_e_ref, nvalid_ref,
                   tokc_ref, tokn_ref, slot_ref, n2_hbm, w1_ref, b1_ref, w2_ref, b2_ref,
                   y_hbm,
                   w1b_ref, w2b_ref, xbuf, ybuf, gsem, ssem):
    i = pl.program_id(0)
    nb = pl.num_programs(0)
    s = i % 2
    nvalid = nvalid_ref[0]

    def gather_copy(tok, r, buf):
        return pltpu.make_async_copy(n2_hbm.at[pl.ds(tok, 1), :], xbuf.at[buf, pl.ds(r, 1), :],
                                     gsem.at[buf])

    def scatter_copy(slot, r, buf):
        return pltpu.make_async_copy(ybuf.at[buf, pl.ds(r, 1), :], y_hbm.at[pl.ds(slot, 1), :],
                                     ssem.at[buf])

    @pl.when(i == 0)
    def _():
        for r in range(MOE_BLK):
            gather_copy(tokc_ref[0, r], r, 0).start()

    @pl.when(i + 1 < nvalid)
    def _():
        for r in range(MOE_BLK):
            gather_copy(tokn_ref[0, r], r, 1 - s).start()

    prev = blk_e_ref[jnp.maximum(i - 1, 0)]
    fresh = (i == 0) | (blk_e_ref[i] != prev)

    @pl.when(fresh)
    def _():
        w1b_ref[...] = w1_ref[...].astype(_BF16)
        w2b_ref[...] = w2_ref[...].astype(_BF16)

    @pl.when(i < nvalid)
    def _():
        for r in range(MOE_BLK):
            gather_copy(0, r, s).wait()
        x = xbuf[s].astype(_BF16)
        hh = jnp.dot(x, w1b_ref[...], preferred_element_type=_F32) + b1_ref[...]
        x_glu = jnp.minimum(hh[:, 0:D_FF], SWIGLU_LIMIT)
        x_lin = jnp.clip(hh[:, D_FF:2 * D_FF], -SWIGLU_LIMIT, SWIGLU_LIMIT)
        act = x_glu * jax.nn.sigmoid(SWIGLU_ALPHA * x_glu) * (x_lin + 1.0)
        ybuf[s] = jnp.dot(act.astype(_BF16), w2b_ref[...], preferred_element_type=_F32) + b2_ref[...]

    @pl.when(i >= nvalid)
    def _():
        ybuf[s] = jnp.zeros((MOE_BLK, D_MODEL), _F32)

    @pl.when(i >= 1)
    def _():
        for r in range(MOE_BLK):
            scatter_copy(0, r, 1 - s).wait()

    for r in range(MOE_BLK):
        scatter_copy(slot_ref[0, r], r, s).start()

    @pl.when(i == nb - 1)
    def _():
        for r in range(MOE_BLK):
            scatter_copy(0, r, s).wait()


def _expert_call(l, blk_e, nvalid, tok_of_row, slot_of_row, n2, w1, b1, w2, b2):
    wsel = lambda i, e, n: (l, e[i], 0, 0)
    cur = lambda i, e, n: (i, 0, 0)
    nxt = lambda i, e, n: (jnp.minimum(i + 1, N_MOE_BLOCKS - 1), 0, 0)
    smem_blk = lambda imap: pl.BlockSpec((None, 1, MOE_BLK), imap, memory_space=pltpu.SMEM)
    tok3 = tok_of_row.reshape(N_MOE_BLOCKS, 1, MOE_BLK)
    slot3 = slot_of_row.reshape(N_MOE_BLOCKS, 1, MOE_BLK)
    grid_spec = pltpu.PrefetchScalarGridSpec(
        num_scalar_prefetch=2,
        grid=(N_MOE_BLOCKS,),
        in_specs=[
            smem_blk(cur), smem_blk(nxt), smem_blk(cur),
            pl.BlockSpec(memory_space=pl.ANY),
            pl.BlockSpec((None, None, D_MODEL, 2 * D_FF), wsel),
            pl.BlockSpec((None, None, 1, 2 * D_FF), wsel),
            pl.BlockSpec((None, None, D_FF, D_MODEL), wsel),
            pl.BlockSpec((None, None, 1, D_MODEL), wsel),
        ],
        out_specs=pl.BlockSpec(memory_space=pl.ANY),
        scratch_shapes=[pltpu.VMEM((D_MODEL, 2 * D_FF), _BF16),
                        pltpu.VMEM((D_FF, D_MODEL), _BF16),
                        pltpu.VMEM((2, MOE_BLK, D_MODEL), _F32),
                        pltpu.VMEM((2, MOE_BLK, D_MODEL), _F32),
                        pltpu.SemaphoreType.DMA((2,)),
                        pltpu.SemaphoreType.DMA((2,))],
    )
    return pl.pallas_call(
        _expert_kernel,
        grid_spec=grid_spec,
        out_shape=jax.ShapeDtypeStruct((N_MOE_ROWS, D_MODEL), _F32),
        compiler_params=_params(1),
        name="experts",
    )(blk_e, nvalid, tok3, tok3, slot3, n2, w1,
      b1.reshape(DEPTH, N_EXPERTS, 1, 2 * D_FF), w2, b2.reshape(DEPTH, N_EXPERTS, 1, D_MODEL))


def _combine_kernel(modrow_ref, h1_ref, mod_ref, ng_ref, y0_ref, y1_ref, y2_ref, y3_ref, w_ref, o_ref):
    mod = mod_ref[...]
    g2 = mod[:, 5 * D_MODEL:6 * D_MODEL]
    w = w_ref[...]
    f = jnp.zeros((TM, D_MODEL), _F32)
    for k, y_ref in enumerate((y0_ref, y1_ref, y2_ref, y3_ref)):
        f = f + y_ref[...] * w[:, k:k + 1]
    o_ref[...] = h1_ref[...] + g2 * _rms(f, ng_ref[3:4, :])


def _combine_call(h1, mod3, norm_g, yslot, topw, modrow):
    row = lambda i, a: (i, 0)
    choice = lambda k: pl.BlockSpec((TM, D_MODEL), lambda i, a: (k * N_TILES + i, 0))
    grid_spec = pltpu.PrefetchScalarGridSpec(
        num_scalar_prefetch=1,
        grid=(N_TILES,),
        in_specs=[
            pl.BlockSpec((TM, D_MODEL), row),
            pl.BlockSpec((None, 1, 6 * D_MODEL), lambda i, a: (a[i], 0, 0)),
            pl.BlockSpec((4, D_MODEL), lambda i, a: (0, 0)),
            choice(0), choice(1), choice(2), choice(3),
            pl.BlockSpec((TM, LANES), row),
        ],
        out_specs=pl.BlockSpec((TM, D_MODEL), row),
    )
    return pl.pallas_call(
        _combine_kernel,
        grid_spec=grid_spec,
        out_shape=jax.ShapeDtypeStruct((T_ALL, D_MODEL), _F32),
        compiler_params=_params(1),
        name="moe_combine",
    )(modrow, h1, mod3, norm_g, yslot, yslot, yslot, yslot, topw)


def _tile_tables():
    t = np.arange(N_TILES)
    lat = t >= CTX_TILES
    j = (t - CTX_TILES) % LAT_TILES_PER_SEQ
    modrow = np.where(lat, 1 + (t - CTX_TILES) // LAT_TILES_PER_SEQ, 0)
    ropeblk = np.where(lat, j, LAT_TILES_PER_SEQ)
    stblk = np.where(lat, CTX_TILES, t)
    hasprev = np.where(lat & (j > 0), 1, 0)
    hasnext = np.where(lat & (j < LAT_TILES_PER_SEQ - 1), 1, 0)
    as_i32 = lambda a: jnp.asarray(a, jnp.int32)
    return as_i32(modrow), as_i32(ropeblk), as_i32(stblk), as_i32(hasprev), as_i32(hasnext)


def _rope_tables():
    half = HEAD_DIM // 4
    freqs = ROPE_THETA ** (-jnp.arange(half, dtype=_F32) / half)
    t = jnp.arange(DEC_SEQ)
    pos = jnp.stack([t // GRID_W, t % GRID_W], axis=1).astype(_F32)
    ang = pos[:, :, None] * freqs[None, None, :]
    cos = jnp.cos(ang)
    sin = jnp.sin(ang)
    cos_h = jnp.concatenate([cos, cos], axis=-1).reshape(DEC_SEQ, HEAD_DIM)
    sin_h = jnp.concatenate([-sin, sin], axis=-1).reshape(DEC_SEQ, HEAD_DIM)
    cos_t = jnp.tile(cos_h, (1, N_HEADS))
    sin_t = jnp.tile(sin_h, (1, N_HEADS))
    cos_t = jnp.concatenate([cos_t, jnp.ones((TM, BR_WIDTH), _F32)], axis=0)
    sin_t = jnp.concatenate([sin_t, jnp.zeros((TM, BR_WIDTH), _F32)], axis=0)
    return cos_t, sin_t


def _permute_heads_cols(w):
    lead = w.shape[:-1]
    return w.reshape(lead + (N_HEADS, HEAD_DIM))[..., jnp.array([0, 2, 1, 3]), :].reshape(lead + (BR_WIDTH,))


def _cache_rows(cache, l):
    return cache[:, l].reshape(DEC_BATCH, PAST_LEN, KV_WIDTH).astype(_BF16)


def _moe_layout(cnt, topi, rank):
    counts = cnt[0, :N_EXPERTS]
    padded = (counts + MOE_BLK - 1) // MOE_BLK * MOE_BLK
    pad_end = jnp.cumsum(padded)
    pad_start = pad_end - padded
    blk_row0 = jnp.arange(N_MOE_BLOCKS, dtype=jnp.int32) * MOE_BLK
    blk_e = jnp.minimum(jnp.sum(pad_end[None, :] <= blk_row0[:, None], axis=1),
                        N_EXPERTS - 1).astype(jnp.int32)
    nvalid = (pad_end[-1:] // MOE_BLK).astype(jnp.int32)
    spare_before = jnp.cumsum(padded - counts) - (padded - counts)
    row = jnp.arange(N_MOE_ROWS, dtype=jnp.int32).reshape(N_MOE_BLOCKS, MOE_BLK)
    in_tail = (blk_row0 >= pad_end[-1])[:, None]
    e = blk_e[:, None]
    spare_in_group = row - pad_start[e] - counts[e] + spare_before[e]
    spare_tail = row - pad_end[-1] + jnp.sum(padded - counts)
    spare = N_SLOTS + jnp.where(in_tail, spare_tail, spare_in_group)
    dest = pad_start[topi[:, :TOP_K]] + rank[:, :TOP_K]
    slot_id = (jnp.arange(TOP_K, dtype=jnp.int32)[None, :] * T_ALL
               + jnp.arange(T_ALL, dtype=jnp.int32)[:, None])
    slot_of_row = spare.reshape(-1).astype(jnp.int32).at[dest.reshape(-1)].set(slot_id.reshape(-1))
    tok_of_row = jnp.where(slot_of_row < N_SLOTS, slot_of_row % T_ALL, 0)
    return blk_e, nvalid, tok_of_row, slot_of_row


def kernel(x_prompt, x_sample, cache_win_k, cache_win_v, cache_full_k, cache_full_v, c, c_ctx,
           w_mod, b_mod, norm_g, w_in, a_ln_g, a_ln_b, w_sp, b_sp, conv_w, conv_b, b_ln_g, b_ln_b,
           sink, qn_g, kn_g, w_br, w_o, w_router, b_router, w1, b1, w2, b2):
    modrow, ropeblk, stblk, hasprev, hasnext = _tile_tables()
    cos_t, sin_t = _rope_tables()
    head_id = np.arange(BR_WIDTH) // HEAD_DIM
    ones_bd = jnp.asarray(head_id[:, None] == head_id[None, :], _BF16)
    ltri = jnp.asarray(np.tril(np.ones((TM, TM)), -1), _BF16)
    row2 = lambda v: v.reshape(1, -1)

    cvec = jnp.concatenate([c_ctx[None, :], c, jnp.zeros((8 - 1 - DEC_BATCH, D_MODEL), _F32)], axis=0)
    mod_all = _mod_call(cvec, w_mod, b_mod)

    h = jnp.concatenate([x_prompt.reshape(T_CTX, D_MODEL), x_sample.reshape(T_LAT, D_MODEL)], axis=0)
    states = []
    for l in range(DEPTH):
        mod3 = mod_all[l].reshape(8, 1, 6 * D_MODEL)
        wl = w_in[l]
        w_main = jnp.concatenate([
            wl[:, 0:1024],
            _permute_heads_cols(wl[:, 1024:1280]), wl[:, 1280:1536],
            _permute_heads_cols(wl[:, 1536:1792]), wl[:, 1792:2048]], axis=1).astype(_BF16)
        wg = wl[:, MAIN_WIDTH:].astype(_BF16)
        qn = jnp.tile(qn_g[l], N_HEADS).reshape(1, BR_WIDTH)
        kn = jnp.tile(kn_g[l], KV_HEADS).reshape(1, KV_WIDTH)

        ab, qc, kc, vc, qd, kd, vd, st = _inproj_call(
            h, mod3, row2(norm_g[l, 0]), w_main, qn, kn, ones_bd, cos_t, sin_t,
            (modrow, ropeblk, stblk))
        states.append(st[:T_CTX].reshape(BATCH, SEQ, 4, KV_HEADS, HEAD_DIM))

        bsp = jnp.repeat(b_sp[l].T, BR_WIDTH // A_GROUPS, axis=1)
        cw = jnp.concatenate([conv_w[l], jnp.zeros((1, BR_WIDTH), _F32)], axis=0)
        oa, ob = _mix_call(ab, row2(a_ln_g[l]), row2(a_ln_b[l]), w_sp[l].astype(_BF16), bsp, cw,
                           row2(conv_b[l]), row2(b_ln_g[l]), row2(b_ln_b[l]), (hasprev, hasnext))

        occ, odc = _ctx_attn_call(sink[l], qc, kc, vc, qd, kd, vd)
        ocl = _win_attn_call(sink[l], qc, kc, vc, _cache_rows(cache_win_k, l),
                             _cache_rows(cache_win_v, l))
        odl = _full_attn_call(qd, kd, vd, _cache_rows(cache_full_k, l),
                              _cache_rows(cache_full_v, l))

        perm = jnp.array([0, 2, 1, 3])
        wbr = w_br[l].reshape(N_BRANCH, N_HEADS, HEAD_DIM, D_MODEL)
        wbr = jnp.concatenate([wbr[0:2], wbr[2:4][:, perm]], axis=0)
        wbr = wbr.reshape(N_BRANCH, BR_WIDTH, D_MODEL).astype(_BF16)
        wr = jnp.concatenate([w_router[l], jnp.zeros((D_MODEL, LANES - N_EXPERTS), _F32)], axis=1)
        wr_hi, wr_lo = _hi_lo(wr)
        br = jnp.concatenate([b_router[l], jnp.full((LANES - N_EXPERTS,), NEG_BIG, _F32)]).reshape(1, LANES)
        h1, n2, topi, topw, rank, cnt = _merge_call(
            h, mod3, norm_g[l], oa, ob, occ, ocl, odc, odl, wg, wbr, w_o[l].astype(_BF16),
            wr_hi, wr_lo, br, ltri, modrow)

        blk_e, nvalid, tok_of_row, slot_of_row = _moe_layout(cnt, topi, rank)
        yslot = _expert_call(l, blk_e, nvalid, tok_of_row, slot_of_row, n2, w1, b1, w2, b2)
        h = _combine_call(h1, mod3, norm_g[l], yslot, topw, modrow)

    y_prompt = h[:T_CTX].reshape(BATCH, SEQ, D_MODEL)
    y_sample = h[T_CTX:].reshape(DEC_BATCH, DEC_SEQ, D_MODEL)
    st = jnp.stack(states, axis=1)
    return (y_prompt, y_sample, st[:, :, :, 0], st[:, :, :, 1], st[:, :, :, 2], st[:, :, :, 3])
```

```python
import functools

import numpy as np
import jax
import jax.numpy as jnp
from jax import lax
from jax.experimental import pallas as pl
from jax.experimental.pallas import tpu as pltpu

D_MODEL = 1024
BATCH = 16
SEQ = 256
DEPTH = 2
DEC_BATCH = 4
DEC_SEQ = 4096
PAST_LEN = 256
GRID_W = 64
HEAD_DIM = 64
BR_WIDTH = 256
N_BRANCH = 4
A_GROUPS = 4
CHUNK = 128
CONV_W = 31
N_HEADS = 4
KV_HEADS = 2
KV_WIDTH = KV_HEADS * HEAD_DIM
WINDOW = 128
ROPE_THETA = 10000.0
EPS = 1e-6
N_EXPERTS = 32
TOP_K = 4
D_FF = D_MODEL
SWIGLU_LIMIT = 7.0
SWIGLU_ALPHA = 1.702
MAIN_WIDTH = 2048

LANES = 128
T_CTX = BATCH * SEQ
T_LAT = DEC_BATCH * DEC_SEQ
T_ALL = T_CTX + T_LAT
TM = 256
N_TILES = T_ALL // TM
CTX_TILES = T_CTX // TM
LAT_TILES_PER_SEQ = DEC_SEQ // TM
HALO = 16
MOE_BLK = 256
N_SLOTS = T_ALL * TOP_K
N_MOE_BLOCKS = N_SLOTS // MOE_BLK + N_EXPERTS
N_MOE_ROWS = N_MOE_BLOCKS * MOE_BLK
WIN_TQ = 512
FULL_TQ = 256
FULL_TK = 512
NEG_BIG = -1e30
VMEM_LIMIT = 56 * 1024 * 1024

_F32 = jnp.float32
_BF16 = jnp.bfloat16


def _params(n_axes, vmem=None):
    return pltpu.CompilerParams(
        dimension_semantics=("arbitrary",) * n_axes,
        vmem_limit_bytes=vmem if vmem is not None else VMEM_LIMIT)


def _split_dot(a, b_hi, b_lo):
    a_hi = a.astype(_BF16)
    a_lo = (a - a_hi.astype(_F32)).astype(_BF16)
    return (jnp.dot(a_hi, b_hi, preferred_element_type=_F32)
            + jnp.dot(a_lo, b_hi, preferred_element_type=_F32)
            + jnp.dot(a_hi, b_lo, preferred_element_type=_F32))


def _hi_lo(w):
    hi = w.astype(_BF16)
    lo = (w - hi.astype(_F32)).astype(_BF16)
    return hi, lo


def _mod_kernel(c_ref, w_ref, b_ref, o_ref):
    c = c_ref[...]
    s = c * jax.nn.sigmoid(c)
    w = w_ref[...]
    w_hi = w.astype(_BF16)
    w_lo = (w - w_hi.astype(_F32)).astype(_BF16)
    o_ref[...] = _split_dot(s, w_hi, w_lo) + b_ref[...]


def _mod_call(cvec, w_mod, b_mod):
    tn = 1024
    return pl.pallas_call(
        _mod_kernel,
        grid=(DEPTH, 6 * D_MODEL // tn),
        in_specs=[
            pl.BlockSpec((8, D_MODEL), lambda l, j: (0, 0)),
            pl.BlockSpec((None, D_MODEL, tn), lambda l, j: (l, 0, j)),
            pl.BlockSpec((None, 1, tn), lambda l, j: (l, 0, j)),
        ],
        out_specs=pl.BlockSpec((None, 8, tn), lambda l, j: (l, 0, j)),
        out_shape=jax.ShapeDtypeStruct((DEPTH, 8, 6 * D_MODEL), _F32),
        compiler_params=_params(2),
        name="mod",
    )(cvec, w_mod, b_mod.reshape(DEPTH, 1, 6 * D_MODEL))


def _rms(x, g):
    ms = jnp.mean(x * x, axis=-1, keepdims=True)
    return x * lax.rsqrt(ms + EPS) * g


def _head_rms(x, ones_bd, g):
    xx = x * x
    hi = xx.astype(_BF16)
    lo = (xx - hi.astype(_F32)).astype(_BF16)
    ss = (jnp.dot(hi, ones_bd, preferred_element_type=_F32)
          + jnp.dot(lo, ones_bd, preferred_element_type=_F32))
    return x * lax.rsqrt(ss * (1.0 / HEAD_DIM) + EPS) * g


def _rope(x, cos, sin_signed):
    w = x.shape[-1]
    lane = lax.broadcasted_iota(jnp.int32, x.shape, 1)
    first = (lane % 32) < 16
    partner = jnp.where(first, pltpu.roll(x, w - 16, 1), pltpu.roll(x, 16, 1))
    return x * cos + partner * sin_signed


def _inproj_kernel(modrow_ref, ropeblk_ref, stblk_ref,
                   x_ref, mod_ref, g0_ref, w_ref, qn_ref, kn_ref, ones_ref,
                   cos_ref, sin_ref,
                   ab_ref, qc_ref, kc_ref, vc_ref, qd_ref, kd_ref, vd_ref, st_ref):
    x = x_ref[...]
    mod = mod_ref[...]
    sh1 = mod[:, 0:D_MODEL]
    sc1 = mod[:, D_MODEL:2 * D_MODEL]
    n = _rms(x, g0_ref[...]) * (1.0 + sc1) + sh1
    z = jnp.dot(n.astype(_BF16), w_ref[...], preferred_element_type=_F32)
    ab_ref[...] = z[:, 0:1024]
    cq = z[:, 1024:1280]
    ck = z[:, 1280:1408]
    cv = z[:, 1408:1536]
    dq = z[:, 1536:1792]
    dk = z[:, 1792:1920]
    dv = z[:, 1920:2048]
    ones = ones_ref[...]
    dq = _head_rms(dq, ones, qn_ref[...])
    dk = _head_rms(dk, ones[0:KV_WIDTH, 0:KV_WIDTH], kn_ref[...])
    cos = cos_ref[...]
    sin = sin_ref[...]
    scale = HEAD_DIM ** -0.5
    qc_ref[...] = (_rope(cq, cos, sin) * scale).astype(_BF16)
    qd_ref[...] = (_rope(dq, cos, sin) * scale).astype(_BF16)
    kc = _rope(ck, cos[:, 0:KV_WIDTH], sin[:, 0:KV_WIDTH])
    kd = _rope(dk, cos[:, 0:KV_WIDTH], sin[:, 0:KV_WIDTH])
    kc_ref[...] = kc.astype(_BF16)
    kd_ref[...] = kd.astype(_BF16)
    vc_ref[...] = cv.astype(_BF16)
    vd_ref[...] = dv.astype(_BF16)
    st_ref[:, 0:128] = kc
    st_ref[:, 128:256] = cv
    st_ref[:, 256:384] = kd
    st_ref[:, 384:512] = dv


def _inproj_call(x, mod3, g0, w_main, qn, kn, ones_bd, cos_t, sin_t, tables):
    modrow, ropeblk, stblk = tables
    row = lambda i, a, b, c: (i, 0)
    full = lambda i, a, b, c: (0, 0)
    bf = lambda w: jax.ShapeDtypeStruct((T_ALL, w), _BF16)
    grid_spec = pltpu.PrefetchScalarGridSpec(
        num_scalar_prefetch=3,
        grid=(N_TILES,),
        in_specs=[
            pl.BlockSpec((TM, D_MODEL), row),
            pl.BlockSpec((None, 1, 6 * D_MODEL), lambda i, a, b, c: (a[i], 0, 0)),
            pl.BlockSpec((1, D_MODEL), full),
            pl.BlockSpec((D_MODEL, MAIN_WIDTH), full),
            pl.BlockSpec((1, BR_WIDTH), full),
            pl.BlockSpec((1, KV_WIDTH), full),
            pl.BlockSpec((BR_WIDTH, BR_WIDTH), full),
            pl.BlockSpec((TM, BR_WIDTH), lambda i, a, b, c: (b[i], 0)),
            pl.BlockSpec((TM, BR_WIDTH), lambda i, a, b, c: (b[i], 0)),
        ],
        out_specs=[
            pl.BlockSpec((TM, 1024), row),
            pl.BlockSpec((TM, BR_WIDTH), row),
            pl.BlockSpec((TM, KV_WIDTH), row),
            pl.BlockSpec((TM, KV_WIDTH), row),
            pl.BlockSpec((TM, BR_WIDTH), row),
            pl.BlockSpec((TM, KV_WIDTH), row),
            pl.BlockSpec((TM, KV_WIDTH), row),
            pl.BlockSpec((TM, 512), lambda i, a, b, c: (c[i], 0)),
        ],
    )
    return pl.pallas_call(
        _inproj_kernel,
        grid_spec=grid_spec,
        out_shape=[
            jax.ShapeDtypeStruct((T_ALL, 1024), _F32),
            bf(BR_WIDTH), bf(KV_WIDTH), bf(KV_WIDTH),
            bf(BR_WIDTH), bf(KV_WIDTH), bf(KV_WIDTH),
            jax.ShapeDtypeStruct((T_CTX + TM, 512), _F32),
        ],
        compiler_params=_params(1),
        name="inproj",
    )(modrow, ropeblk, stblk, x, mod3, g0, w_main, qn, kn, ones_bd, cos_t, sin_t)


def _layernorm(x, g, b):
    mu = jnp.mean(x, axis=-1, keepdims=True)
    xc = x - mu
    var = jnp.mean(xc * xc, axis=-1, keepdims=True)
    return xc * lax.rsqrt(var + EPS) * g + b


def _gelu(x):
    return 0.5 * x * (1.0 + lax.erf(x * (2.0 ** -0.5)))


def _mix_kernel(hasprev_ref, hasnext_ref,
                ab_ref, prev_ref, next_ref, alg_ref, alb_ref, wsp_ref, bsp_ref,
                cw_ref, cb_ref, blg_ref, blb_ref,
                oa_ref, ob_ref, buf_ref):
    i = pl.program_id(0)
    ab = ab_ref[...]
    u = _gelu(ab[:, 0:256])
    va = _layernorm(_gelu(ab[:, 256:512]), alg_ref[...], alb_ref[...])
    vab = va.astype(_BF16)
    lane = lax.broadcasted_iota(jnp.int32, (CHUNK, BR_WIDTH), 1)
    group = lane // (BR_WIDTH // A_GROUPS)
    bsp = bsp_ref[...]
    for c in range(TM // CHUNK):
        v_c = vab[c * CHUNK:(c + 1) * CHUNK, :]
        sp = bsp
        for g in range(A_GROUPS):
            full = jnp.dot(wsp_ref[g], v_c, preferred_element_type=_F32)
            sp = sp + jnp.where(group == g, full, 0.0)
        oa_ref[c * CHUNK:(c + 1) * CHUNK, :] = (u[c * CHUNK:(c + 1) * CHUNK, :] * sp).astype(_BF16)
    glu = ab[:, 512:768] * jax.nn.sigmoid(ab[:, 768:1024])
    pv = prev_ref[...]
    nx = next_ref[...]
    hp = hasprev_ref[i].astype(_F32)
    hn = hasnext_ref[i].astype(_F32)
    buf_ref[0:HALO, :] = pv[:, 0:256] * jax.nn.sigmoid(pv[:, 256:512]) * hp
    buf_ref[HALO:HALO + TM, :] = glu
    buf_ref[HALO + TM:HALO + TM + HALO, :] = nx[:, 0:256] * jax.nn.sigmoid(nx[:, 256:512]) * hn
    cw = cw_ref[...]
    acc = jnp.zeros((TM, BR_WIDTH), _F32) + cb_ref[...]
    off = HALO - CONV_W // 2
    for k in range(CONV_W):
        acc = acc + buf_ref[off + k:off + k + TM, :] * cw[k:k + 1, :]
    y = _layernorm(acc, blg_ref[...], blb_ref[...])
    ob_ref[...] = (y * jax.nn.sigmoid(y)).astype(_BF16)


def _mix_call(ab, a_ln_g, a_ln_b, wsp, bsp, conv_w, conv_b, b_ln_g, b_ln_b, tables):
    hasprev, hasnext = tables
    n_halo_blocks = T_ALL // HALO
    per = TM // HALO
    full2 = lambda i, a, b: (0, 0)
    grid_spec = pltpu.PrefetchScalarGridSpec(
        num_scalar_prefetch=2,
        grid=(N_TILES,),
        in_specs=[
            pl.BlockSpec((TM, 1024), lambda i, a, b: (i, 0)),
            pl.BlockSpec((HALO, 512), lambda i, a, b: (jnp.maximum(i * per - 1, 0), 1)),
            pl.BlockSpec((HALO, 512), lambda i, a, b: (jnp.minimum((i + 1) * per, n_halo_blocks - 1), 1)),
            pl.BlockSpec((1, BR_WIDTH), full2),
            pl.BlockSpec((1, BR_WIDTH), full2),
            pl.BlockSpec((A_GROUPS, CHUNK, CHUNK), lambda i, a, b: (0, 0, 0)),
            pl.BlockSpec((CHUNK, BR_WIDTH), full2),
            pl.BlockSpec((32, BR_WIDTH), full2),
            pl.BlockSpec((1, BR_WIDTH), full2),
            pl.BlockSpec((1, BR_WIDTH), full2),
            pl.BlockSpec((1, BR_WIDTH), full2),
        ],
        out_specs=[
            pl.BlockSpec((TM, BR_WIDTH), lambda i, a, b: (i, 0)),
            pl.BlockSpec((TM, BR_WIDTH), lambda i, a, b: (i, 0)),
        ],
        scratch_shapes=[pltpu.VMEM((TM + 2 * HALO, BR_WIDTH), _F32)],
    )
    return pl.pallas_call(
        _mix_kernel,
        grid_spec=grid_spec,
        out_shape=[jax.ShapeDtypeStruct((T_ALL, BR_WIDTH), _BF16)] * 2,
        compiler_params=_params(1),
        name="mixers_ab",
    )(hasprev, hasnext, ab, ab, ab, a_ln_g, a_ln_b, wsp, bsp, conv_w, conv_b, b_ln_g, b_ln_b)


def _lane_masks(n_rows):
    lane = lax.broadcasted_iota(jnp.int32, (n_rows, LANES), 1)
    return lane < HEAD_DIM, lane >= HEAD_DIM


def _group_queries(q, kv):
    tq = q.shape[0]
    lo, hi = _lane_masks(tq)
    keep = lo if kv == 0 else hi
    zero = jnp.zeros((), q.dtype)
    return jnp.concatenate([jnp.where(keep, q[:, 0:LANES], zero),
                            jnp.where(keep, q[:, LANES:2 * LANES], zero)], axis=0)


def _scores(qs, k):
    return lax.dot_general(qs, k, (((1,), (1,)), ((), ())), preferred_element_type=_F32)


def _attend_once(q, chunks, sinks):
    tq = q.shape[0]
    lo, hi = _lane_masks(tq)
    outs = [jnp.zeros((tq, LANES), _F32), jnp.zeros((tq, LANES), _F32)]
    for kv in range(KV_HEADS):
        qs = _group_queries(q, kv)
        ss = []
        for k, _, mask in chunks:
            s = _scores(qs, k)
            if mask is not None:
                s = jnp.where(mask, s, -jnp.inf)
            ss.append(s)
        m = ss[0].max(axis=-1, keepdims=True)
        for s in ss[1:]:
            m = jnp.maximum(m, s.max(axis=-1, keepdims=True))
        if sinks is not None:
            row = lax.broadcasted_iota(jnp.int32, (2 * tq, 1), 0)
            sink_col = jnp.where(row < tq, sinks[2 * kv], sinks[2 * kv + 1])
            m = jnp.maximum(m, sink_col)
            l = jnp.exp(sink_col - m)
        else:
            l = jnp.zeros((2 * tq, 1), _F32)
        acc = jnp.zeros((2 * tq, LANES), _F32)
        for s, (_, v, _) in zip(ss, chunks):
            p = jnp.exp(s - m)
            l = l + p.sum(axis=-1, keepdims=True)
            acc = acc + jnp.dot(p.astype(_BF16), v, preferred_element_type=_F32)
        o = acc / l
        keep = lo if kv == 0 else hi
        outs[0] = outs[0] + jnp.where(keep, o[0:tq], 0.0)
        outs[1] = outs[1] + jnp.where(keep, o[tq:2 * tq], 0.0)
    return jnp.concatenate(outs, axis=1)


def _ctx_attn_kernel(sink_ref, qc_ref, kc_ref, vc_ref, qd_ref, kd_ref, vd_ref, oc_ref, od_ref):
    sinks = [sink_ref[h] for h in range(N_HEADS)]
    oc_ref[...] = _attend_once(qc_ref[...], [(kc_ref[...], vc_ref[...], None)], sinks).astype(_BF16)
    od_ref[...] = _attend_once(qd_ref[...], [(kd_ref[...], vd_ref[...], None)], None).astype(_BF16)


def _ctx_attn_call(sink, qc, kc, vc, qd, kd, vd):
    seq = lambda b: (b, 0)
    q_spec = pl.BlockSpec((SEQ, BR_WIDTH), seq)
    kv_spec = pl.BlockSpec((SEQ, KV_WIDTH), seq)
    return pl.pallas_call(
        _ctx_attn_kernel,
        grid=(BATCH,),
        in_specs=[pl.BlockSpec(memory_space=pltpu.SMEM),
                  q_spec, kv_spec, kv_spec, q_spec, kv_spec, kv_spec],
        out_specs=[q_spec, q_spec],
        out_shape=[jax.ShapeDtypeStruct((T_CTX, BR_WIDTH), _BF16)] * 2,
        compiler_params=_params(1),
        name="ctx_attention",
    )(sink, qc, kc, vc, qd, kd, vd)


def _win_attn_kernel(sink_ref, q_ref, kcur_ref, kprev_ref, knext_ref, vcur_ref, vprev_ref,
                     vnext_ref, kctx_ref, vctx_ref, oc_ref):
    j = pl.program_id(1)
    nq = pl.num_programs(1)
    sinks = [sink_ref[h] for h in range(N_HEADS)]
    tq = WIN_TQ
    def rel(n_keys):
        qi = lax.broadcasted_iota(jnp.int32, (2 * tq, n_keys), 0) % tq
        kj = lax.broadcasted_iota(jnp.int32, (2 * tq, n_keys), 1)
        return qi, kj
    qi, kj = rel(tq)
    m_cur = jnp.abs(qi - kj) <= WINDOW
    qi, kj = rel(WINDOW)
    m_prev = ((qi - (kj - WINDOW)) <= WINDOW) & (j > 0)
    m_next = (((kj + tq) - qi) <= WINDOW) & (j < nq - 1)
    chunks = [
        (kcur_ref[...], vcur_ref[...], m_cur),
        (kprev_ref[...], vprev_ref[...], m_prev),
        (knext_ref[...], vnext_ref[...], m_next),
        (kctx_ref[...], vctx_ref[...], None),
    ]
    oc_ref[...] = _attend_once(q_ref[...], chunks, sinks).astype(_BF16)


def _win_attn_call(sink, qc, kc, vc, kctx, vctx):
    nq = DEC_SEQ // WIN_TQ
    ctx_q = T_CTX // WIN_TQ
    sub = WIN_TQ // WINDOW
    ctx_w = T_CTX // WINDOW
    per_seq_w = DEC_SEQ // WINDOW
    cur = lambda b, j: (ctx_q + b * nq + j, 0)
    prev = lambda b, j: (ctx_w + b * per_seq_w + jnp.maximum(j * sub - 1, 0), 0)
    nxt = lambda b, j: (ctx_w + b * per_seq_w + jnp.minimum((j + 1) * sub, per_seq_w - 1), 0)
    cache = lambda b, j: (b, 0, 0)
    return pl.pallas_call(
        _win_attn_kernel,
        grid=(DEC_BATCH, nq),
        in_specs=[
            pl.BlockSpec(memory_space=pltpu.SMEM),
            pl.BlockSpec((WIN_TQ, BR_WIDTH), cur),
            pl.BlockSpec((WIN_TQ, KV_WIDTH), cur),
            pl.BlockSpec((WINDOW, KV_WIDTH), prev),
            pl.BlockSpec((WINDOW, KV_WIDTH), nxt),
            pl.BlockSpec((WIN_TQ, KV_WIDTH), cur),
            pl.BlockSpec((WINDOW, KV_WIDTH), prev),
            pl.BlockSpec((WINDOW, KV_WIDTH), nxt),
            pl.BlockSpec((None, PAST_LEN, KV_WIDTH), cache),
            pl.BlockSpec((None, PAST_LEN, KV_WIDTH), cache),
        ],
        out_specs=pl.BlockSpec((WIN_TQ, BR_WIDTH), lambda b, j: (b * nq + j, 0)),
        out_shape=jax.ShapeDtypeStruct((T_LAT, BR_WIDTH), _BF16),
        compiler_params=_params(2),
        name="window_attention",
    )(sink, qc, kc, kc, kc, vc, vc, vc, kctx, vctx)


def _full_attn_kernel(q_ref, k_ref, v_ref, kctx_ref, vctx_ref, od_ref):
    tq = FULL_TQ
    q = q_ref[...]
    lo, hi = _lane_masks(tq)
    qss = [_group_queries(q, kv) for kv in range(KV_HEADS)]

    def update(carries, k, v):
        new = []
        for qs, (m, l, acc) in zip(qss, carries):
            s = _scores(qs, k)
            m_new = jnp.maximum(m, s.max(axis=-1, keepdims=True))
            a = jnp.exp(m - m_new)
            p = jnp.exp(s - m_new)
            l = a * l + p.sum(axis=-1, keepdims=True)
            acc = a * acc + jnp.dot(p.astype(_BF16), v, preferred_element_type=_F32)
            new.append((m_new, l, acc))
        return tuple(new)

    def body(c, carries):
        start = pl.multiple_of(c * FULL_TK, FULL_TK)
        return update(carries, k_ref[pl.ds(start, FULL_TK), :], v_ref[pl.ds(start, FULL_TK), :])

    init = (jnp.full((2 * tq, 1), -jnp.inf, _F32), jnp.zeros((2 * tq, 1), _F32),
            jnp.zeros((2 * tq, LANES), _F32))
    carries = lax.fori_loop(0, DEC_SEQ // FULL_TK, body, (init, init))
    carries = update(carries, kctx_ref[...], vctx_ref[...])
    outs = [jnp.zeros((tq, LANES), _F32), jnp.zeros((tq, LANES), _F32)]
    for kv, (m, l, acc) in enumerate(carries):
        o = acc / l
        keep = lo if kv == 0 else hi
        outs[0] = outs[0] + jnp.where(keep, o[0:tq], 0.0)
        outs[1] = outs[1] + jnp.where(keep, o[tq:2 * tq], 0.0)
    od_ref[...] = jnp.concatenate(outs, axis=1).astype(_BF16)


def _full_attn_call(qd, kd, vd, kctx, vctx):
    nq = DEC_SEQ // FULL_TQ
    ctx_q = T_CTX // FULL_TQ
    ctx_seq = T_CTX // DEC_SEQ
    cur = lambda b, j: (ctx_q + b * nq + j, 0)
    seq = lambda b, j: (ctx_seq + b, 0)
    cache = lambda b, j: (b, 0, 0)
    return pl.pallas_call(
        _full_attn_kernel,
        grid=(DEC_BATCH, nq),
        in_specs=[
            pl.BlockSpec((FULL_TQ, BR_WIDTH), cur),
            pl.BlockSpec((DEC_SEQ, KV_WIDTH), seq),
            pl.BlockSpec((DEC_SEQ, KV_WIDTH), seq),
            pl.BlockSpec((None, PAST_LEN, KV_WIDTH), cache),
            pl.BlockSpec((None, PAST_LEN, KV_WIDTH), cache),
        ],
        out_specs=pl.BlockSpec((FULL_TQ, BR_WIDTH), lambda b, j: (b * nq + j, 0)),
        out_shape=jax.ShapeDtypeStruct((T_LAT, BR_WIDTH), _BF16),
        compiler_params=_params(2),
        name="full_attention",
    )(qd, kd, vd, kctx, vctx)


SUBL = D_MODEL // LANES


def _store_row_tiles(ref, n, x):
    for c in range(SUBL):
        ref[pl.ds(c, n, stride=SUBL), :] = x[:, c * LANES:(c + 1) * LANES]


def _load_row_tiles(ref, n):
    return jnp.concatenate([ref[pl.ds(c, n, stride=SUBL), :] for c in range(SUBL)], axis=1)


def _merge_kernel(modrow_ref,
                  x_ref, mod_ref, ng_ref, oa_ref, ob_ref, occ_ref, ocl_ref, odc_ref, odl_ref,
                  wg_ref, wbr_ref, wo_ref, wrh_ref, wrl_ref, br_ref, ltri_ref,
                  h1_ref, n2_ref, topi_ref, topw_ref, rank_ref, cnt_ref, base_ref):
    i = pl.program_id(0)

    @pl.when(i == 0)
    def _():
        base_ref[...] = jnp.zeros_like(base_ref)

    x = x_ref[...]
    mod = mod_ref[...]
    sh1 = mod[:, 0:D_MODEL]
    sc1 = mod[:, D_MODEL:2 * D_MODEL]
    g1 = mod[:, 2 * D_MODEL:3 * D_MODEL]
    sh2 = mod[:, 3 * D_MODEL:4 * D_MODEL]
    sc2 = mod[:, 4 * D_MODEL:5 * D_MODEL]
    ng = ng_ref[...]
    nb = (_rms(x, ng[0:1, :]) * (1.0 + sc1) + sh1).astype(_BF16)
    is_ctx = i < CTX_TILES
    oc = jnp.where(is_ctx, occ_ref[...], ocl_ref[...])
    od = jnp.where(is_ctx, odc_ref[...], odl_ref[...])
    branches = (oa_ref[...], ob_ref[...], oc, od)
    mixed = jnp.zeros((TM, D_MODEL), _F32)
    for k in range(N_BRANCH):
        gate = jax.nn.sigmoid(jnp.dot(nb, wg_ref[:, k * D_MODEL:(k + 1) * D_MODEL],
                                      preferred_element_type=_F32))
        proj = jnp.dot(branches[k], wbr_ref[k], preferred_element_type=_F32)
        mixed = mixed + gate * proj
    mix = jnp.dot(mixed.astype(_BF16), wo_ref[...], preferred_element_type=_F32)
    h1 = x + g1 * _rms(mix, ng[1:2, :])
    h1_ref[...] = h1
    n2 = _rms(h1, ng[2:3, :]) * (1.0 + sc2) + sh2
    _store_row_tiles(n2_ref, TM, n2)
    logits = _split_dot(n2, wrh_ref[...], wrl_ref[...]) + br_ref[...]
    lane = lax.broadcasted_iota(jnp.int32, (TM, LANES), 1).astype(_F32)
    work = logits
    vals, idxs = [], []
    for _ in range(TOP_K):
        m = work.max(axis=-1, keepdims=True)
        idx = jnp.where(work == m, lane, float(LANES)).min(axis=-1, keepdims=True)
        vals.append(m)
        idxs.append(idx)
        work = jnp.where(lane == idx, -jnp.inf, work)
    es = [jnp.exp(v - vals[0]) for v in vals]
    denom = es[0] + es[1] + es[2] + es[3]
    ltri = ltri_ref[...]
    running = base_ref[0:1, :]
    topi = jnp.zeros((TM, LANES), _F32)
    topw = jnp.zeros((TM, LANES), _F32)
    rank = jnp.zeros((TM, LANES), _F32)
    for k in range(TOP_K):
        onehot = (lane == idxs[k]).astype(_F32)
        before = jnp.dot(ltri, onehot.astype(_BF16), preferred_element_type=_F32)
        r = (onehot * (before + running)).sum(axis=-1, keepdims=True)
        running = running + onehot.sum(axis=0, keepdims=True)
        topi = jnp.where(lane == k, idxs[k], topi)
        topw = jnp.where(lane == k, es[k] / denom, topw)
        rank = jnp.where(lane == k, r, rank)
    base_ref[0:1, :] = running
    topi_ref[...] = topi.astype(jnp.int32)
    topw_ref[...] = topw
    rank_ref[...] = rank.astype(jnp.int32)
    cnt_ref[...] = jnp.broadcast_to(running, (8, LANES)).astype(jnp.int32)


def _merge_call(x, mod3, norm_g, oa, ob, occ, ocl, odc, odl, wg, wbr, wo, wr_hi, wr_lo, br, ltri, modrow):
    row = lambda i, a: (i, 0)
    full = lambda i, a: (0, 0)
    act = pl.BlockSpec((TM, BR_WIDTH), row)
    act_ctx = pl.BlockSpec((TM, BR_WIDTH), lambda i, a: (jnp.minimum(i, CTX_TILES - 1), 0))
    act_lat = pl.BlockSpec((TM, BR_WIDTH), lambda i, a: (jnp.maximum(i - CTX_TILES, 0), 0))
    wide = pl.BlockSpec((TM, LANES), row)
    grid_spec = pltpu.PrefetchScalarGridSpec(
        num_scalar_prefetch=1,
        grid=(N_TILES,),
        in_specs=[
            pl.BlockSpec((TM, D_MODEL), row),
            pl.BlockSpec((None, 1, 6 * D_MODEL), lambda i, a: (a[i], 0, 0)),
            pl.BlockSpec((4, D_MODEL), full),
            act, act, act_ctx, act_lat, act_ctx, act_lat,
            pl.BlockSpec((D_MODEL, N_BRANCH * D_MODEL), full),
            pl.BlockSpec((N_BRANCH, BR_WIDTH, D_MODEL), lambda i, a: (0, 0, 0)),
            pl.BlockSpec((D_MODEL, D_MODEL), full),
            pl.BlockSpec((D_MODEL, LANES), full),
            pl.BlockSpec((D_MODEL, LANES), full),
            pl.BlockSpec((1, LANES), full),
            pl.BlockSpec((TM, TM), full),
        ],
        out_specs=[
            pl.BlockSpec((TM, D_MODEL), row),
            pl.BlockSpec((TM * SUBL, LANES), row),
            wide, wide, wide,
            pl.BlockSpec((8, LANES), full),
        ],
        scratch_shapes=[pltpu.VMEM((8, LANES), _F32)],
    )
    return pl.pallas_call(
        _merge_kernel,
        grid_spec=grid_spec,
        out_shape=[
            jax.ShapeDtypeStruct((T_ALL, D_MODEL), _F32),
            jax.ShapeDtypeStruct((T_ALL * SUBL, LANES), _F32),
            jax.ShapeDtypeStruct((T_ALL, LANES), jnp.int32),
            jax.ShapeDtypeStruct((T_ALL, LANES), _F32),
            jax.ShapeDtypeStruct((T_ALL, LANES), jnp.int32),
            jax.ShapeDtypeStruct((8, LANES), jnp.int32),
        ],
        compiler_params=_params(1),
        name="merge_router",
    )(modrow, x, mod3, norm_g, oa, ob, occ, ocl, odc, odl, wg, wbr, wo, wr_hi, wr_lo, br, ltri)


def _expert_kernel(blk_e_ref, nvalid_ref,
                   tokc_ref, tokn_ref, slotp_ref, slotc_ref, n2_hbm, w1_ref, b1_ref, w2_ref, b2_ref,
                   y_hbm,
                   w1b_ref, w2b_ref, x0_ref, x1_ref, y0_ref, y1_ref, gsem, ssem):
    i = pl.program_id(0)
    nb = pl.num_programs(0)
    nvalid = nvalid_ref[0]
    xbufs = (x0_ref, x1_ref)
    ybufs = (y0_ref, y1_ref)

    def gather_copy(tok, r, p):
        return pltpu.make_async_copy(n2_hbm.at[tok], xbufs[p].at[pl.ds(SUBL * r, SUBL), :],
                                     gsem.at[p])

    def scatter_copy(slot, r, p):
        return pltpu.make_async_copy(ybufs[p].at[pl.ds(SUBL * r, SUBL), :], y_hbm.at[slot],
                                     ssem.at[p])

    def wait_gather(p):
        for r in range(MOE_BLK):
            gather_copy(0, r, p).wait()

    def wait_scatter(p):
        for r in range(MOE_BLK):
            scatter_copy(0, r, p).wait()

    @pl.when(i == 0)
    def _():
        y1_ref[...] = jnp.zeros_like(y1_ref)
        for r in range(MOE_BLK):
            gather_copy(tokc_ref[0, r], r, 0).start()

    prev = blk_e_ref[jnp.maximum(i - 1, 0)]
    fresh = (i == 0) | (blk_e_ref[i] != prev)

    @pl.when(fresh)
    def _():
        w1b_ref[...] = w1_ref[...].astype(_BF16)
        w2b_ref[...] = w2_ref[...].astype(_BF16)

    def valid_step(p):
        q = 1 - p
        wait_gather(p)

        @pl.when(i >= 1)
        def _():
            wait_scatter(p)

        for r in range(MOE_BLK):
            gather_copy(tokn_ref[0, r], r, q).start()
            scatter_copy(slotp_ref[0, r], r, q).start()
        x = _load_row_tiles(xbufs[p], MOE_BLK).astype(_BF16)
        hh = jnp.dot(x, w1b_ref[...], preferred_element_type=_F32) + b1_ref[...]
        x_glu = jnp.minimum(hh[:, 0:D_FF], SWIGLU_LIMIT)
        x_lin = jnp.clip(hh[:, D_FF:2 * D_FF], -SWIGLU_LIMIT, SWIGLU_LIMIT)
        act = x_glu * jax.nn.sigmoid(SWIGLU_ALPHA * x_glu) * (x_lin + 1.0)
        _store_row_tiles(ybufs[p], MOE_BLK,
                         jnp.dot(act.astype(_BF16), w2b_ref[...], preferred_element_type=_F32)
                         + b2_ref[...])

    def tail_step(p):
        q = 1 - p

        @pl.when(i == nvalid)
        def _():
            wait_gather(p)

        wait_scatter(p)
        ybufs[p][...] = jnp.zeros((MOE_BLK * SUBL, LANES), _F32)
        for r in range(MOE_BLK):
            scatter_copy(slotp_ref[0, r], r, q).start()

        @pl.when(i == nb - 1)
        def _():
            wait_scatter(q)
            for r in range(MOE_BLK):
                scatter_copy(slotc_ref[0, r], r, p).start()
            wait_scatter(p)

    is_valid = i < nvalid
    for p in range(2):
        pl.when(is_valid & (i % 2 == p))(functools.partial(valid_step, p))
        pl.when((~is_valid) & (i % 2 == p))(functools.partial(tail_step, p))


def _expert_call(l, blk_e, nvalid, tok_of_row, slot_of_row, n2, w1, b1, w2, b2):
    wsel = lambda i, e, n: (l, e[i], 0, 0)
    cur = lambda i, e, n: (i, 0, 0)
    nxt = lambda i, e, n: (jnp.minimum(i + 1, N_MOE_BLOCKS - 1), 0, 0)
    smem_blk = lambda imap: pl.BlockSpec((None, 1, MOE_BLK), imap, memory_space=pltpu.SMEM)
    tok3 = tok_of_row.reshape(N_MOE_BLOCKS, 1, MOE_BLK)
    virtual = N_MOE_ROWS + jnp.arange(MOE_BLK, dtype=jnp.int32)
    slot3 = jnp.concatenate([virtual, slot_of_row]).reshape(N_MOE_BLOCKS + 1, 1, MOE_BLK)
    grid_spec = pltpu.PrefetchScalarGridSpec(
        num_scalar_prefetch=2,
        grid=(N_MOE_BLOCKS,),
        in_specs=[
            smem_blk(cur), smem_blk(nxt),
            smem_blk(cur),
            smem_blk(lambda i, e, n: (i + 1, 0, 0)),
            pl.BlockSpec(memory_space=pl.ANY),
            pl.BlockSpec((None, None, D_MODEL, 2 * D_FF), wsel),
            pl.BlockSpec((None, None, 1, 2 * D_FF), wsel),
            pl.BlockSpec((None, None, D_FF, D_MODEL), wsel),
            pl.BlockSpec((None, None, 1, D_MODEL), wsel),
        ],
        out_specs=pl.BlockSpec(memory_space=pl.ANY),
        scratch_shapes=[pltpu.VMEM((D_MODEL, 2 * D_FF), _BF16),
                        pltpu.VMEM((D_FF, D_MODEL), _BF16),
                        pltpu.VMEM((MOE_BLK * SUBL, LANES), _F32),
                        pltpu.VMEM((MOE_BLK * SUBL, LANES), _F32),
                        pltpu.VMEM((MOE_BLK * SUBL, LANES), _F32),
                        pltpu.VMEM((MOE_BLK * SUBL, LANES), _F32),
                        pltpu.SemaphoreType.DMA((2,)),
                        pltpu.SemaphoreType.DMA((2,))],
    )
    return pl.pallas_call(
        _expert_kernel,
        grid_spec=grid_spec,
        out_shape=jax.ShapeDtypeStruct((N_MOE_ROWS + MOE_BLK, SUBL, LANES), _F32),
        compiler_params=_params(1),
        name="experts",
    )(blk_e, nvalid, tok3, tok3, slot3, slot3, n2, w1,
      b1.reshape(DEPTH, N_EXPERTS, 1, 2 * D_FF), w2, b2.reshape(DEPTH, N_EXPERTS, 1, D_MODEL))


def _combine_kernel(modrow_ref, h1_ref, mod_ref, ng_ref, y0_ref, y1_ref, y2_ref, y3_ref, w_ref, o_ref):
    mod = mod_ref[...]
    g2 = mod[:, 5 * D_MODEL:6 * D_MODEL]
    w = w_ref[...]
    f = jnp.zeros((TM, D_MODEL), _F32)
    for k, y_ref in enumerate((y0_ref, y1_ref, y2_ref, y3_ref)):
        f = f + _load_row_tiles(y_ref, TM) * w[:, k:k + 1]
    o_ref[...] = h1_ref[...] + g2 * _rms(f, ng_ref[3:4, :])


def _combine_call(h1, mod3, norm_g, yslot, topw, modrow):
    row = lambda i, a: (i, 0)
    choice = lambda k: pl.BlockSpec((TM * SUBL, LANES), lambda i, a: (k * N_TILES + i, 0))
    yslot = yslot.reshape(-1, LANES)
    grid_spec = pltpu.PrefetchScalarGridSpec(
        num_scalar_prefetch=1,
        grid=(N_TILES,),
        in_specs=[
            pl.BlockSpec((TM, D_MODEL), row),
            pl.BlockSpec((None, 1, 6 * D_MODEL), lambda i, a: (a[i], 0, 0)),
            pl.BlockSpec((4, D_MODEL), lambda i, a: (0, 0)),
            choice(0), choice(1), choice(2), choice(3),
            pl.BlockSpec((TM, LANES), row),
        ],
        out_specs=pl.BlockSpec((TM, D_MODEL), row),
    )
    return pl.pallas_call(
        _combine_kernel,
        grid_spec=grid_spec,
        out_shape=jax.ShapeDtypeStruct((T_ALL, D_MODEL), _F32),
        compiler_params=_params(1),
        name="moe_combine",
    )(modrow, h1, mod3, norm_g, yslot, yslot, yslot, yslot, topw)


def _tile_tables():
    t = np.arange(N_TILES)
    lat = t >= CTX_TILES
    j = (t - CTX_TILES) % LAT_TILES_PER_SEQ
    modrow = np.where(lat, 1 + (t - CTX_TILES) // LAT_TILES_PER_SEQ, 0)
    ropeblk = np.where(lat, j, LAT_TILES_PER_SEQ)
    stblk = np.where(lat, CTX_TILES, t)
    hasprev = np.where(lat & (j > 0), 1, 0)
    hasnext = np.where(lat & (j < LAT_TILES_PER_SEQ - 1), 1, 0)
    as_i32 = lambda a: jnp.asarray(a, jnp.int32)
    return as_i32(modrow), as_i32(ropeblk), as_i32(stblk), as_i32(hasprev), as_i32(hasnext)


def _rope_tables():
    half = HEAD_DIM // 4
    freqs = ROPE_THETA ** (-jnp.arange(half, dtype=_F32) / half)
    t = jnp.arange(DEC_SEQ)
    pos = jnp.stack([t // GRID_W, t % GRID_W], axis=1).astype(_F32)
    ang = pos[:, :, None] * freqs[None, None, :]
    cos = jnp.cos(ang)
    sin = jnp.sin(ang)
    cos_h = jnp.concatenate([cos, cos], axis=-1).reshape(DEC_SEQ, HEAD_DIM)
    sin_h = jnp.concatenate([-sin, sin], axis=-1).reshape(DEC_SEQ, HEAD_DIM)
    cos_t = jnp.tile(cos_h, (1, N_HEADS))
    sin_t = jnp.tile(sin_h, (1, N_HEADS))
    cos_t = jnp.concatenate([cos_t, jnp.ones((TM, BR_WIDTH), _F32)], axis=0)
    sin_t = jnp.concatenate([sin_t, jnp.zeros((TM, BR_WIDTH), _F32)], axis=0)
    return cos_t, sin_t


def _permute_heads_cols(w):
    lead = w.shape[:-1]
    return w.reshape(lead + (N_HEADS, HEAD_DIM))[..., jnp.array([0, 2, 1, 3]), :].reshape(lead + (BR_WIDTH,))


def _cache_rows(cache, l):
    return cache[:, l].reshape(DEC_BATCH, PAST_LEN, KV_WIDTH).astype(_BF16)


def _moe_layout(cnt, topi, rank):
    counts = cnt[0, :N_EXPERTS]
    padded = (counts + MOE_BLK - 1) // MOE_BLK * MOE_BLK
    pad_end = jnp.cumsum(padded)
    pad_start = pad_end - padded
    blk_row0 = jnp.arange(N_MOE_BLOCKS, dtype=jnp.int32) * MOE_BLK
    blk_e = jnp.minimum(jnp.sum(pad_end[None, :] <= blk_row0[:, None], axis=1),
                        N_EXPERTS - 1).astype(jnp.int32)
    nvalid = (pad_end[-1:] // MOE_BLK).astype(jnp.int32)
    spare_before = jnp.cumsum(padded - counts) - (padded - counts)
    row = jnp.arange(N_MOE_ROWS, dtype=jnp.int32).reshape(N_MOE_BLOCKS, MOE_BLK)
    in_tail = (blk_row0 >= pad_end[-1])[:, None]
    e = blk_e[:, None]
    spare_in_group = row - pad_start[e] - counts[e] + spare_before[e]
    spare_tail = row - pad_end[-1] + jnp.sum(padded - counts)
    spare = N_SLOTS + jnp.where(in_tail, spare_tail, spare_in_group)
    dest = pad_start[topi[:, :TOP_K]] + rank[:, :TOP_K]
    slot_id = (jnp.arange(TOP_K, dtype=jnp.int32)[None, :] * T_ALL
               + jnp.arange(T_ALL, dtype=jnp.int32)[:, None])
    slot_of_row = spare.reshape(-1).astype(jnp.int32).at[dest.reshape(-1)].set(slot_id.reshape(-1))
    tok_of_row = jnp.where(slot_of_row < N_SLOTS, slot_of_row % T_ALL, 0)
    return blk_e, nvalid, tok_of_row, slot_of_row


def kernel(x_prompt, x_sample, cache_win_k, cache_win_v, cache_full_k, cache_full_v, c, c_ctx,
           w_mod, b_mod, norm_g, w_in, a_ln_g, a_ln_b, w_sp, b_sp, conv_w, conv_b, b_ln_g, b_ln_b,
           sink, qn_g, kn_g, w_br, w_o, w_router, b_router, w1, b1, w2, b2):
    modrow, ropeblk, stblk, hasprev, hasnext = _tile_tables()
    cos_t, sin_t = _rope_tables()
    head_id = np.arange(BR_WIDTH) // HEAD_DIM
    ones_bd = jnp.asarray(head_id[:, None] == head_id[None, :], _BF16)
    ltri = jnp.asarray(np.tril(np.ones((TM, TM)), -1), _BF16)
    row2 = lambda v: v.reshape(1, -1)

    cvec = jnp.concatenate([c_ctx[None, :], c, jnp.zeros((8 - 1 - DEC_BATCH, D_MODEL), _F32)], axis=0)
    mod_all = _mod_call(cvec, w_mod, b_mod)

    h = jnp.concatenate([x_prompt.reshape(T_CTX, D_MODEL), x_sample.reshape(T_LAT, D_MODEL)], axis=0)
    states = []
    for l in range(DEPTH):
        mod3 = mod_all[l].reshape(8, 1, 6 * D_MODEL)
        wl = w_in[l]
        w_main = jnp.concatenate([
            wl[:, 0:1024],
            _permute_heads_cols(wl[:, 1024:1280]), wl[:, 1280:1536],
            _permute_heads_cols(wl[:, 1536:1792]), wl[:, 1792:2048]], axis=1).astype(_BF16)
        wg = wl[:, MAIN_WIDTH:].astype(_BF16)
        qn = jnp.tile(qn_g[l], N_HEADS).reshape(1, BR_WIDTH)
        kn = jnp.tile(kn_g[l], KV_HEADS).reshape(1, KV_WIDTH)

        ab, qc, kc, vc, qd, kd, vd, st = _inproj_call(
            h, mod3, row2(norm_g[l, 0]), w_main, qn, kn, ones_bd, cos_t, sin_t,
            (modrow, ropeblk, stblk))
        states.append(st[:T_CTX].reshape(BATCH, SEQ, 4, KV_HEADS, HEAD_DIM))

        bsp = jnp.repeat(b_sp[l].T, BR_WIDTH // A_GROUPS, axis=1)
        cw = jnp.concatenate([conv_w[l], jnp.zeros((1, BR_WIDTH), _F32)], axis=0)
        oa, ob = _mix_call(ab, row2(a_ln_g[l]), row2(a_ln_b[l]), w_sp[l].astype(_BF16), bsp, cw,
                           row2(conv_b[l]), row2(b_ln_g[l]), row2(b_ln_b[l]), (hasprev, hasnext))

        occ, odc = _ctx_attn_call(sink[l], qc, kc, vc, qd, kd, vd)
        ocl = _win_attn_call(sink[l], qc, kc, vc, _cache_rows(cache_win_k, l),
                             _cache_rows(cache_win_v, l))
        odl = _full_attn_call(qd, kd, vd, _cache_rows(cache_full_k, l),
                              _cache_rows(cache_full_v, l))

        perm = jnp.array([0, 2, 1, 3])
        wbr = w_br[l].reshape(N_BRANCH, N_HEADS, HEAD_DIM, D_MODEL)
        wbr = jnp.concatenate([wbr[0:2], wbr[2:4][:, perm]], axis=0)
        wbr = wbr.reshape(N_BRANCH, BR_WIDTH, D_MODEL).astype(_BF16)
        wr = jnp.concatenate([w_router[l], jnp.zeros((D_MODEL, LANES - N_EXPERTS), _F32)], axis=1)
        wr_hi, wr_lo = _hi_lo(wr)
        br = jnp.concatenate([b_router[l], jnp.full((LANES - N_EXPERTS,), NEG_BIG, _F32)]).reshape(1, LANES)
        h1, n2, topi, topw, rank, cnt = _merge_call(
            h, mod3, norm_g[l], oa, ob, occ, ocl, odc, odl, wg, wbr, w_o[l].astype(_BF16),
            wr_hi, wr_lo, br, ltri, modrow)

        blk_e, nvalid, tok_of_row, slot_of_row = _moe_layout(cnt, topi, rank)
        yslot = _expert_call(l, blk_e, nvalid, tok_of_row, slot_of_row,
                             n2.reshape(T_ALL, SUBL, LANES), w1, b1, w2, b2)
        h = _combine_call(h1, mod3, norm_g[l], yslot, topw, modrow)

    y_prompt = h[:T_CTX].reshape(BATCH, SEQ, D_MODEL)
    y_sample = h[T_CTX:].reshape(DEC_BATCH, DEC_SEQ, D_MODEL)
    st = jnp.stack(states, axis=1)
    return (y_prompt, y_sample, st[:, :, :, 0], st[:, :, :, 1], st[:, :, :, 2], st[:, :, :, 3])
```

```python
import functools

import numpy as np
import jax
import jax.numpy as jnp
from jax import lax
from jax.experimental import pallas as pl
from jax.experimental.pallas import tpu as pltpu

D_MODEL = 1024
BATCH = 16
SEQ = 256
DEPTH = 2
DEC_BATCH = 4
DEC_SEQ = 4096
PAST_LEN = 256
GRID_W = 64
HEAD_DIM = 64
BR_WIDTH = 256
N_BRANCH = 4
A_GROUPS = 4
CHUNK = 128
CONV_W = 31
N_HEADS = 4
KV_HEADS = 2
KV_WIDTH = KV_HEADS * HEAD_DIM
WINDOW = 128
ROPE_THETA = 10000.0
EPS = 1e-6
N_EXPERTS = 32
TOP_K = 4
D_FF = D_MODEL
SWIGLU_LIMIT = 7.0
SWIGLU_ALPHA = 1.702
MAIN_WIDTH = 2048

LANES = 128
T_CTX = BATCH * SEQ
T_LAT = DEC_BATCH * DEC_SEQ
T_ALL = T_CTX + T_LAT
TM = 256
N_TILES = T_ALL // TM
CTX_TILES = T_CTX // TM
LAT_TILES_PER_SEQ = DEC_SEQ // TM
HALO = 16
MOE_BLK = 256
N_SLOTS = T_ALL * TOP_K
N_MOE_BLOCKS = N_SLOTS // MOE_BLK + N_EXPERTS
N_MOE_ROWS = N_MOE_BLOCKS * MOE_BLK
WIN_TQ = 512
FULL_TQ = 256
FULL_TK = 512
NEG_BIG = -1e30
VMEM_LIMIT = 56 * 1024 * 1024

_F32 = jnp.float32
_BF16 = jnp.bfloat16


def _params(n_axes, vmem=None):
    return pltpu.CompilerParams(
        dimension_semantics=("arbitrary",) * n_axes,
        vmem_limit_bytes=vmem if vmem is not None else VMEM_LIMIT)


def _split_dot(a, b_hi, b_lo):
    a_hi = a.astype(_BF16)
    a_lo = (a - a_hi.astype(_F32)).astype(_BF16)
    return (jnp.dot(a_hi, b_hi, preferred_element_type=_F32)
            + jnp.dot(a_lo, b_hi, preferred_element_type=_F32)
            + jnp.dot(a_hi, b_lo, preferred_element_type=_F32))


def _hi_lo(w):
    hi = w.astype(_BF16)
    lo = (w - hi.astype(_F32)).astype(_BF16)
    return hi, lo


def _mod_kernel(c_ref, w_ref, b_ref, o_ref):
    c = c_ref[...]
    s = c * jax.nn.sigmoid(c)
    w = w_ref[...]
    w_hi = w.astype(_BF16)
    w_lo = (w - w_hi.astype(_F32)).astype(_BF16)
    o_ref[...] = _split_dot(s, w_hi, w_lo) + b_ref[...]


def _mod_call(cvec, w_mod, b_mod):
    tn = 1024
    return pl.pallas_call(
        _mod_kernel,
        grid=(DEPTH, 6 * D_MODEL // tn),
        in_specs=[
            pl.BlockSpec((8, D_MODEL), lambda l, j: (0, 0)),
            pl.BlockSpec((None, D_MODEL, tn), lambda l, j: (l, 0, j)),
            pl.BlockSpec((None, 1, tn), lambda l, j: (l, 0, j)),
        ],
        out_specs=pl.BlockSpec((None, 8, tn), lambda l, j: (l, 0, j)),
        out_shape=jax.ShapeDtypeStruct((DEPTH, 8, 6 * D_MODEL), _F32),
        compiler_params=_params(2),
        name="mod",
    )(cvec, w_mod, b_mod.reshape(DEPTH, 1, 6 * D_MODEL))


def _rms(x, g):
    ms = jnp.mean(x * x, axis=-1, keepdims=True)
    return x * lax.rsqrt(ms + EPS) * g


def _head_rms(x, ones_bd, g):
    xx = x * x
    hi = xx.astype(_BF16)
    lo = (xx - hi.astype(_F32)).astype(_BF16)
    ss = (jnp.dot(hi, ones_bd, preferred_element_type=_F32)
          + jnp.dot(lo, ones_bd, preferred_element_type=_F32))
    return x * lax.rsqrt(ss * (1.0 / HEAD_DIM) + EPS) * g


def _rope(x, cos, sin_signed):
    w = x.shape[-1]
    lane = lax.broadcasted_iota(jnp.int32, x.shape, 1)
    first = (lane % 32) < 16
    partner = jnp.where(first, pltpu.roll(x, w - 16, 1), pltpu.roll(x, 16, 1))
    return x * cos + partner * sin_signed


def _inproj_kernel(modrow_ref, ropeblk_ref, stblk_ref,
                   xc_ref, xl_ref, mod_ref, g0_ref, w_ref, qn_ref, kn_ref, ones_ref,
                   cos_ref, sin_ref,
                   ab_ref, qc_ref, kc_ref, vc_ref, qd_ref, kd_ref, vd_ref, st_ref):
    x = jnp.where(pl.program_id(0) < CTX_TILES, xc_ref[...], xl_ref[...])
    mod = mod_ref[...]
    sh1 = mod[:, 0:D_MODEL]
    sc1 = mod[:, D_MODEL:2 * D_MODEL]
    n = _rms(x, g0_ref[...]) * (1.0 + sc1) + sh1
    z = jnp.dot(n.astype(_BF16), w_ref[...], preferred_element_type=_F32)
    ab_ref[...] = z[:, 0:1024]
    cq = z[:, 1024:1280]
    ck = z[:, 1280:1408]
    cv = z[:, 1408:1536]
    dq = z[:, 1536:1792]
    dk = z[:, 1792:1920]
    dv = z[:, 1920:2048]
    ones = ones_ref[...]
    dq = _head_rms(dq, ones, qn_ref[...])
    dk = _head_rms(dk, ones[0:KV_WIDTH, 0:KV_WIDTH], kn_ref[...])
    cos = cos_ref[...]
    sin = sin_ref[...]
    scale = HEAD_DIM ** -0.5
    qc_ref[...] = (_rope(cq, cos, sin) * scale).astype(_BF16)
    qd_ref[...] = (_rope(dq, cos, sin) * scale).astype(_BF16)
    kc = _rope(ck, cos[:, 0:KV_WIDTH], sin[:, 0:KV_WIDTH])
    kd = _rope(dk, cos[:, 0:KV_WIDTH], sin[:, 0:KV_WIDTH])
    kc_ref[...] = kc.astype(_BF16)
    kd_ref[...] = kd.astype(_BF16)
    vc_ref[...] = cv.astype(_BF16)
    vd_ref[...] = dv.astype(_BF16)
    st_ref[:, 0:128] = kc
    st_ref[:, 128:256] = cv
    st_ref[:, 256:384] = kd
    st_ref[:, 384:512] = dv


def _ctx_rows(*_):
    i = _[0]
    return (jnp.minimum(i, CTX_TILES - 1), 0)


def _lat_rows(*_):
    i = _[0]
    return (jnp.maximum(i - CTX_TILES, 0), 0)


def _inproj_call(xc, xl, mod3, g0, w_main, qn, kn, ones_bd, cos_t, sin_t, tables):
    modrow, ropeblk, stblk = tables
    row = lambda i, a, b, c: (i, 0)
    full = lambda i, a, b, c: (0, 0)
    bf = lambda w: jax.ShapeDtypeStruct((T_ALL, w), _BF16)
    grid_spec = pltpu.PrefetchScalarGridSpec(
        num_scalar_prefetch=3,
        grid=(N_TILES,),
        in_specs=[
            pl.BlockSpec((TM, D_MODEL), _ctx_rows),
            pl.BlockSpec((TM, D_MODEL), _lat_rows),
            pl.BlockSpec((None, 1, 6 * D_MODEL), lambda i, a, b, c: (a[i], 0, 0)),
            pl.BlockSpec((1, D_MODEL), full),
            pl.BlockSpec((D_MODEL, MAIN_WIDTH), full),
            pl.BlockSpec((1, BR_WIDTH), full),
            pl.BlockSpec((1, KV_WIDTH), full),
            pl.BlockSpec((BR_WIDTH, BR_WIDTH), full),
            pl.BlockSpec((TM, BR_WIDTH), lambda i, a, b, c: (b[i], 0)),
            pl.BlockSpec((TM, BR_WIDTH), lambda i, a, b, c: (b[i], 0)),
        ],
        out_specs=[
            pl.BlockSpec((TM, 1024), row),
            pl.BlockSpec((TM, BR_WIDTH), row),
            pl.BlockSpec((TM, KV_WIDTH), row),
            pl.BlockSpec((TM, KV_WIDTH), row),
            pl.BlockSpec((TM, BR_WIDTH), row),
            pl.BlockSpec((TM, KV_WIDTH), row),
            pl.BlockSpec((TM, KV_WIDTH), row),
            pl.BlockSpec((TM, 512), lambda i, a, b, c: (c[i], 0)),
        ],
    )
    return pl.pallas_call(
        _inproj_kernel,
        grid_spec=grid_spec,
        out_shape=[
            jax.ShapeDtypeStruct((T_ALL, 1024), _F32),
            bf(BR_WIDTH), bf(KV_WIDTH), bf(KV_WIDTH),
            bf(BR_WIDTH), bf(KV_WIDTH), bf(KV_WIDTH),
            jax.ShapeDtypeStruct((T_CTX + TM, 512), _F32),
        ],
        compiler_params=_params(1),
        name="inproj",
    )(modrow, ropeblk, stblk, xc, xl, mod3, g0, w_main, qn, kn, ones_bd, cos_t, sin_t)


def _layernorm(x, g, b):
    mu = jnp.mean(x, axis=-1, keepdims=True)
    xc = x - mu
    var = jnp.mean(xc * xc, axis=-1, keepdims=True)
    return xc * lax.rsqrt(var + EPS) * g + b


def _gelu(x):
    return 0.5 * x * (1.0 + lax.erf(x * (2.0 ** -0.5)))


def _mix_kernel(hasprev_ref, hasnext_ref,
                ab_ref, prev_ref, next_ref, alg_ref, alb_ref, wsp_ref, bsp_ref,
                cw_ref, cb_ref, blg_ref, blb_ref,
                oa_ref, ob_ref, buf_ref):
    i = pl.program_id(0)
    ab = ab_ref[...]
    u = _gelu(ab[:, 0:256])
    va = _layernorm(_gelu(ab[:, 256:512]), alg_ref[...], alb_ref[...])
    vab = va.astype(_BF16)
    lane = lax.broadcasted_iota(jnp.int32, (CHUNK, BR_WIDTH), 1)
    group = lane // (BR_WIDTH // A_GROUPS)
    bsp = bsp_ref[...]
    for c in range(TM // CHUNK):
        v_c = vab[c * CHUNK:(c + 1) * CHUNK, :]
        sp = bsp
        for g in range(A_GROUPS):
            full = jnp.dot(wsp_ref[g], v_c, preferred_element_type=_F32)
            sp = sp + jnp.where(group == g, full, 0.0)
        oa_ref[c * CHUNK:(c + 1) * CHUNK, :] = (u[c * CHUNK:(c + 1) * CHUNK, :] * sp).astype(_BF16)
    glu = ab[:, 512:768] * jax.nn.sigmoid(ab[:, 768:1024])
    pv = prev_ref[...]
    nx = next_ref[...]
    hp = hasprev_ref[i].astype(_F32)
    hn = hasnext_ref[i].astype(_F32)
    buf_ref[0:HALO, :] = pv[:, 0:256] * jax.nn.sigmoid(pv[:, 256:512]) * hp
    buf_ref[HALO:HALO + TM, :] = glu
    buf_ref[HALO + TM:HALO + TM + HALO, :] = nx[:, 0:256] * jax.nn.sigmoid(nx[:, 256:512]) * hn
    cw = cw_ref[...]
    acc = jnp.zeros((TM, BR_WIDTH), _F32) + cb_ref[...]
    off = HALO - CONV_W // 2
    for k in range(CONV_W):
        acc = acc + buf_ref[off + k:off + k + TM, :] * cw[k:k + 1, :]
    y = _layernorm(acc, blg_ref[...], blb_ref[...])
    ob_ref[...] = (y * jax.nn.sigmoid(y)).astype(_BF16)


def _mix_call(ab, a_ln_g, a_ln_b, wsp, bsp, conv_w, conv_b, b_ln_g, b_ln_b, tables):
    hasprev, hasnext = tables
    n_halo_blocks = T_ALL // HALO
    per = TM // HALO
    full2 = lambda i, a, b: (0, 0)
    grid_spec = pltpu.PrefetchScalarGridSpec(
        num_scalar_prefetch=2,
        grid=(N_TILES,),
        in_specs=[
            pl.BlockSpec((TM, 1024), lambda i, a, b: (i, 0)),
            pl.BlockSpec((HALO, 512), lambda i, a, b: (jnp.maximum(i * per - 1, 0), 1)),
            pl.BlockSpec((HALO, 512), lambda i, a, b: (jnp.minimum((i + 1) * per, n_halo_blocks - 1), 1)),
            pl.BlockSpec((1, BR_WIDTH), full2),
            pl.BlockSpec((1, BR_WIDTH), full2),
            pl.BlockSpec((A_GROUPS, CHUNK, CHUNK), lambda i, a, b: (0, 0, 0)),
            pl.BlockSpec((CHUNK, BR_WIDTH), full2),
            pl.BlockSpec((32, BR_WIDTH), full2),
            pl.BlockSpec((1, BR_WIDTH), full2),
            pl.BlockSpec((1, BR_WIDTH), full2),
            pl.BlockSpec((1, BR_WIDTH), full2),
        ],
        out_specs=[
            pl.BlockSpec((TM, BR_WIDTH), lambda i, a, b: (i, 0)),
            pl.BlockSpec((TM, BR_WIDTH), lambda i, a, b: (i, 0)),
        ],
        scratch_shapes=[pltpu.VMEM((TM + 2 * HALO, BR_WIDTH), _F32)],
    )
    return pl.pallas_call(
        _mix_kernel,
        grid_spec=grid_spec,
        out_shape=[jax.ShapeDtypeStruct((T_ALL, BR_WIDTH), _BF16)] * 2,
        compiler_params=_params(1),
        name="mixers_ab",
    )(hasprev, hasnext, ab, ab, ab, a_ln_g, a_ln_b, wsp, bsp, conv_w, conv_b, b_ln_g, b_ln_b)


def _lane_masks(n_rows):
    lane = lax.broadcasted_iota(jnp.int32, (n_rows, LANES), 1)
    return lane < HEAD_DIM, lane >= HEAD_DIM


def _group_queries(q, kv):
    tq = q.shape[0]
    lo, hi = _lane_masks(tq)
    keep = lo if kv == 0 else hi
    zero = jnp.zeros((), q.dtype)
    return jnp.concatenate([jnp.where(keep, q[:, 0:LANES], zero),
                            jnp.where(keep, q[:, LANES:2 * LANES], zero)], axis=0)


def _scores(qs, k):
    return lax.dot_general(qs, k, (((1,), (1,)), ((), ())), preferred_element_type=_F32)


def _attend_once(q, chunks, sinks):
    tq = q.shape[0]
    lo, hi = _lane_masks(tq)
    outs = [jnp.zeros((tq, LANES), _F32), jnp.zeros((tq, LANES), _F32)]
    for kv in range(KV_HEADS):
        qs = _group_queries(q, kv)
        ss = []
        for k, _, mask in chunks:
            s = _scores(qs, k)
            if mask is not None:
                s = jnp.where(mask, s, -jnp.inf)
            ss.append(s)
        m = ss[0].max(axis=-1, keepdims=True)
        for s in ss[1:]:
            m = jnp.maximum(m, s.max(axis=-1, keepdims=True))
        if sinks is not None:
            row = lax.broadcasted_iota(jnp.int32, (2 * tq, 1), 0)
            sink_col = jnp.where(row < tq, sinks[2 * kv], sinks[2 * kv + 1])
            m = jnp.maximum(m, sink_col)
            l = jnp.exp(sink_col - m)
        else:
            l = jnp.zeros((2 * tq, 1), _F32)
        acc = jnp.zeros((2 * tq, LANES), _F32)
        for s, (_, v, _) in zip(ss, chunks):
            p = jnp.exp(s - m)
            l = l + p.sum(axis=-1, keepdims=True)
            acc = acc + jnp.dot(p.astype(_BF16), v, preferred_element_type=_F32)
        o = acc / l
        keep = lo if kv == 0 else hi
        outs[0] = outs[0] + jnp.where(keep, o[0:tq], 0.0)
        outs[1] = outs[1] + jnp.where(keep, o[tq:2 * tq], 0.0)
    return jnp.concatenate(outs, axis=1)


def _ctx_attn_kernel(sink_ref, qc_ref, kc_ref, vc_ref, qd_ref, kd_ref, vd_ref, oc_ref, od_ref):
    sinks = [sink_ref[h] for h in range(N_HEADS)]
    oc_ref[...] = _attend_once(qc_ref[...], [(kc_ref[...], vc_ref[...], None)], sinks).astype(_BF16)
    od_ref[...] = _attend_once(qd_ref[...], [(kd_ref[...], vd_ref[...], None)], None).astype(_BF16)


def _ctx_attn_call(sink, qc, kc, vc, qd, kd, vd):
    seq = lambda b: (b, 0)
    q_spec = pl.BlockSpec((SEQ, BR_WIDTH), seq)
    kv_spec = pl.BlockSpec((SEQ, KV_WIDTH), seq)
    return pl.pallas_call(
        _ctx_attn_kernel,
        grid=(BATCH,),
        in_specs=[pl.BlockSpec(memory_space=pltpu.SMEM),
                  q_spec, kv_spec, kv_spec, q_spec, kv_spec, kv_spec],
        out_specs=[q_spec, q_spec],
        out_shape=[jax.ShapeDtypeStruct((T_CTX, BR_WIDTH), _BF16)] * 2,
        compiler_params=_params(1),
        name="ctx_attention",
    )(sink, qc, kc, vc, qd, kd, vd)


def _win_attn_kernel(sink_ref, q_ref, kcur_ref, kprev_ref, knext_ref, vcur_ref, vprev_ref,
                     vnext_ref, kctx_ref, vctx_ref, oc_ref):
    j = pl.program_id(1)
    nq = pl.num_programs(1)
    sinks = [sink_ref[h] for h in range(N_HEADS)]
    tq = WIN_TQ
    def rel(n_keys):
        qi = lax.broadcasted_iota(jnp.int32, (2 * tq, n_keys), 0) % tq
        kj = lax.broadcasted_iota(jnp.int32, (2 * tq, n_keys), 1)
        return qi, kj
    qi, kj = rel(tq)
    m_cur = jnp.abs(qi - kj) <= WINDOW
    qi, kj = rel(WINDOW)
    m_prev = ((qi - (kj - WINDOW)) <= WINDOW) & (j > 0)
    m_next = (((kj + tq) - qi) <= WINDOW) & (j < nq - 1)
    chunks = [
        (kcur_ref[...], vcur_ref[...], m_cur),
        (kprev_ref[...], vprev_ref[...], m_prev),
        (knext_ref[...], vnext_ref[...], m_next),
        (kctx_ref[...], vctx_ref[...], None),
    ]
    oc_ref[...] = _attend_once(q_ref[...], chunks, sinks).astype(_BF16)


def _win_attn_call(sink, qc, kc, vc, kctx, vctx):
    nq = DEC_SEQ // WIN_TQ
    ctx_q = T_CTX // WIN_TQ
    sub = WIN_TQ // WINDOW
    ctx_w = T_CTX // WINDOW
    per_seq_w = DEC_SEQ // WINDOW
    cur = lambda b, j: (ctx_q + b * nq + j, 0)
    prev = lambda b, j: (ctx_w + b * per_seq_w + jnp.maximum(j * sub - 1, 0), 0)
    nxt = lambda b, j: (ctx_w + b * per_seq_w + jnp.minimum((j + 1) * sub, per_seq_w - 1), 0)
    cache = lambda b, j: (b, 0, 0)
    return pl.pallas_call(
        _win_attn_kernel,
        grid=(DEC_BATCH, nq),
        in_specs=[
            pl.BlockSpec(memory_space=pltpu.SMEM),
            pl.BlockSpec((WIN_TQ, BR_WIDTH), cur),
            pl.BlockSpec((WIN_TQ, KV_WIDTH), cur),
            pl.BlockSpec((WINDOW, KV_WIDTH), prev),
            pl.BlockSpec((WINDOW, KV_WIDTH), nxt),
            pl.BlockSpec((WIN_TQ, KV_WIDTH), cur),
            pl.BlockSpec((WINDOW, KV_WIDTH), prev),
            pl.BlockSpec((WINDOW, KV_WIDTH), nxt),
            pl.BlockSpec((None, PAST_LEN, KV_WIDTH), cache),
            pl.BlockSpec((None, PAST_LEN, KV_WIDTH), cache),
        ],
        out_specs=pl.BlockSpec((WIN_TQ, BR_WIDTH), lambda b, j: (b * nq + j, 0)),
        out_shape=jax.ShapeDtypeStruct((T_LAT, BR_WIDTH), _BF16),
        compiler_params=_params(2),
        name="window_attention",
    )(sink, qc, kc, kc, kc, vc, vc, vc, kctx, vctx)


def _full_attn_kernel(q_ref, k_ref, v_ref, kctx_ref, vctx_ref, od_ref):
    tq = FULL_TQ
    q = q_ref[...]
    lo, hi = _lane_masks(tq)
    qss = [_group_queries(q, kv) for kv in range(KV_HEADS)]

    def update(carries, k, v):
        new = []
        for qs, (m, l, acc) in zip(qss, carries):
            s = _scores(qs, k)
            m_new = jnp.maximum(m, s.max(axis=-1, keepdims=True))
            a = jnp.exp(m - m_new)
            p = jnp.exp(s - m_new)
            l = a * l + p.sum(axis=-1, keepdims=True)
            acc = a * acc + jnp.dot(p.astype(_BF16), v, preferred_element_type=_F32)
            new.append((m_new, l, acc))
        return tuple(new)

    def body(c, carries):
        start = pl.multiple_of(c * FULL_TK, FULL_TK)
        return update(carries, k_ref[pl.ds(start, FULL_TK), :], v_ref[pl.ds(start, FULL_TK), :])

    init = (jnp.full((2 * tq, 1), -jnp.inf, _F32), jnp.zeros((2 * tq, 1), _F32),
            jnp.zeros((2 * tq, LANES), _F32))
    carries = lax.fori_loop(0, DEC_SEQ // FULL_TK, body, (init, init))
    carries = update(carries, kctx_ref[...], vctx_ref[...])
    outs = [jnp.zeros((tq, LANES), _F32), jnp.zeros((tq, LANES), _F32)]
    for kv, (m, l, acc) in enumerate(carries):
        o = acc / l
        keep = lo if kv == 0 else hi
        outs[0] = outs[0] + jnp.where(keep, o[0:tq], 0.0)
        outs[1] = outs[1] + jnp.where(keep, o[tq:2 * tq], 0.0)
    od_ref[...] = jnp.concatenate(outs, axis=1).astype(_BF16)


def _full_attn_call(qd, kd, vd, kctx, vctx):
    nq = DEC_SEQ // FULL_TQ
    ctx_q = T_CTX // FULL_TQ
    ctx_seq = T_CTX // DEC_SEQ
    cur = lambda b, j: (ctx_q + b * nq + j, 0)
    seq = lambda b, j: (ctx_seq + b, 0)
    cache = lambda b, j: (b, 0, 0)
    return pl.pallas_call(
        _full_attn_kernel,
        grid=(DEC_BATCH, nq),
        in_specs=[
            pl.BlockSpec((FULL_TQ, BR_WIDTH), cur),
            pl.BlockSpec((DEC_SEQ, KV_WIDTH), seq),
            pl.BlockSpec((DEC_SEQ, KV_WIDTH), seq),
            pl.BlockSpec((None, PAST_LEN, KV_WIDTH), cache),
            pl.BlockSpec((None, PAST_LEN, KV_WIDTH), cache),
        ],
        out_specs=pl.BlockSpec((FULL_TQ, BR_WIDTH), lambda b, j: (b * nq + j, 0)),
        out_shape=jax.ShapeDtypeStruct((T_LAT, BR_WIDTH), _BF16),
        compiler_params=_params(2),
        name="full_attention",
    )(qd, kd, vd, kctx, vctx)


SUBL = D_MODEL // LANES


def _store_row_tiles(ref, n, x):
    for c in range(SUBL):
        ref[pl.ds(c, n, stride=SUBL), :] = x[:, c * LANES:(c + 1) * LANES]


def _load_row_tiles(ref, n):
    return jnp.concatenate([ref[pl.ds(c, n, stride=SUBL), :] for c in range(SUBL)], axis=1)


def _merge_kernel(modrow_ref,
                  xc_ref, xl_ref, mod_ref, ng_ref, oa_ref, ob_ref, occ_ref, ocl_ref, odc_ref, odl_ref,
                  wg_ref, wbr_ref, wo_ref, wrh_ref, wrl_ref, br_ref, ltri_ref,
                  h1_ref, n2_ref, topi_ref, topw_ref, rank_ref, cnt_ref, base_ref):
    i = pl.program_id(0)

    @pl.when(i == 0)
    def _():
        base_ref[...] = jnp.zeros_like(base_ref)

    is_ctx = i < CTX_TILES
    x = jnp.where(is_ctx, xc_ref[...], xl_ref[...])
    mod = mod_ref[...]
    sh1 = mod[:, 0:D_MODEL]
    sc1 = mod[:, D_MODEL:2 * D_MODEL]
    g1 =mod[:, 2 * D_MODEL:3 * D_MODEL]
    sh2 = mod[:, 3 * D_MODEL:4 * D_MODEL]
    sc2 = mod[:, 4 * D_MODEL:5 * D_MODEL]
    ng = ng_ref[...]
    nb = (_rms(x, ng[0:1, :]) * (1.0 + sc1) + sh1).astype(_BF16)
    oc = jnp.where(is_ctx, occ_ref[...], ocl_ref[...])
    od = jnp.where(is_ctx, odc_ref[...], odl_ref[...])
    branches = (oa_ref[...], ob_ref[...], oc, od)
    mixed = jnp.zeros((TM, D_MODEL), _F32)
    for k in range(N_BRANCH):
        gate = jax.nn.sigmoid(jnp.dot(nb, wg_ref[:, k * D_MODEL:(k + 1) * D_MODEL],
                                      preferred_element_type=_F32))
        proj = jnp.dot(branches[k], wbr_ref[k], preferred_element_type=_F32)
        mixed = mixed + gate * proj
    mix = jnp.dot(mixed.astype(_BF16), wo_ref[...], preferred_element_type=_F32)
    h1 = x + g1 * _rms(mix, ng[1:2, :])
    h1_ref[...] = h1
    n2 = _rms(h1, ng[2:3, :]) * (1.0 + sc2) + sh2
    _store_row_tiles(n2_ref, TM, n2)
    logits = _split_dot(n2, wrh_ref[...], wrl_ref[...]) + br_ref[...]
    lane = lax.broadcasted_iota(jnp.int32, (TM, LANES), 1).astype(_F32)
    work = logits
    vals, idxs = [], []
    for _ in range(TOP_K):
        m = work.max(axis=-1, keepdims=True)
        idx = jnp.where(work == m, lane, float(LANES)).min(axis=-1, keepdims=True)
        vals.append(m)
        idxs.append(idx)
        work = jnp.where(lane == idx, -jnp.inf, work)
    es = [jnp.exp(v - vals[0]) for v in vals]
    denom = es[0] + es[1] + es[2] + es[3]
    ltri = ltri_ref[...]
    running = base_ref[0:1, :]
    topi = jnp.zeros((TM, LANES), _F32)
    topw = jnp.zeros((TM, LANES), _F32)
    rank = jnp.zeros((TM, LANES), _F32)
    for k in range(TOP_K):
        onehot = (lane == idxs[k]).astype(_F32)
        before = jnp.dot(ltri, onehot.astype(_BF16), preferred_element_type=_F32)
        r = (onehot * (before + running)).sum(axis=-1, keepdims=True)
        running = running + onehot.sum(axis=0, keepdims=True)
        topi = jnp.where(lane == k, idxs[k], topi)
        topw = jnp.where(lane == k, es[k] / denom, topw)
        rank = jnp.where(lane == k, r, rank)
    base_ref[0:1, :] = running
    topi_ref[...] = topi.astype(jnp.int32)
    topw_ref[...] = topw
    rank_ref[...] = rank.astype(jnp.int32)
    cnt_ref[...] = jnp.broadcast_to(running, (8, LANES)).astype(jnp.int32)


def _merge_call(xc, xl, mod3, norm_g, oa, ob, occ, ocl, odc, odl, wg, wbr, wo, wr_hi, wr_lo, br, ltri,
                modrow):
    row = lambda i, a: (i, 0)
    full = lambda i, a: (0, 0)
    act = pl.BlockSpec((TM, BR_WIDTH), row)
    act_ctx = pl.BlockSpec((TM, BR_WIDTH), lambda i, a: (jnp.minimum(i, CTX_TILES - 1), 0))
    act_lat = pl.BlockSpec((TM, BR_WIDTH), lambda i, a: (jnp.maximum(i - CTX_TILES, 0), 0))
    wide = pl.BlockSpec((TM, LANES), row)
    grid_spec = pltpu.PrefetchScalarGridSpec(
        num_scalar_prefetch=1,
        grid=(N_TILES,),
        in_specs=[
            pl.BlockSpec((TM, D_MODEL), _ctx_rows),
            pl.BlockSpec((TM, D_MODEL), _lat_rows),
            pl.BlockSpec((None, 1, 6 * D_MODEL), lambda i, a: (a[i], 0, 0)),
            pl.BlockSpec((4, D_MODEL), full),
            act, act, act_ctx, act_lat, act_ctx, act_lat,
            pl.BlockSpec((D_MODEL, N_BRANCH * D_MODEL), full),
            pl.BlockSpec((N_BRANCH, BR_WIDTH, D_MODEL), lambda i, a: (0, 0, 0)),
            pl.BlockSpec((D_MODEL, D_MODEL), full),
            pl.BlockSpec((D_MODEL, LANES), full),
            pl.BlockSpec((D_MODEL, LANES), full),
            pl.BlockSpec((1, LANES), full),
            pl.BlockSpec((TM, TM), full),
        ],
        out_specs=[
            pl.BlockSpec((TM, D_MODEL), row),
            pl.BlockSpec((TM * SUBL, LANES), row),
            wide, wide, wide,
            pl.BlockSpec((8, LANES), full),
        ],
        scratch_shapes=[pltpu.VMEM((8, LANES), _F32)],
    )
    return pl.pallas_call(
        _merge_kernel,
        grid_spec=grid_spec,
        out_shape=[
            jax.ShapeDtypeStruct((T_ALL, D_MODEL), _F32),
            jax.ShapeDtypeStruct((T_ALL * SUBL, LANES), _F32),
            jax.ShapeDtypeStruct((T_ALL, LANES), jnp.int32),
            jax.ShapeDtypeStruct((T_ALL, LANES), _F32),
            jax.ShapeDtypeStruct((T_ALL, LANES), jnp.int32),
            jax.ShapeDtypeStruct((8, LANES), jnp.int32),
        ],
        compiler_params=_params(1),
        name="merge_router",
    )(modrow, xc, xl, mod3, norm_g, oa, ob, occ, ocl, odc, odl, wg, wbr, wo, wr_hi, wr_lo, br, ltri)


def _expert_kernel(blk_e_ref, nvalid_ref,
                   tokc_ref, tokn_ref, slotp_ref, slotc_ref, n2_hbm, w1_ref, b1_ref, w2_ref, b2_ref,
                   y_hbm,
                   w1b_ref, w2b_ref, x0_ref, x1_ref, y0_ref, y1_ref, gsem, ssem):
    i = pl.program_id(0)
    nb = pl.num_programs(0)
    nvalid = nvalid_ref[0]
    xbufs = (x0_ref, x1_ref)
    ybufs = (y0_ref, y1_ref)

    def gather_copy(tok, r, p):
        return pltpu.make_async_copy(n2_hbm.at[tok], xbufs[p].at[pl.ds(SUBL * r, SUBL), :],
                                     gsem.at[p])

    def scatter_copy(slot, r, p):
        return pltpu.make_async_copy(ybufs[p].at[pl.ds(SUBL * r, SUBL), :], y_hbm.at[slot],
                                     ssem.at[p])

    def wait_gather(p):
        for r in range(MOE_BLK):
            gather_copy(0, r, p).wait()

    def wait_scatter(p):
        for r in range(MOE_BLK):
            scatter_copy(0, r, p).wait()

    @pl.when(i == 0)
    def _():
        y1_ref[...] = jnp.zeros_like(y1_ref)
        for r in range(MOE_BLK):
            gather_copy(tokc_ref[0, r], r, 0).start()

    prev = blk_e_ref[jnp.maximum(i - 1, 0)]
    fresh = (i == 0) | (blk_e_ref[i] != prev)

    @pl.when(fresh)
    def _():
        w1b_ref[...] = w1_ref[...].astype(_BF16)
        w2b_ref[...] = w2_ref[...].astype(_BF16)

    def valid_step(p):
        q = 1 - p
        wait_gather(p)

        @pl.when(i >= 1)
        def _():
            wait_scatter(p)

        for r in range(MOE_BLK):
            gather_copy(tokn_ref[0, r], r, q).start(priority=r % 2)
            scatter_copy(slotp_ref[0, r], r, q).start(priority=(r + 1) % 2)
        x = _load_row_tiles(xbufs[p], MOE_BLK).astype(_BF16)
        hh = jnp.dot(x, w1b_ref[...], preferred_element_type=_F32) + b1_ref[...]
        x_glu = jnp.minimum(hh[:, 0:D_FF], SWIGLU_LIMIT)
        x_lin = jnp.clip(hh[:, D_FF:2 * D_FF], -SWIGLU_LIMIT, SWIGLU_LIMIT)
        act = x_glu * jax.nn.sigmoid(SWIGLU_ALPHA * x_glu) * (x_lin + 1.0)
        _store_row_tiles(ybufs[p], MOE_BLK,
                         jnp.dot(act.astype(_BF16), w2b_ref[...], preferred_element_type=_F32)
                         + b2_ref[...])

    def tail_step(p):
        q = 1 - p

        @pl.when(i == nvalid)
        def _():
            wait_gather(p)

        wait_scatter(p)
        ybufs[p][...] = jnp.zeros((MOE_BLK * SUBL, LANES), _F32)
        for r in range(MOE_BLK):
            scatter_copy(slotp_ref[0, r], r, q).start()

        @pl.when(i == nb - 1)
        def _():
            wait_scatter(q)
            for r in range(MOE_BLK):
                scatter_copy(slotc_ref[0, r], r, p).start()
            wait_scatter(p)

    is_valid = i < nvalid
    for p in range(2):
        pl.when(is_valid & (i % 2 == p))(functools.partial(valid_step, p))
        pl.when((~is_valid) & (i % 2 == p))(functools.partial(tail_step, p))


def _expert_call(l, blk_e, nvalid, tok_of_row, slot_of_row, n2, w1, b1, w2, b2):
    wsel = lambda i, e, n: (l, e[i], 0, 0)
    cur = lambda i, e, n: (i, 0, 0)
    nxt = lambda i, e, n: (jnp.minimum(i + 1, N_MOE_BLOCKS - 1), 0, 0)
    smem_blk = lambda imap: pl.BlockSpec((None, 1, MOE_BLK), imap, memory_space=pltpu.SMEM)
    tok3 = tok_of_row.reshape(N_MOE_BLOCKS, 1, MOE_BLK)
    virtual = N_MOE_ROWS + jnp.arange(MOE_BLK, dtype=jnp.int32)
    slot3 = jnp.concatenate([virtual, slot_of_row]).reshape(N_MOE_BLOCKS + 1, 1, MOE_BLK)
    grid_spec = pltpu.PrefetchScalarGridSpec(
        num_scalar_prefetch=2,
        grid=(N_MOE_BLOCKS,),
        in_specs=[
            smem_blk(cur), smem_blk(nxt),
            smem_blk(cur),
            smem_blk(lambda i, e, n: (i + 1, 0, 0)),
            pl.BlockSpec(memory_space=pl.ANY),
            pl.BlockSpec((None, None, D_MODEL, 2 * D_FF), wsel),
            pl.BlockSpec((None, None, 1, 2 * D_FF), wsel),
            pl.BlockSpec((None, None, D_FF, D_MODEL), wsel),
            pl.BlockSpec((None, None, 1, D_MODEL), wsel),
        ],
        out_specs=pl.BlockSpec(memory_space=pl.ANY),
        scratch_shapes=[pltpu.VMEM((D_MODEL, 2 * D_FF), _BF16),
                        pltpu.VMEM((D_FF, D_MODEL), _BF16),
                        pltpu.VMEM((MOE_BLK * SUBL, LANES), _F32),
                        pltpu.VMEM((MOE_BLK * SUBL, LANES), _F32),
                        pltpu.VMEM((MOE_BLK * SUBL, LANES), _F32),
                        pltpu.VMEM((MOE_BLK * SUBL, LANES), _F32),
                        pltpu.SemaphoreType.DMA((2,)),
                        pltpu.SemaphoreType.DMA((2,))],
    )
    return pl.pallas_call(
        _expert_kernel,
        grid_spec=grid_spec,
        out_shape=jax.ShapeDtypeStruct((N_MOE_ROWS + MOE_BLK, SUBL, LANES), _F32),
        compiler_params=_params(1),
        name="experts",
    )(blk_e, nvalid, tok3, tok3, slot3, slot3, n2, w1,
      b1.reshape(DEPTH, N_EXPERTS, 1, 2 * D_FF), w2, b2.reshape(DEPTH, N_EXPERTS, 1, D_MODEL))


def _combine_kernel(modrow_ref, h1_ref, mod_ref, ng_ref, y0_ref, y1_ref, y2_ref, y3_ref, w_ref,
                    oc_ref, ol_ref):
    i = pl.program_id(0)
    mod = mod_ref[...]
    g2 = mod[:, 5 * D_MODEL:6 * D_MODEL]
    w = w_ref[...]
    f = jnp.zeros((TM, D_MODEL), _F32)
    for k, y_ref in enumerate((y0_ref, y1_ref, y2_ref, y3_ref)):
        f = f + _load_row_tiles(y_ref, TM) * w[:, k:k + 1]
    h2 = h1_ref[...] + g2 * _rms(f, ng_ref[3:4, :])

    @pl.when(i < CTX_TILES)
    def _():
        oc_ref[...] = h2

    @pl.when(i >= CTX_TILES)
    def _():
        ol_ref[...] = h2


def _combine_call(h1, mod3, norm_g, yslot, topw, modrow):
    row = lambda i, a: (i, 0)
    choice = lambda k: pl.BlockSpec((TM * SUBL, LANES), lambda i, a: (k * N_TILES + i, 0))
    yslot = yslot.reshape(-1, LANES)
    grid_spec = pltpu.PrefetchScalarGridSpec(
        num_scalar_prefetch=1,
        grid=(N_TILES,),
        in_specs=[
            pl.BlockSpec((TM, D_MODEL), row),
            pl.BlockSpec((None, 1, 6 * D_MODEL), lambda i, a: (a[i], 0, 0)),
            pl.BlockSpec((4, D_MODEL), lambda i, a: (0, 0)),
            choice(0), choice(1), choice(2), choice(3),
            pl.BlockSpec((TM, LANES), row),
        ],
        out_specs=[pl.BlockSpec((TM, D_MODEL), _ctx_rows), pl.BlockSpec((TM, D_MODEL), _lat_rows)],
    )
    return pl.pallas_call(
        _combine_kernel,
        grid_spec=grid_spec,
        out_shape=[jax.ShapeDtypeStruct((T_CTX, D_MODEL), _F32),
                   jax.ShapeDtypeStruct((T_LAT, D_MODEL), _F32)],
        compiler_params=_params(1),
        name="moe_combine",
    )(modrow, h1, mod3, norm_g, yslot, yslot, yslot, yslot, topw)


def _tile_tables():
    t = np.arange(N_TILES)
    lat = t >= CTX_TILES
    j = (t - CTX_TILES) % LAT_TILES_PER_SEQ
    modrow = np.where(lat, 1 + (t - CTX_TILES) // LAT_TILES_PER_SEQ, 0)
    ropeblk = np.where(lat, j, LAT_TILES_PER_SEQ)
    stblk = np.where(lat, CTX_TILES, t)
    hasprev = np.where(lat & (j > 0), 1, 0)
    hasnext = np.where(lat & (j < LAT_TILES_PER_SEQ - 1), 1, 0)
    as_i32 = lambda a: jnp.asarray(a, jnp.int32)
    return as_i32(modrow), as_i32(ropeblk), as_i32(stblk), as_i32(hasprev), as_i32(hasnext)


def _rope_tables():
    f32 = np.float32
    half = HEAD_DIM // 4
    freqs = np.power(f32(ROPE_THETA), -np.arange(half, dtype=f32) / f32(half)).astype(f32)
    t = np.arange(DEC_SEQ)
    pos = np.stack([t // GRID_W, t % GRID_W], axis=1).astype(f32)
    ang = (pos[:, :, None] * freqs[None, None, :]).astype(f32)
    cos = np.cos(ang).astype(f32)
    sin = np.sin(ang).astype(f32)
    cos_h = np.concatenate([cos, cos], axis=-1).reshape(DEC_SEQ, HEAD_DIM)
    sin_h = np.concatenate([-sin, sin], axis=-1).reshape(DEC_SEQ, HEAD_DIM)
    cos_t = np.concatenate([np.tile(cos_h, (1, N_HEADS)), np.ones((TM, BR_WIDTH), f32)], axis=0)
    sin_t = np.concatenate([np.tile(sin_h, (1, N_HEADS)), np.zeros((TM, BR_WIDTH), f32)], axis=0)
    return jnp.asarray(cos_t), jnp.asarray(sin_t)


def _permute_heads_cols(w):
    lead = w.shape[:-1]
    return w.reshape(lead + (N_HEADS, HEAD_DIM))[..., jnp.array([0, 2, 1, 3]), :].reshape(lead + (BR_WIDTH,))


def _cache_rows(cache, l):
    return cache[:, l].reshape(DEC_BATCH, PAST_LEN, KV_WIDTH).astype(_BF16)


def _lookup(table, idx):
    hit = idx[..., None] == jnp.arange(N_EXPERTS, dtype=jnp.int32)
    return jnp.sum(jnp.where(hit, table, 0), axis=-1)


def _moe_layout(cnt, topi, rank):
    counts = cnt[0, :N_EXPERTS]
    padded = (counts + MOE_BLK - 1) // MOE_BLK * MOE_BLK
    pad_end = jnp.cumsum(padded)
    pad_start = pad_end - padded
    blk_row0 = jnp.arange(N_MOE_BLOCKS, dtype=jnp.int32) * MOE_BLK
    blk_e = jnp.minimum(jnp.sum(pad_end[None, :] <= blk_row0[:, None], axis=1),
                        N_EXPERTS - 1).astype(jnp.int32)
    nvalid = (pad_end[-1:] // MOE_BLK).astype(jnp.int32)
    spare_before = jnp.cumsum(padded - counts) - (padded - counts)
    row = jnp.arange(N_MOE_ROWS, dtype=jnp.int32).reshape(N_MOE_BLOCKS, MOE_BLK)
    in_tail = (blk_row0 >= pad_end[-1])[:, None]
    spare_in_group = row + _lookup(spare_before - pad_start - counts, blk_e)[:, None]
    spare_tail = row - pad_end[-1] + jnp.sum(padded - counts)
    spare = N_SLOTS + jnp.where(in_tail, spare_tail, spare_in_group)
    dest = _lookup(pad_start, topi[:, :TOP_K]) + rank[:, :TOP_K]
    slot_id = (jnp.arange(TOP_K, dtype=jnp.int32)[None, :] * T_ALL
               + jnp.arange(T_ALL, dtype=jnp.int32)[:, None])
    slot_of_row = spare.reshape(-1).astype(jnp.int32).at[dest.reshape(-1)].set(slot_id.reshape(-1))
    tok_of_row = jnp.where(slot_of_row < N_SLOTS, slot_of_row % T_ALL, 0)
    return blk_e, nvalid, tok_of_row, slot_of_row


def kernel(x_prompt, x_sample, cache_win_k, cache_win_v, cache_full_k, cache_full_v, c, c_ctx,
           w_mod, b_mod, norm_g, w_in, a_ln_g, a_ln_b, w_sp, b_sp, conv_w, conv_b, b_ln_g, b_ln_b,
           sink, qn_g, kn_g, w_br, w_o, w_router, b_router, w1, b1, w2, b2):
    modrow, ropeblk, stblk, hasprev, hasnext = _tile_tables()
    cos_t, sin_t = _rope_tables()
    head_id = np.arange(BR_WIDTH) // HEAD_DIM
    ones_bd = jnp.asarray(head_id[:, None] == head_id[None, :], _BF16)
    ltri = jnp.asarray(np.tril(np.ones((TM, TM)), -1), _BF16)
    row2 = lambda v: v.reshape(1, -1)

    cvec = jnp.concatenate([c_ctx[None, :], c, jnp.zeros((8 - 1 - DEC_BATCH, D_MODEL), _F32)], axis=0)
    mod_all = _mod_call(cvec, w_mod, b_mod)

    hc = x_prompt.reshape(T_CTX, D_MODEL)
    hl = x_sample.reshape(T_LAT, D_MODEL)
    states = []
    for l in range(DEPTH):
        mod3 = mod_all[l].reshape(8, 1, 6 * D_MODEL)
        wl = w_in[l]
        w_main = jnp.concatenate([
            wl[:, 0:1024],
            _permute_heads_cols(wl[:, 1024:1280]), wl[:, 1280:1536],
            _permute_heads_cols(wl[:, 1536:1792]), wl[:, 1792:2048]], axis=1).astype(_BF16)
        wg = wl[:, MAIN_WIDTH:].astype(_BF16)
        qn = jnp.tile(qn_g[l], N_HEADS).reshape(1, BR_WIDTH)
        kn = jnp.tile(kn_g[l], KV_HEADS).reshape(1, KV_WIDTH)

        ab, qc, kc, vc, qd, kd, vd, st = _inproj_call(
            hc, hl, mod3, row2(norm_g[l, 0]), w_main, qn, kn, ones_bd, cos_t, sin_t,
            (modrow, ropeblk, stblk))
        states.append(st[:T_CTX].reshape(BATCH, SEQ, 4, KV_HEADS, HEAD_DIM))

        bsp = jnp.repeat(b_sp[l].T, BR_WIDTH // A_GROUPS, axis=1)
        cw = jnp.concatenate([conv_w[l], jnp.zeros((1, BR_WIDTH), _F32)], axis=0)
        oa, ob = _mix_call(ab, row2(a_ln_g[l]), row2(a_ln_b[l]), w_sp[l].astype(_BF16), bsp, cw,
                           row2(conv_b[l]), row2(b_ln_g[l]), row2(b_ln_b[l]), (hasprev, hasnext))

        occ, odc = _ctx_attn_call(sink[l], qc, kc, vc, qd, kd, vd)
        ocl = _win_attn_call(sink[l], qc, kc, vc, _cache_rows(cache_win_k, l),
                             _cache_rows(cache_win_v, l))
        odl = _full_attn_call(qd, kd, vd, _cache_rows(cache_full_k, l),
                              _cache_rows(cache_full_v, l))

        perm = jnp.array([0, 2, 1, 3])
        wbr = w_br[l].reshape(N_BRANCH, N_HEADS, HEAD_DIM, D_MODEL)
        wbr = jnp.concatenate([wbr[0:2], wbr[2:4][:, perm]], axis=0)
        wbr = wbr.reshape(N_BRANCH, BR_WIDTH, D_MODEL).astype(_BF16)
        wr = jnp.concatenate([w_router[l], jnp.zeros((D_MODEL, LANES - N_EXPERTS), _F32)], axis=1)
        wr_hi, wr_lo = _hi_lo(wr)
        br = jnp.concatenate([b_router[l], jnp.full((LANES - N_EXPERTS,), NEG_BIG, _F32)]).reshape(1, LANES)
        h1, n2, topi, topw, rank, cnt = _merge_call(
            hc, hl, mod3, norm_g[l], oa, ob, occ, ocl, odc, odl, wg, wbr, w_o[l].astype(_BF16),
            wr_hi, wr_lo, br, ltri, modrow)

        blk_e, nvalid, tok_of_row, slot_of_row = _moe_layout(cnt, topi, rank)
        yslot = _expert_call(l, blk_e, nvalid, tok_of_row, slot_of_row,
                             n2.reshape(T_ALL, SUBL, LANES), w1, b1, w2, b2)
        hc, hl = _combine_call(h1, mod3, norm_g[l], yslot, topw, modrow)

    y_prompt = hc.reshape(BATCH, SEQ, D_MODEL)
    y_sample = hl.reshape(DEC_BATCH, DEC_SEQ, D_MODEL)
    st = jnp.stack(states, axis=1)
    return (y_prompt, y_sample, st[:, :, :, 0], st[:, :, :, 1], st[:, :, :, 2], st[:, :, :, 3])
```

```python
import functools

import numpy as np
import jax
import jax.numpy as jnp
from jax import lax
from jax.experimental import pallas as pl
from jax.experimental.pallas import tpu as pltpu

D_MODEL = 1024
BATCH = 16
SEQ = 256
DEPTH = 2
DEC_BATCH = 4
DEC_SEQ = 4096
PAST_LEN = 256
GRID_W = 64
HEAD_DIM = 64
BR_WIDTH = 256
N_BRANCH = 4
A_GROUPS = 4
CHUNK = 128
CONV_W = 31
N_HEADS = 4
KV_HEADS = 2
KV_WIDTH = KV_HEADS * HEAD_DIM
WINDOW = 128
ROPE_THETA = 10000.0
EPS = 1e-6
N_EXPERTS = 32
TOP_K = 4
D_FF = D_MODEL
SWIGLU_LIMIT = 7.0
SWIGLU_ALPHA = 1.702
MAIN_WIDTH = 2048

LANES = 128
T_CTX = BATCH * SEQ
T_LAT = DEC_BATCH * DEC_SEQ
T_ALL = T_CTX + T_LAT
TM = 256
N_TILES = T_ALL // TM
CTX_TILES = T_CTX // TM
LAT_TILES_PER_SEQ = DEC_SEQ // TM
HALO = 16
MOE_BLK = 256
N_SLOTS = T_ALL * TOP_K
RING = 3
N_MOE_BLOCKS = N_SLOTS // MOE_BLK + N_EXPERTS + 1
N_MOE_ROWS = N_MOE_BLOCKS * MOE_BLK
WIN_TQ = 512
FULL_TQ = 256
FULL_TK = 512
NEG_BIG = -1e30
VMEM_LIMIT = 56 * 1024 * 1024

_F32 = jnp.float32
_BF16 = jnp.bfloat16


def _params(n_axes, vmem=None):
    return pltpu.CompilerParams(
        dimension_semantics=("arbitrary",) * n_axes,
        vmem_limit_bytes=vmem if vmem is not None else VMEM_LIMIT)


def _split_dot(a, b_hi, b_lo):
    a_hi = a.astype(_BF16)
    a_lo = (a - a_hi.astype(_F32)).astype(_BF16)
    return (jnp.dot(a_hi, b_hi, preferred_element_type=_F32)
            + jnp.dot(a_lo, b_hi, preferred_element_type=_F32)
            + jnp.dot(a_hi, b_lo, preferred_element_type=_F32))


def _hi_lo(w):
    hi = w.astype(_BF16)
    lo = (w - hi.astype(_F32)).astype(_BF16)
    return hi, lo


def _mod_kernel(c_ref, w_ref, b_ref, o_ref):
    c = c_ref[...]
    s = c * jax.nn.sigmoid(c)
    w = w_ref[...]
    w_hi = w.astype(_BF16)
    w_lo = (w - w_hi.astype(_F32)).astype(_BF16)
    o_ref[...] = _split_dot(s, w_hi, w_lo) + b_ref[...]


def _mod_call(cvec, w_mod, b_mod):
    tn = 1024
    return pl.pallas_call(
        _mod_kernel,
        grid=(DEPTH, 6 * D_MODEL // tn),
        in_specs=[
            pl.BlockSpec((8, D_MODEL), lambda l, j: (0, 0)),
            pl.BlockSpec((None, D_MODEL, tn), lambda l, j: (l, 0, j)),
            pl.BlockSpec((None, 1, tn), lambda l, j: (l, 0, j)),
        ],
        out_specs=pl.BlockSpec((None, 8, tn), lambda l, j: (l, 0, j)),
        out_shape=jax.ShapeDtypeStruct((DEPTH, 8, 6 * D_MODEL), _F32),
        compiler_params=_params(2),
        name="mod",
    )(cvec, w_mod, b_mod.reshape(DEPTH, 1, 6 * D_MODEL))


def _rms(x, g):
    ms = jnp.mean(x * x, axis=-1, keepdims=True)
    return x * lax.rsqrt(ms + EPS) * g


def _head_rms(x, ones_bd, g):
    xx = x * x
    hi = xx.astype(_BF16)
    lo = (xx - hi.astype(_F32)).astype(_BF16)
    ss = (jnp.dot(hi, ones_bd, preferred_element_type=_F32)
          + jnp.dot(lo, ones_bd, preferred_element_type=_F32))
    return x * lax.rsqrt(ss * (1.0 / HEAD_DIM) + EPS) * g


def _rope(x, cos, sin_signed):
    w = x.shape[-1]
    lane = lax.broadcasted_iota(jnp.int32, x.shape, 1)
    first = (lane % 32) < 16
    partner = jnp.where(first, pltpu.roll(x, w - 16, 1), pltpu.roll(x, 16, 1))
    return x * cos + partner * sin_signed


def _inproj_kernel(modrow_ref, ropeblk_ref, stblk_ref,
                   xc_ref, xl_ref, mod_ref, g0_ref, w_ref, qn_ref, kn_ref, ones_ref,
                   cos_ref, sin_ref,
                   ab_ref, qc_ref, kc_ref, vc_ref, qd_ref, kd_ref, vd_ref, st_ref):
    x = jnp.where(pl.program_id(0) < CTX_TILES, xc_ref[...], xl_ref[...])
    mod = mod_ref[...]
    sh1 = mod[:, 0:D_MODEL]
    sc1 = mod[:, D_MODEL:2 * D_MODEL]
    n = _rms(x, g0_ref[...]) * (1.0 + sc1) + sh1
    z = jnp.dot(n.astype(_BF16), w_ref[...], preferred_element_type=_F32)
    ab_ref[...] = z[:, 0:1024]
    cq = z[:, 1024:1280]
    ck = z[:, 1280:1408]
    cv = z[:, 1408:1536]
    dq = z[:, 1536:1792]
    dk = z[:, 1792:1920]
    dv = z[:, 1920:2048]
    ones = ones_ref[...]
    dq = _head_rms(dq, ones, qn_ref[...])
    dk = _head_rms(dk, ones[0:KV_WIDTH, 0:KV_WIDTH], kn_ref[...])
    cos = cos_ref[...]
    sin = sin_ref[...]
    scale = HEAD_DIM ** -0.5
    qc_ref[...] = (_rope(cq, cos, sin) * scale).astype(_BF16)
    qd_ref[...] = (_rope(dq, cos, sin) * scale).astype(_BF16)
    kc = _rope(ck, cos[:, 0:KV_WIDTH], sin[:, 0:KV_WIDTH])
    kd = _rope(dk, cos[:, 0:KV_WIDTH], sin[:, 0:KV_WIDTH])
    kc_ref[...] = kc.astype(_BF16)
    kd_ref[...] = kd.astype(_BF16)
    vc_ref[...] = cv.astype(_BF16)
    vd_ref[...] = dv.astype(_BF16)
    st_ref[:, 0:128] = kc
    st_ref[:, 128:256] = cv
    st_ref[:, 256:384] = kd
    st_ref[:, 384:512] = dv


def _ctx_rows(*_):
    i = _[0]
    return (jnp.minimum(i, CTX_TILES - 1), 0)


def _lat_rows(*_):
    i = _[0]
    return (jnp.maximum(i - CTX_TILES, 0), 0)


def _inproj_call(xc, xl, mod3, g0, w_main, qn, kn, ones_bd, cos_t, sin_t, tables):
    modrow, ropeblk, stblk = tables
    row = lambda i, a, b, c: (i, 0)
    full = lambda i, a, b, c: (0, 0)
    bf = lambda w: jax.ShapeDtypeStruct((T_ALL, w), _BF16)
    grid_spec = pltpu.PrefetchScalarGridSpec(
        num_scalar_prefetch=3,
        grid=(N_TILES,),
        in_specs=[
            pl.BlockSpec((TM, D_MODEL), _ctx_rows),
            pl.BlockSpec((TM, D_MODEL), _lat_rows),
            pl.BlockSpec((None, 1, 6 * D_MODEL), lambda i, a, b, c: (a[i], 0, 0)),
            pl.BlockSpec((1, D_MODEL), full),
            pl.BlockSpec((D_MODEL, MAIN_WIDTH), full),
            pl.BlockSpec((1, BR_WIDTH), full),
            pl.BlockSpec((1, KV_WIDTH), full),
            pl.BlockSpec((BR_WIDTH, BR_WIDTH), full),
            pl.BlockSpec((TM, BR_WIDTH), lambda i, a, b, c: (b[i], 0)),
            pl.BlockSpec((TM, BR_WIDTH), lambda i, a, b, c: (b[i], 0)),
        ],
        out_specs=[
            pl.BlockSpec((TM, 1024), row),
            pl.BlockSpec((TM, BR_WIDTH), row),
            pl.BlockSpec((TM, KV_WIDTH), row),
            pl.BlockSpec((TM, KV_WIDTH), row),
            pl.BlockSpec((TM, BR_WIDTH), row),
            pl.BlockSpec((TM, KV_WIDTH), row),
            pl.BlockSpec((TM, KV_WIDTH), row),
            pl.BlockSpec((TM, 512), lambda i, a, b, c: (c[i], 0)),
        ],
    )
    return pl.pallas_call(
        _inproj_kernel,
        grid_spec=grid_spec,
        out_shape=[
            jax.ShapeDtypeStruct((T_ALL, 1024), _F32),
            bf(BR_WIDTH), bf(KV_WIDTH), bf(KV_WIDTH),
            bf(BR_WIDTH), bf(KV_WIDTH), bf(KV_WIDTH),
            jax.ShapeDtypeStruct((T_CTX + TM, 512), _F32),
        ],
        compiler_params=_params(1),
        name="inproj",
    )(modrow, ropeblk, stblk, xc, xl, mod3, g0, w_main, qn, kn, ones_bd, cos_t, sin_t)


def _layernorm(x, g, b):
    mu = jnp.mean(x, axis=-1, keepdims=True)
    xc = x - mu
    var = jnp.mean(xc * xc, axis=-1, keepdims=True)
    return xc * lax.rsqrt(var + EPS) * g + b


def _gelu(x):
    return 0.5 * x * (1.0 + lax.erf(x * (2.0 ** -0.5)))


def _mix_kernel(hasprev_ref, hasnext_ref,
                ab_ref, prev_ref, next_ref, alg_ref, alb_ref, wsp_ref, bsp_ref,
                cw_ref, cb_ref, blg_ref, blb_ref,
                oa_ref, ob_ref, buf_ref):
    i = pl.program_id(0)
    ab = ab_ref[...]
    u = _gelu(ab[:, 0:256])
    va = _layernorm(_gelu(ab[:, 256:512]), alg_ref[...], alb_ref[...])
    vab = va.astype(_BF16)
    lane = lax.broadcasted_iota(jnp.int32, (CHUNK, BR_WIDTH), 1)
    group = lane // (BR_WIDTH // A_GROUPS)
    bsp = bsp_ref[...]
    for c in range(TM // CHUNK):
        v_c = vab[c * CHUNK:(c + 1) * CHUNK, :]
        sp = bsp
        for g in range(A_GROUPS):
            full = jnp.dot(wsp_ref[g], v_c, preferred_element_type=_F32)
            sp = sp + jnp.where(group == g, full, 0.0)
        oa_ref[c * CHUNK:(c + 1) * CHUNK, :] = (u[c * CHUNK:(c + 1) * CHUNK, :] * sp).astype(_BF16)
    glu = ab[:, 512:768] * jax.nn.sigmoid(ab[:, 768:1024])
    pv = prev_ref[...]
    nx = next_ref[...]
    hp = hasprev_ref[i].astype(_F32)
    hn = hasnext_ref[i].astype(_F32)
    buf_ref[0:HALO, :] = pv[:, 0:256] * jax.nn.sigmoid(pv[:, 256:512]) * hp
    buf_ref[HALO:HALO + TM, :] = glu
    buf_ref[HALO + TM:HALO + TM + HALO, :] = nx[:, 0:256] * jax.nn.sigmoid(nx[:, 256:512]) * hn
    cw = cw_ref[...]
    acc = jnp.zeros((TM, BR_WIDTH), _F32) + cb_ref[...]
    off = HALO - CONV_W // 2
    for k in range(CONV_W):
        acc = acc + buf_ref[off + k:off + k + TM, :] * cw[k:k + 1, :]
    y = _layernorm(acc, blg_ref[...], blb_ref[...])
    ob_ref[...] = (y * jax.nn.sigmoid(y)).astype(_BF16)


def _mix_call(ab, a_ln_g, a_ln_b, wsp, bsp, conv_w, conv_b, b_ln_g, b_ln_b, tables):
    hasprev, hasnext = tables
    n_halo_blocks = T_ALL // HALO
    per = TM // HALO
    full2 = lambda i, a, b: (0, 0)
    grid_spec = pltpu.PrefetchScalarGridSpec(
        num_scalar_prefetch=2,
        grid=(N_TILES,),
        in_specs=[
            pl.BlockSpec((TM, 1024), lambda i, a, b: (i, 0)),
            pl.BlockSpec((HALO, 512), lambda i, a, b: (jnp.maximum(i * per - 1, 0), 1)),
            pl.BlockSpec((HALO, 512), lambda i, a, b: (jnp.minimum((i + 1) * per, n_halo_blocks - 1), 1)),
            pl.BlockSpec((1, BR_WIDTH), full2),
            pl.BlockSpec((1, BR_WIDTH), full2),
            pl.BlockSpec((A_GROUPS, CHUNK, CHUNK), lambda i, a, b: (0, 0, 0)),
            pl.BlockSpec((CHUNK, BR_WIDTH), full2),
            pl.BlockSpec((32, BR_WIDTH), full2),
            pl.BlockSpec((1, BR_WIDTH), full2),
            pl.BlockSpec((1, BR_WIDTH), full2),
            pl.BlockSpec((1, BR_WIDTH), full2),
        ],
        out_specs=[
            pl.BlockSpec((TM, BR_WIDTH), lambda i, a, b: (i, 0)),
            pl.BlockSpec((TM, BR_WIDTH), lambda i, a, b: (i, 0)),
        ],
        scratch_shapes=[pltpu.VMEM((TM + 2 * HALO, BR_WIDTH), _F32)],
    )
    return pl.pallas_call(
        _mix_kernel,
        grid_spec=grid_spec,
        out_shape=[jax.ShapeDtypeStruct((T_ALL, BR_WIDTH), _BF16)] * 2,
        compiler_params=_params(1),
        name="mixers_ab",
    )(hasprev, hasnext, ab, ab, ab, a_ln_g, a_ln_b, wsp, bsp, conv_w, conv_b, b_ln_g, b_ln_b)


def _lane_masks(n_rows):
    lane = lax.broadcasted_iota(jnp.int32, (n_rows, LANES), 1)
    return lane < HEAD_DIM, lane >= HEAD_DIM


def _group_queries(q, kv):
    tq = q.shape[0]
    lo, hi = _lane_masks(tq)
    keep = lo if kv == 0 else hi
    zero = jnp.zeros((), q.dtype)
    return jnp.concatenate([jnp.where(keep, q[:, 0:LANES], zero),
                            jnp.where(keep, q[:, LANES:2 * LANES], zero)], axis=0)


def _scores(qs, k):
    return lax.dot_general(qs, k, (((1,), (1,)), ((), ())), preferred_element_type=_F32)


def _attend_once(q, chunks, sinks):
    tq = q.shape[0]
    lo, hi = _lane_masks(tq)
    outs = [jnp.zeros((tq, LANES), _F32), jnp.zeros((tq, LANES), _F32)]
    for kv in range(KV_HEADS):
        qs = _group_queries(q, kv)
        ss = []
        for k, _, mask in chunks:
            s = _scores(qs, k)
            if mask is not None:
                s = jnp.where(mask, s, -jnp.inf)
            ss.append(s)
        m = ss[0].max(axis=-1, keepdims=True)
        for s in ss[1:]:
            m = jnp.maximum(m, s.max(axis=-1, keepdims=True))
        if sinks is not None:
            row = lax.broadcasted_iota(jnp.int32, (2 * tq, 1), 0)
            sink_col = jnp.where(row < tq, sinks[2 * kv], sinks[2 * kv + 1])
            m = jnp.maximum(m, sink_col)
            l = jnp.exp(sink_col - m)
        else:
            l = jnp.zeros((2 * tq, 1), _F32)
        acc = jnp.zeros((2 * tq, LANES), _F32)
        for s, (_, v, _) in zip(ss, chunks):
            p = jnp.exp(s - m)
            l = l + p.sum(axis=-1, keepdims=True)
            acc = acc + jnp.dot(p.astype(_BF16), v, preferred_element_type=_F32)
        o = acc / l
        keep = lo if kv == 0 else hi
        outs[0] = outs[0] + jnp.where(keep, o[0:tq], 0.0)
        outs[1] = outs[1] + jnp.where(keep, o[tq:2 * tq], 0.0)
    return jnp.concatenate(outs, axis=1)


def _ctx_attn_kernel(sink_ref, qc_ref, kc_ref, vc_ref, qd_ref, kd_ref, vd_ref, oc_ref, od_ref):
    sinks = [sink_ref[h] for h in range(N_HEADS)]
    oc_ref[...] = _attend_once(qc_ref[...], [(kc_ref[...], vc_ref[...], None)], sinks).astype(_BF16)
    od_ref[...] = _attend_once(qd_ref[...], [(kd_ref[...], vd_ref[...], None)], None).astype(_BF16)


def _ctx_attn_call(sink, qc, kc, vc, qd, kd, vd):
    seq = lambda b: (b, 0)
    q_spec = pl.BlockSpec((SEQ, BR_WIDTH), seq)
    kv_spec = pl.BlockSpec((SEQ, KV_WIDTH), seq)
    return pl.pallas_call(
        _ctx_attn_kernel,
        grid=(BATCH,),
        in_specs=[pl.BlockSpec(memory_space=pltpu.SMEM),
                  q_spec, kv_spec, kv_spec, q_spec, kv_spec, kv_spec],
        out_specs=[q_spec, q_spec],
        out_shape=[jax.ShapeDtypeStruct((T_CTX, BR_WIDTH), _BF16)] * 2,
        compiler_params=_params(1),
        name="ctx_attention",
    )(sink, qc, kc, vc, qd, kd, vd)


def _win_attn_kernel(sink_ref, q_ref, kcur_ref, kprev_ref, knext_ref, vcur_ref, vprev_ref,
                     vnext_ref, kctx_ref, vctx_ref, oc_ref):
    j = pl.program_id(1)
    nq = pl.num_programs(1)
    sinks = [sink_ref[h] for h in range(N_HEADS)]
    tq = WIN_TQ
    def rel(n_keys):
        qi = lax.broadcasted_iota(jnp.int32, (2 * tq, n_keys), 0) % tq
        kj = lax.broadcasted_iota(jnp.int32, (2 * tq, n_keys), 1)
        return qi, kj
    qi, kj = rel(tq)
    m_cur = jnp.abs(qi - kj) <= WINDOW
    qi, kj = rel(WINDOW)
    m_prev = ((qi - (kj - WINDOW)) <= WINDOW) & (j > 0)
    m_next = (((kj + tq) - qi) <= WINDOW) & (j < nq - 1)
    chunks = [
        (kcur_ref[...], vcur_ref[...], m_cur),
        (kprev_ref[...], vprev_ref[...], m_prev),
        (knext_ref[...], vnext_ref[...], m_next),
        (kctx_ref[...], vctx_ref[...], None),
    ]
    oc_ref[...] = _attend_once(q_ref[...], chunks, sinks).astype(_BF16)


def _win_attn_call(sink, qc, kc, vc, kctx, vctx):
    nq = DEC_SEQ // WIN_TQ
    ctx_q = T_CTX // WIN_TQ
    sub = WIN_TQ // WINDOW
    ctx_w = T_CTX // WINDOW
    per_seq_w = DEC_SEQ // WINDOW
    cur = lambda b, j: (ctx_q + b * nq + j, 0)
    prev = lambda b, j: (ctx_w + b * per_seq_w + jnp.maximum(j * sub - 1, 0), 0)
    nxt = lambda b, j: (ctx_w + b * per_seq_w + jnp.minimum((j + 1) * sub, per_seq_w - 1), 0)
    cache = lambda b, j: (b, 0, 0)
    return pl.pallas_call(
        _win_attn_kernel,
        grid=(DEC_BATCH, nq),
        in_specs=[
            pl.BlockSpec(memory_space=pltpu.SMEM),
            pl.BlockSpec((WIN_TQ, BR_WIDTH), cur),
            pl.BlockSpec((WIN_TQ, KV_WIDTH), cur),
            pl.BlockSpec((WINDOW, KV_WIDTH), prev),
            pl.BlockSpec((WINDOW, KV_WIDTH), nxt),
            pl.BlockSpec((WIN_TQ, KV_WIDTH), cur),
            pl.BlockSpec((WINDOW, KV_WIDTH), prev),
            pl.BlockSpec((WINDOW, KV_WIDTH), nxt),
            pl.BlockSpec((None, PAST_LEN, KV_WIDTH), cache),
            pl.BlockSpec((None, PAST_LEN, KV_WIDTH), cache),
        ],
        out_specs=pl.BlockSpec((WIN_TQ, BR_WIDTH), lambda b, j: (b * nq + j, 0)),
        out_shape=jax.ShapeDtypeStruct((T_LAT, BR_WIDTH), _BF16),
        compiler_params=_params(2),
        name="window_attention",
    )(sink, qc, kc, kc, kc, vc, vc, vc, kctx, vctx)


def _full_attn_kernel(q_ref, k_ref, v_ref, kctx_ref, vctx_ref, od_ref):
    tq = FULL_TQ
    q = q_ref[...]
    lo, hi = _lane_masks(tq)
    qss = [_group_queries(q, kv) for kv in range(KV_HEADS)]

    def update(carries, k, v):
        new = []
        for qs, (m, l, acc) in zip(qss, carries):
            s = _scores(qs, k)
            m_new = jnp.maximum(m, s.max(axis=-1, keepdims=True))
            a = jnp.exp(m - m_new)
            p = jnp.exp(s - m_new)
            l = a * l + p.sum(axis=-1, keepdims=True)
            acc = a * acc + jnp.dot(p.astype(_BF16), v, preferred_element_type=_F32)
            new.append((m_new, l, acc))
        return tuple(new)

    def body(c, carries):
        start = pl.multiple_of(c * FULL_TK, FULL_TK)
        return update(carries, k_ref[pl.ds(start, FULL_TK), :], v_ref[pl.ds(start, FULL_TK), :])

    init = (jnp.full((2 * tq, 1), -jnp.inf, _F32), jnp.zeros((2 * tq, 1), _F32),
            jnp.zeros((2 * tq, LANES), _F32))
    carries = lax.fori_loop(0, DEC_SEQ // FULL_TK, body, (init, init))
    carries = update(carries, kctx_ref[...], vctx_ref[...])
    outs = [jnp.zeros((tq, LANES), _F32), jnp.zeros((tq, LANES), _F32)]
    for kv, (m, l, acc) in enumerate(carries):
        o = acc / l
        keep = lo if kv == 0 else hi
        outs[0] = outs[0] + jnp.where(keep, o[0:tq], 0.0)
        outs[1] = outs[1] + jnp.where(keep, o[tq:2 * tq], 0.0)
    od_ref[...] = jnp.concatenate(outs, axis=1).astype(_BF16)


def _full_attn_call(qd, kd, vd, kctx, vctx):
    nq = DEC_SEQ // FULL_TQ
    ctx_q = T_CTX // FULL_TQ
    ctx_seq = T_CTX // DEC_SEQ
    cur = lambda b, j: (ctx_q + b * nq + j, 0)
    seq = lambda b, j: (ctx_seq + b, 0)
    cache = lambda b, j: (b, 0, 0)
    return pl.pallas_call(
        _full_attn_kernel,
        grid=(DEC_BATCH, nq),
        in_specs=[
            pl.BlockSpec((FULL_TQ, BR_WIDTH), cur),
            pl.BlockSpec((DEC_SEQ, KV_WIDTH), seq),
            pl.BlockSpec((DEC_SEQ, KV_WIDTH), seq),
            pl.BlockSpec((None, PAST_LEN, KV_WIDTH), cache),
            pl.BlockSpec((None, PAST_LEN, KV_WIDTH), cache),
        ],
        out_specs=pl.BlockSpec((FULL_TQ, BR_WIDTH), lambda b, j: (b * nq + j, 0)),
        out_shape=jax.ShapeDtypeStruct((T_LAT, BR_WIDTH), _BF16),
        compiler_params=_params(2),
        name="full_attention",
    )(qd, kd, vd, kctx, vctx)


SUBL = D_MODEL // LANES


def _store_row_tiles(ref, n, x):
    for c in range(SUBL):
        ref[pl.ds(c, n, stride=SUBL), :] = x[:, c * LANES:(c + 1) * LANES]


def _load_row_tiles(ref, n):
    return jnp.concatenate([ref[pl.ds(c, n, stride=SUBL), :] for c in range(SUBL)], axis=1)


def _merge_kernel(modrow_ref,
                  xc_ref, xl_ref, mod_ref, ng_ref, oa_ref, ob_ref, occ_ref, ocl_ref, odc_ref, odl_ref,
                  wg_ref, wbr_ref, wo_ref, wrh_ref, wrl_ref, br_ref, ltri_ref,
                  h1_ref, n2_ref, topi_ref, topw_ref, rank_ref, cnt_ref, base_ref):
    i = pl.program_id(0)

    @pl.when(i == 0)
    def _():
        base_ref[...] = jnp.zeros_like(base_ref)

    is_ctx = i < CTX_TILES
    x = jnp.where(is_ctx, xc_ref[...], xl_ref[...])
    mod = mod_ref[...]
    sh1 = mod[:, 0:D_MODEL]
    sc1 = mod[:, D_MODEL:2 * D_MODEL]
    g1 =mod[:, 2 * D_MODEL:3 * D_MODEL]
    sh2 = mod[:, 3 * D_MODEL:4 * D_MODEL]
    sc2 = mod[:, 4 * D_MODEL:5 * D_MODEL]
    ng = ng_ref[...]
    nb = (_rms(x, ng[0:1, :]) * (1.0 + sc1) + sh1).astype(_BF16)
    oc = jnp.where(is_ctx, occ_ref[...], ocl_ref[...])
    od = jnp.where(is_ctx, odc_ref[...], odl_ref[...])
    branches = (oa_ref[...], ob_ref[...], oc, od)
    mixed = jnp.zeros((TM, D_MODEL), _F32)
    for k in range(N_BRANCH):
        gate = jax.nn.sigmoid(jnp.dot(nb, wg_ref[:, k * D_MODEL:(k + 1) * D_MODEL],
                                      preferred_element_type=_F32))
        proj = jnp.dot(branches[k], wbr_ref[k], preferred_element_type=_F32)
        mixed = mixed + gate * proj
    mix = jnp.dot(mixed.astype(_BF16), wo_ref[...], preferred_element_type=_F32)
    h1 = x + g1 * _rms(mix, ng[1:2, :])
    h1_ref[...] = h1
    n2 = _rms(h1, ng[2:3, :]) * (1.0 + sc2) + sh2
    _store_row_tiles(n2_ref, TM, n2)
    logits = _split_dot(n2, wrh_ref[...], wrl_ref[...]) + br_ref[...]
    lane = lax.broadcasted_iota(jnp.int32, (TM, LANES), 1).astype(_F32)
    work = logits
    vals, idxs = [], []
    for _ in range(TOP_K):
        m = work.max(axis=-1, keepdims=True)
        idx = jnp.where(work == m, lane, float(LANES)).min(axis=-1, keepdims=True)
        vals.append(m)
        idxs.append(idx)
        work = jnp.where(lane == idx, -jnp.inf, work)
    es = [jnp.exp(v - vals[0]) for v in vals]
    denom = es[0] + es[1] + es[2] + es[3]
    ltri = ltri_ref[...]
    running = base_ref[0:1, :]
    topi = jnp.zeros((TM, LANES), _F32)
    topw = jnp.zeros((TM, LANES), _F32)
    rank = jnp.zeros((TM, LANES), _F32)
    for k in range(TOP_K):
        onehot = (lane == idxs[k]).astype(_F32)
        before = jnp.dot(ltri, onehot.astype(_BF16), preferred_element_type=_F32)
        r = (onehot * (before + running)).sum(axis=-1, keepdims=True)
        running = running + onehot.sum(axis=0, keepdims=True)
        topi = jnp.where(lane == k, idxs[k], topi)
        topw = jnp.where(lane == k, es[k] / denom, topw)
        rank = jnp.where(lane == k, r, rank)
    base_ref[0:1, :] = running
    topi_ref[...] = topi.astype(jnp.int32)
    topw_ref[...] = topw
    rank_ref[...] = rank.astype(jnp.int32)
    cnt_ref[...] = jnp.broadcast_to(running, (8, LANES)).astype(jnp.int32)


def _merge_call(xc, xl, mod3, norm_g, oa, ob, occ, ocl, odc, odl, wg, wbr, wo, wr_hi, wr_lo, br, ltri,
                modrow):
    row = lambda i, a: (i, 0)
    full = lambda i, a: (0, 0)
    act = pl.BlockSpec((TM, BR_WIDTH), row)
    act_ctx = pl.BlockSpec((TM, BR_WIDTH), lambda i, a: (jnp.minimum(i, CTX_TILES - 1), 0))
    act_lat = pl.BlockSpec((TM, BR_WIDTH), lambda i, a: (jnp.maximum(i - CTX_TILES, 0), 0))
    wide = pl.BlockSpec((TM, LANES), row)
    grid_spec = pltpu.PrefetchScalarGridSpec(
        num_scalar_prefetch=1,
        grid=(N_TILES,),
        in_specs=[
            pl.BlockSpec((TM, D_MODEL), _ctx_rows),
            pl.BlockSpec((TM, D_MODEL), _lat_rows),
            pl.BlockSpec((None, 1, 6 * D_MODEL), lambda i, a: (a[i], 0, 0)),
            pl.BlockSpec((4, D_MODEL), full),
            act, act, act_ctx, act_lat, act_ctx, act_lat,
            pl.BlockSpec((D_MODEL, N_BRANCH * D_MODEL), full),
            pl.BlockSpec((N_BRANCH, BR_WIDTH, D_MODEL), lambda i, a: (0, 0, 0)),
            pl.BlockSpec((D_MODEL, D_MODEL), full),
            pl.BlockSpec((D_MODEL, LANES), full),
            pl.BlockSpec((D_MODEL, LANES), full),
            pl.BlockSpec((1, LANES), full),
            pl.BlockSpec((TM, TM), full),
        ],
        out_specs=[
            pl.BlockSpec((TM, D_MODEL), row),
            pl.BlockSpec((TM * SUBL, LANES), row),
            wide, wide, wide,
            pl.BlockSpec((8, LANES), full),
        ],
        scratch_shapes=[pltpu.VMEM((8, LANES), _F32)],
    )
    return pl.pallas_call(
        _merge_kernel,
        grid_spec=grid_spec,
        out_shape=[
            jax.ShapeDtypeStruct((T_ALL, D_MODEL), _F32),
            jax.ShapeDtypeStruct((T_ALL * SUBL, LANES), _F32),
            jax.ShapeDtypeStruct((T_ALL, LANES), jnp.int32),
            jax.ShapeDtypeStruct((T_ALL, LANES), _F32),
            jax.ShapeDtypeStruct((T_ALL, LANES), jnp.int32),
            jax.ShapeDtypeStruct((8, LANES), jnp.int32),
        ],
        compiler_params=_params(1),
        name="merge_router",
    )(modrow, xc, xl, mod3, norm_g, oa, ob, occ, ocl, odc, odl, wg, wbr, wo, wr_hi, wr_lo, br, ltri)


def _expert_kernel(blk_e_ref, nvalid_ref,
                   tok0_ref, tok1_ref, tok2_ref, slotp_ref, slotc_ref,
                   n2_hbm, w1_ref, b1_ref, w2_ref, b2_ref,
                   y_hbm,
                   w1b_ref, w2b_ref, x0_ref, x1_ref, x2_ref, y0_ref, y1_ref, y2_ref, gsem, ssem):
    i = pl.program_id(0)
    nb = pl.num_programs(0)
    nvalid = nvalid_ref[0]
    xbufs = (x0_ref, x1_ref, x2_ref)
    ybufs = (y0_ref, y1_ref, y2_ref)

    def gather_copy(tok, r, p):
        return pltpu.make_async_copy(n2_hbm.at[tok], xbufs[p].at[pl.ds(SUBL * r, SUBL), :],
                                     gsem.at[p])

    def scatter_copy(slot, r, p):
        return pltpu.make_async_copy(ybufs[p].at[pl.ds(SUBL * r, SUBL), :], y_hbm.at[slot],
                                     ssem.at[p])

    def wait_gather(p):
        for r in range(MOE_BLK):
            gather_copy(0, r, p).wait()

    def wait_scatter(p):
        for r in range(MOE_BLK):
            scatter_copy(0, r, p).wait()

    @pl.when(i == 0)
    def _():
        ybufs[RING - 1][...] = jnp.zeros_like(ybufs[RING - 1])
        for r in range(MOE_BLK):
            gather_copy(tok0_ref[0, r], r, 0).start()
            gather_copy(tok1_ref[0, r], r, 1).start()

    prev = blk_e_ref[jnp.maximum(i - 1, 0)]
    fresh = (i == 0) | (blk_e_ref[i] != prev)

    @pl.when(fresh)
    def _():
        w1b_ref[...] = w1_ref[...].astype(_BF16)
        w2b_ref[...] = w2_ref[...].astype(_BF16)

    def valid_step(p):
        ahead = (p + 2) % RING
        behind = (p - 1) % RING
        wait_gather(p)

        @pl.when(i >= 2)
        def _():
            wait_scatter(p)

        for r in range(MOE_BLK):
            gather_copy(tok2_ref[0, r], r, ahead).start(priority=r % 2)
            scatter_copy(slotp_ref[0, r], r, behind).start(priority=(r + 1) % 2)
        x = _load_row_tiles(xbufs[p], MOE_BLK).astype(_BF16)
        hh = jnp.dot(x, w1b_ref[...], preferred_element_type=_F32) + b1_ref[...]
        x_glu = jnp.minimum(hh[:, 0:D_FF], SWIGLU_LIMIT)
        x_lin = jnp.clip(hh[:, D_FF:2 * D_FF], -SWIGLU_LIMIT, SWIGLU_LIMIT)
        act = x_glu * jax.nn.sigmoid(SWIGLU_ALPHA * x_glu) * (x_lin + 1.0)
        _store_row_tiles(ybufs[p], MOE_BLK,
                         jnp.dot(act.astype(_BF16), w2b_ref[...], preferred_element_type=_F32)
                         + b2_ref[...])

    def tail_step(p):
        behind = (p - 1) % RING
        behind2 = (p - 2) % RING

        @pl.when(i < nvalid + 2)
        def _():
            wait_gather(p)

        wait_scatter(p)
        ybufs[p][...] = jnp.zeros((MOE_BLK * SUBL, LANES), _F32)
        for r in range(MOE_BLK):
            scatter_copy(slotp_ref[0, r], r, behind).start()

        @pl.when(i == nb - 1)
        def _():
            for r in range(MOE_BLK):
                scatter_copy(slotc_ref[0, r], r, p).start()
            wait_scatter(behind2)
            wait_scatter(behind)
            wait_scatter(p)

    is_valid = i < nvalid
    for p in range(RING):
        pl.when(is_valid & (i % RING == p))(functools.partial(valid_step, p))
        pl.when((~is_valid) & (i % RING == p))(functools.partial(tail_step, p))


def _expert_call(l, blk_e, nvalid, tok_of_row, slot_of_row, n2, w1, b1, w2, b2):
    wsel = lambda i, e, n: (l, e[i], 0, 0)
    at = lambda d: (lambda i, e, n: (jnp.minimum(i + d, N_MOE_BLOCKS - 1), 0, 0))
    smem_blk = lambda imap: pl.BlockSpec((None, 1, MOE_BLK), imap, memory_space=pltpu.SMEM)
    tok3 = tok_of_row.reshape(N_MOE_BLOCKS, 1, MOE_BLK)
    virtual = N_MOE_ROWS + jnp.arange(MOE_BLK, dtype=jnp.int32)
    slot3 = jnp.concatenate([virtual, slot_of_row]).reshape(N_MOE_BLOCKS + 1, 1, MOE_BLK)
    row_buf = pltpu.VMEM((MOE_BLK * SUBL, LANES), _F32)
    grid_spec = pltpu.PrefetchScalarGridSpec(
        num_scalar_prefetch=2,
        grid=(N_MOE_BLOCKS,),
        in_specs=[
            smem_blk(at(0)), smem_blk(at(1)), smem_blk(at(2)),
            smem_blk(lambda i, e, n: (i, 0, 0)),
            smem_blk(lambda i, e, n: (i + 1, 0, 0)),
            pl.BlockSpec(memory_space=pl.ANY),
            pl.BlockSpec((None, None, D_MODEL, 2 * D_FF), wsel),
            pl.BlockSpec((None, None, 1, 2 * D_FF), wsel),
            pl.BlockSpec((None, None, D_FF, D_MODEL), wsel),
            pl.BlockSpec((None, None, 1, D_MODEL), wsel),
        ],
        out_specs=pl.BlockSpec(memory_space=pl.ANY),
        scratch_shapes=[pltpu.VMEM((D_MODEL, 2 * D_FF), _BF16),
                        pltpu.VMEM((D_FF, D_MODEL), _BF16)]
                       + [row_buf] * (2 * RING)
                       + [pltpu.SemaphoreType.DMA((RING,)), pltpu.SemaphoreType.DMA((RING,))],
    )
    return pl.pallas_call(
        _expert_kernel,
        grid_spec=grid_spec,
        out_shape=jax.ShapeDtypeStruct((N_MOE_ROWS + MOE_BLK, SUBL, LANES), _F32),
        compiler_params=_params(1),
        name="experts",
    )(blk_e, nvalid, tok3, tok3, tok3, slot3, slot3, n2, w1,
      b1.reshape(DEPTH, N_EXPERTS, 1, 2 * D_FF), w2, b2.reshape(DEPTH, N_EXPERTS, 1, D_MODEL))


def _combine_kernel(modrow_ref, h1_ref, mod_ref, ng_ref, y0_ref, y1_ref, y2_ref, y3_ref, w_ref,
                    oc_ref, ol_ref):
    i = pl.program_id(0)
    mod = mod_ref[...]
    g2 = mod[:, 5 * D_MODEL:6 * D_MODEL]
    w = w_ref[...]
    f = jnp.zeros((TM, D_MODEL), _F32)
    for k, y_ref in enumerate((y0_ref, y1_ref, y2_ref, y3_ref)):
        f = f + _load_row_tiles(y_ref, TM) * w[:, k:k + 1]
    h2 = h1_ref[...] + g2 * _rms(f, ng_ref[3:4, :])

    @pl.when(i < CTX_TILES)
    def _():
        oc_ref[...] = h2

    @pl.when(i >= CTX_TILES)
    def _():
        ol_ref[...] = h2


def _combine_call(h1, mod3, norm_g, yslot, topw, modrow):
    row = lambda i, a: (i, 0)
    choice = lambda k: pl.BlockSpec((TM * SUBL, LANES), lambda i, a: (k * N_TILES + i, 0))
    yslot = yslot.reshape(-1, LANES)
    grid_spec = pltpu.PrefetchScalarGridSpec(
        num_scalar_prefetch=1,
        grid=(N_TILES,),
        in_specs=[
            pl.BlockSpec((TM, D_MODEL), row),
            pl.BlockSpec((None, 1, 6 * D_MODEL), lambda i, a: (a[i], 0, 0)),
            pl.BlockSpec((4, D_MODEL), lambda i, a: (0, 0)),
            choice(0), choice(1), choice(2), choice(3),
            pl.BlockSpec((TM, LANES), row),
        ],
        out_specs=[pl.BlockSpec((TM, D_MODEL), _ctx_rows), pl.BlockSpec((TM, D_MODEL), _lat_rows)],
    )
    return pl.pallas_call(
        _combine_kernel,
        grid_spec=grid_spec,
        out_shape=[jax.ShapeDtypeStruct((T_CTX, D_MODEL), _F32),
                   jax.ShapeDtypeStruct((T_LAT, D_MODEL), _F32)],
        compiler_params=_params(1),
        name="moe_combine",
    )(modrow, h1, mod3, norm_g, yslot, yslot, yslot, yslot, topw)


def _tile_tables():
    t = np.arange(N_TILES)
    lat = t >= CTX_TILES
    j = (t - CTX_TILES) % LAT_TILES_PER_SEQ
    modrow = np.where(lat, 1 + (t - CTX_TILES) // LAT_TILES_PER_SEQ, 0)
    ropeblk = np.where(lat, j, LAT_TILES_PER_SEQ)
    stblk = np.where(lat, CTX_TILES, t)
    hasprev = np.where(lat & (j > 0), 1, 0)
    hasnext = np.where(lat & (j < LAT_TILES_PER_SEQ - 1), 1, 0)
    as_i32 = lambda a: jnp.asarray(a, jnp.int32)
    return as_i32(modrow), as_i32(ropeblk), as_i32(stblk), as_i32(hasprev), as_i32(hasnext)


def _rope_tables():
    f32 = np.float32
    half = HEAD_DIM // 4
    freqs = np.power(f32(ROPE_THETA), -np.arange(half, dtype=f32) / f32(half)).astype(f32)
    t = np.arange(DEC_SEQ)
    pos = np.stack([t // GRID_W, t % GRID_W], axis=1).astype(f32)
    ang = (pos[:, :, None] * freqs[None, None, :]).astype(f32)
    cos = np.cos(ang).astype(f32)
    sin = np.sin(ang).astype(f32)
    cos_h = np.concatenate([cos, cos], axis=-1).reshape(DEC_SEQ, HEAD_DIM)
    sin_h = np.concatenate([-sin, sin], axis=-1).reshape(DEC_SEQ, HEAD_DIM)
    cos_t = np.concatenate([np.tile(cos_h, (1, N_HEADS)), np.ones((TM, BR_WIDTH), f32)], axis=0)
    sin_t = np.concatenate([np.tile(sin_h, (1, N_HEADS)), np.zeros((TM, BR_WIDTH), f32)], axis=0)
    return jnp.asarray(cos_t), jnp.asarray(sin_t)


def _permute_heads_cols(w):
    lead = w.shape[:-1]
    return w.reshape(lead + (N_HEADS, HEAD_DIM))[..., jnp.array([0, 2, 1, 3]), :].reshape(lead + (BR_WIDTH,))


def _cache_rows(cache, l):
    return cache[:, l].reshape(DEC_BATCH, PAST_LEN, KV_WIDTH).astype(_BF16)


def _lookup(table, idx):
    hit = idx[..., None] == jnp.arange(N_EXPERTS, dtype=jnp.int32)
    return jnp.sum(jnp.where(hit, table, 0), axis=-1)


def _moe_layout(cnt, topi, rank):
    counts = cnt[0, :N_EXPERTS]
    padded = (counts + MOE_BLK - 1) // MOE_BLK * MOE_BLK
    pad_end = jnp.cumsum(padded)
    pad_start = pad_end - padded
    blk_row0 = jnp.arange(N_MOE_BLOCKS, dtype=jnp.int32) * MOE_BLK
    blk_e = jnp.minimum(jnp.sum(pad_end[None, :] <= blk_row0[:, None], axis=1),
                        N_EXPERTS - 1).astype(jnp.int32)
    nvalid = (pad_end[-1:] // MOE_BLK).astype(jnp.int32)
    spare_before = jnp.cumsum(padded - counts) - (padded - counts)
    row = jnp.arange(N_MOE_ROWS, dtype=jnp.int32).reshape(N_MOE_BLOCKS, MOE_BLK)
    in_tail = (blk_row0 >= pad_end[-1])[:, None]
    spare_in_group = row + _lookup(spare_before - pad_start - counts, blk_e)[:, None]
    spare_tail = row - pad_end[-1] + jnp.sum(padded - counts)
    spare = N_SLOTS + jnp.where(in_tail, spare_tail, spare_in_group)
    dest = _lookup(pad_start, topi[:, :TOP_K]) + rank[:, :TOP_K]
    slot_id = (jnp.arange(TOP_K, dtype=jnp.int32)[None, :] * T_ALL
               + jnp.arange(T_ALL, dtype=jnp.int32)[:, None])
    slot_of_row = spare.reshape(-1).astype(jnp.int32).at[dest.reshape(-1)].set(slot_id.reshape(-1))
    tok_of_row = jnp.where(slot_of_row < N_SLOTS, slot_of_row % T_ALL, 0)
    return blk_e, nvalid, tok_of_row, slot_of_row


def kernel(x_prompt, x_sample, cache_win_k, cache_win_v, cache_full_k, cache_full_v, c, c_ctx,
           w_mod, b_mod, norm_g, w_in, a_ln_g, a_ln_b, w_sp, b_sp, conv_w, conv_b, b_ln_g, b_ln_b,
           sink, qn_g, kn_g, w_br, w_o, w_router, b_router, w1, b1, w2, b2):
    modrow, ropeblk, stblk, hasprev, hasnext = _tile_tables()
    cos_t, sin_t = _rope_tables()
    head_id = np.arange(BR_WIDTH) // HEAD_DIM
    ones_bd = jnp.asarray(head_id[:, None] == head_id[None, :], _BF16)
    ltri = jnp.asarray(np.tril(np.ones((TM, TM)), -1), _BF16)
    row2 = lambda v: v.reshape(1, -1)

    cvec = jnp.concatenate([c_ctx[None, :], c, jnp.zeros((8 - 1 - DEC_BATCH, D_MODEL), _F32)], axis=0)
    mod_all = _mod_call(cvec, w_mod, b_mod)

    hc = x_prompt.reshape(T_CTX, D_MODEL)
    hl = x_sample.reshape(T_LAT, D_MODEL)
    states = []
    for l in range(DEPTH):
        mod3 = mod_all[l].reshape(8, 1, 6 * D_MODEL)
        wl = w_in[l]
        w_main = jnp.concatenate([
            wl[:, 0:1024],
            _permute_heads_cols(wl[:, 1024:1280]), wl[:, 1280:1536],
            _permute_heads_cols(wl[:, 1536:1792]), wl[:, 1792:2048]], axis=1).astype(_BF16)
        wg = wl[:, MAIN_WIDTH:].astype(_BF16)
        qn = jnp.tile(qn_g[l], N_HEADS).reshape(1, BR_WIDTH)
        kn = jnp.tile(kn_g[l], KV_HEADS).reshape(1, KV_WIDTH)

        ab, qc, kc, vc, qd, kd, vd, st = _inproj_call(
            hc, hl, mod3, row2(norm_g[l, 0]), w_main, qn, kn, ones_bd, cos_t, sin_t,
            (modrow, ropeblk, stblk))
        states.append(st[:T_CTX].reshape(BATCH, SEQ, 4, KV_HEADS, HEAD_DIM))

        bsp = jnp.repeat(b_sp[l].T, BR_WIDTH // A_GROUPS, axis=1)
        cw = jnp.concatenate([conv_w[l], jnp.zeros((1, BR_WIDTH), _F32)], axis=0)
        oa, ob = _mix_call(ab, row2(a_ln_g[l]), row2(a_ln_b[l]), w_sp[l].astype(_BF16), bsp, cw,
                           row2(conv_b[l]), row2(b_ln_g[l]), row2(b_ln_b[l]), (hasprev, hasnext))

        occ, odc = _ctx_attn_call(sink[l], qc, kc, vc, qd, kd, vd)
        ocl = _win_attn_call(sink[l], qc, kc, vc, _cache_rows(cache_win_k, l),
                             _cache_rows(cache_win_v, l))
        odl = _full_attn_call(qd, kd, vd, _cache_rows(cache_full_k, l),
                              _cache_rows(cache_full_v, l))

        perm = jnp.array([0, 2, 1, 3])
        wbr = w_br[l].reshape(N_BRANCH, N_HEADS, HEAD_DIM, D_MODEL)
        wbr = jnp.concatenate([wbr[0:2], wbr[2:4][:, perm]], axis=0)
        wbr = wbr.reshape(N_BRANCH, BR_WIDTH, D_MODEL).astype(_BF16)
        wr = jnp.concatenate([w_router[l], jnp.zeros((D_MODEL, LANES - N_EXPERTS), _F32)], axis=1)
        wr_hi, wr_lo = _hi_lo(wr)
        br = jnp.concatenate([b_router[l], jnp.full((LANES - N_EXPERTS,), NEG_BIG, _F32)]).reshape(1, LANES)
        h1, n2, topi, topw, rank, cnt = _merge_call(
            hc, hl, mod3, norm_g[l], oa, ob, occ, ocl, odc, odl, wg, wbr, w_o[l].astype(_BF16),
            wr_hi, wr_lo, br, ltri, modrow)

        blk_e, nvalid, tok_of_row, slot_of_row = _moe_layout(cnt, topi, rank)
        yslot = _expert_call(l, blk_e, nvalid, tok_of_row, slot_of_row,
                             n2.reshape(T_ALL, SUBL, LANES), w1, b1, w2, b2)
        hc, hl = _combine_call(h1, mod3, norm_g[l], yslot, topw, modrow)

    y_prompt = hc.reshape(BATCH, SEQ, D_MODEL)
    y_sample = hl.reshape(DEC_BATCH, DEC_SEQ, D_MODEL)
    st = jnp.stack(states, axis=1)
    return (y_prompt, y_sample, st[:, :, :, 0], st[:, :, :, 1], st[:, :, :, 2], st[:, :, :, 3])
```

```python
import functools

import numpy as np
import jax
import jax.numpy as jnp
from jax import lax
from jax.experimental import pallas as pl
from jax.experimental.pallas import tpu as pltpu

D_MODEL = 1024
BATCH = 16
SEQ = 256
DEPTH = 2
DEC_BATCH = 4
DEC_SEQ = 4096
PAST_LEN = 256
GRID_W = 64
HEAD_DIM = 64
BR_WIDTH = 256
N_BRANCH = 4
A_GROUPS = 4
CHUNK = 128
CONV_W = 31
N_HEADS = 4
KV_HEADS = 2
KV_WIDTH = KV_HEADS * HEAD_DIM
WINDOW = 128
ROPE_THETA = 10000.0
EPS = 1e-6
N_EXPERTS = 32
TOP_K = 4
D_FF = D_MODEL
SWIGLU_LIMIT = 7.0
SWIGLU_ALPHA = 1.702
MAIN_WIDTH = 2048

LANES = 128
T_CTX = BATCH * SEQ
T_LAT = DEC_BATCH * DEC_SEQ
T_ALL = T_CTX + T_LAT
TM = 256
N_TILES = T_ALL // TM
CTX_TILES = T_CTX // TM
LAT_TILES_PER_SEQ = DEC_SEQ // TM
HALO = 16
MOE_BLK = 256
N_SLOTS = T_ALL * TOP_K
RING = 3
N_MOE_BLOCKS = N_SLOTS // MOE_BLK + N_EXPERTS + 1
N_MOE_ROWS = N_MOE_BLOCKS * MOE_BLK
WIN_TQ = 512
FULL_TQ = 512
FULL_TK = 512
NEG_BIG = -1e30
VMEM_LIMIT = 56 * 1024 * 1024

_F32 = jnp.float32
_BF16 = jnp.bfloat16


def _params(n_axes, vmem=None):
    return pltpu.CompilerParams(
        dimension_semantics=("arbitrary",) * n_axes,
        vmem_limit_bytes=vmem if vmem is not None else VMEM_LIMIT)


def _split_dot(a, b_hi, b_lo):
    a_hi = a.astype(_BF16)
    a_lo = (a - a_hi.astype(_F32)).astype(_BF16)
    return (jnp.dot(a_hi, b_hi, preferred_element_type=_F32)
            + jnp.dot(a_lo, b_hi, preferred_element_type=_F32)
            + jnp.dot(a_hi, b_lo, preferred_element_type=_F32))


def _hi_lo(w):
    hi = w.astype(_BF16)
    lo = (w - hi.astype(_F32)).astype(_BF16)
    return hi, lo


def _mod_kernel(c_ref, w_ref, b_ref, o_ref):
    c = c_ref[...]
    s = c * jax.nn.sigmoid(c)
    w = w_ref[...]
    w_hi = w.astype(_BF16)
    w_lo = (w - w_hi.astype(_F32)).astype(_BF16)
    o_ref[...] = _split_dot(s, w_hi, w_lo) + b_ref[...]


def _mod_call(cvec, w_mod, b_mod):
    tn = 1024
    return pl.pallas_call(
        _mod_kernel,
        grid=(DEPTH, 6 * D_MODEL // tn),
        in_specs=[
            pl.BlockSpec((8, D_MODEL), lambda l, j: (0, 0)),
            pl.BlockSpec((None, D_MODEL, tn), lambda l, j: (l, 0, j)),
            pl.BlockSpec((None, 1, tn), lambda l, j: (l, 0, j)),
        ],
        out_specs=pl.BlockSpec((None, 8, tn), lambda l, j: (l, 0, j)),
        out_shape=jax.ShapeDtypeStruct((DEPTH, 8, 6 * D_MODEL), _F32),
        compiler_params=_params(2),
        name="mod",
    )(cvec, w_mod, b_mod.reshape(DEPTH, 1, 6 * D_MODEL))


def _rms(x, g):
    ms = jnp.mean(x * x, axis=-1, keepdims=True)
    return x * lax.rsqrt(ms + EPS) * g


def _head_rms(x, ones_bd, g):
    xx = x * x
    hi = xx.astype(_BF16)
    lo = (xx - hi.astype(_F32)).astype(_BF16)
    ss = (jnp.dot(hi, ones_bd, preferred_element_type=_F32)
          + jnp.dot(lo, ones_bd, preferred_element_type=_F32))
    return x * lax.rsqrt(ss * (1.0 / HEAD_DIM) + EPS) * g


def _rope(x, cos, sin_signed):
    w = x.shape[-1]
    lane = lax.broadcasted_iota(jnp.int32, x.shape, 1)
    first = (lane % 32) < 16
    partner = jnp.where(first, pltpu.roll(x, w - 16, 1), pltpu.roll(x, 16, 1))
    return x * cos + partner * sin_signed


def _inproj_kernel(modrow_ref, ropeblk_ref, stblk_ref,
                   xc_ref, xl_ref, mod_ref, g0_ref, w_ref, qn_ref, kn_ref, ones_ref,
                   cos_ref, sin_ref,
                   ab_ref, qc_ref, kc_ref, vc_ref, qd_ref, kd_ref, vd_ref, st_ref):
    x = jnp.where(pl.program_id(0) < CTX_TILES, xc_ref[...], xl_ref[...])
    mod = mod_ref[...]
    sh1 = mod[:, 0:D_MODEL]
    sc1 = mod[:, D_MODEL:2 * D_MODEL]
    n = _rms(x, g0_ref[...]) * (1.0 + sc1) + sh1
    z = jnp.dot(n.astype(_BF16), w_ref[...], preferred_element_type=_F32)
    ab_ref[...] = z[:, 0:1024]
    cq = z[:, 1024:1280]
    ck = z[:, 1280:1408]
    cv = z[:, 1408:1536]
    dq = z[:, 1536:1792]
    dk = z[:, 1792:1920]
    dv = z[:, 1920:2048]
    ones = ones_ref[...]
    dq = _head_rms(dq, ones, qn_ref[...])
    dk = _head_rms(dk, ones[0:KV_WIDTH, 0:KV_WIDTH], kn_ref[...])
    cos = cos_ref[...]
    sin = sin_ref[...]
    scale = HEAD_DIM ** -0.5
    qc_ref[...] = (_rope(cq, cos, sin) * scale).astype(_BF16)
    qd_ref[...] = (_rope(dq, cos, sin) * scale).astype(_BF16)
    kc = _rope(ck, cos[:, 0:KV_WIDTH], sin[:, 0:KV_WIDTH])
    kd = _rope(dk, cos[:, 0:KV_WIDTH], sin[:, 0:KV_WIDTH])
    kc_ref[...] = kc.astype(_BF16)
    kd_ref[...] = kd.astype(_BF16)
    vc_ref[...] = cv.astype(_BF16)
    vd_ref[...] = dv.astype(_BF16)
    st_ref[:, 0:128] = kc
    st_ref[:, 128:256] = cv
    st_ref[:, 256:384] = kd
    st_ref[:, 384:512] = dv


def _ctx_rows(*_):
    i = _[0]
    return (jnp.minimum(i, CTX_TILES - 1), 0)


def _lat_rows(*_):
    i = _[0]
    return (jnp.maximum(i - CTX_TILES, 0), 0)


def _inproj_call(xc, xl, mod3, g0, w_main, qn, kn, ones_bd, cos_t, sin_t, tables):
    modrow, ropeblk, stblk = tables
    row = lambda i, a, b, c: (i, 0)
    full = lambda i, a, b, c: (0, 0)
    bf = lambda w: jax.ShapeDtypeStruct((T_ALL, w), _BF16)
    grid_spec = pltpu.PrefetchScalarGridSpec(
        num_scalar_prefetch=3,
        grid=(N_TILES,),
        in_specs=[
            pl.BlockSpec((TM, D_MODEL), _ctx_rows),
            pl.BlockSpec((TM, D_MODEL), _lat_rows),
            pl.BlockSpec((None, 1, 6 * D_MODEL), lambda i, a, b, c: (a[i], 0, 0)),
            pl.BlockSpec((1, D_MODEL), full),
            pl.BlockSpec((D_MODEL, MAIN_WIDTH), full),
            pl.BlockSpec((1, BR_WIDTH), full),
            pl.BlockSpec((1, KV_WIDTH), full),
            pl.BlockSpec((BR_WIDTH, BR_WIDTH), full),
            pl.BlockSpec((TM, BR_WIDTH), lambda i, a, b, c: (b[i], 0)),
            pl.BlockSpec((TM, BR_WIDTH), lambda i, a, b, c: (b[i], 0)),
        ],
        out_specs=[
            pl.BlockSpec((TM, 1024), row),
            pl.BlockSpec((TM, BR_WIDTH), row),
            pl.BlockSpec((TM, KV_WIDTH), row),
            pl.BlockSpec((TM, KV_WIDTH), row),
            pl.BlockSpec((TM, BR_WIDTH), row),
            pl.BlockSpec((TM, KV_WIDTH), row),
            pl.BlockSpec((TM, KV_WIDTH), row),
            pl.BlockSpec((TM, 512), lambda i, a, b, c: (c[i], 0)),
        ],
    )
    return pl.pallas_call(
        _inproj_kernel,
        grid_spec=grid_spec,
        out_shape=[
            jax.ShapeDtypeStruct((T_ALL, 1024), _F32),
            bf(BR_WIDTH), bf(KV_WIDTH), bf(KV_WIDTH),
            bf(BR_WIDTH), bf(KV_WIDTH), bf(KV_WIDTH),
            jax.ShapeDtypeStruct((T_CTX + TM, 512), _F32),
        ],
        compiler_params=_params(1),
        name="inproj",
    )(modrow, ropeblk, stblk, xc, xl, mod3, g0, w_main, qn, kn, ones_bd, cos_t, sin_t)


def _layernorm(x, g, b):
    mu = jnp.mean(x, axis=-1, keepdims=True)
    xc = x - mu
    var = jnp.mean(xc * xc, axis=-1, keepdims=True)
    return xc * lax.rsqrt(var + EPS) * g + b


def _gelu(x):
    return 0.5 * x * (1.0 + lax.erf(x * (2.0 ** -0.5)))


def _mix_kernel(hasprev_ref, hasnext_ref,
                ab_ref, prev_ref, next_ref, alg_ref, alb_ref, wsp_ref, bsp_ref,
                cw_ref, cb_ref, blg_ref, blb_ref,
                oa_ref, ob_ref, buf_ref):
    i = pl.program_id(0)
    ab = ab_ref[...]
    u = _gelu(ab[:, 0:256])
    va = _layernorm(_gelu(ab[:, 256:512]), alg_ref[...], alb_ref[...])
    vab = va.astype(_BF16)
    lane = lax.broadcasted_iota(jnp.int32, (CHUNK, BR_WIDTH), 1)
    group = lane // (BR_WIDTH // A_GROUPS)
    bsp = bsp_ref[...]
    for c in range(TM // CHUNK):
        v_c = vab[c * CHUNK:(c + 1) * CHUNK, :]
        sp = bsp
        for g in range(A_GROUPS):
            full = jnp.dot(wsp_ref[g], v_c, preferred_element_type=_F32)
            sp = sp + jnp.where(group == g, full, 0.0)
        oa_ref[c * CHUNK:(c + 1) * CHUNK, :] = (u[c * CHUNK:(c + 1) * CHUNK, :] * sp).astype(_BF16)
    glu = ab[:, 512:768] * jax.nn.sigmoid(ab[:, 768:1024])
    pv = prev_ref[...]
    nx = next_ref[...]
    hp = hasprev_ref[i].astype(_F32)
    hn = hasnext_ref[i].astype(_F32)
    buf_ref[0:HALO, :] = pv[:, 0:256] * jax.nn.sigmoid(pv[:, 256:512]) * hp
    buf_ref[HALO:HALO + TM, :] = glu
    buf_ref[HALO + TM:HALO + TM + HALO, :] = nx[:, 0:256] * jax.nn.sigmoid(nx[:, 256:512]) * hn
    cw = cw_ref[...]
    acc = jnp.zeros((TM, BR_WIDTH), _F32) + cb_ref[...]
    off = HALO - CONV_W // 2
    for k in range(CONV_W):
        acc = acc + buf_ref[off + k:off + k + TM, :] * cw[k:k + 1, :]
    y = _layernorm(acc, blg_ref[...], blb_ref[...])
    ob_ref[...] = (y * jax.nn.sigmoid(y)).astype(_BF16)


def _mix_call(ab, a_ln_g, a_ln_b, wsp, bsp, conv_w, conv_b, b_ln_g, b_ln_b, tables):
    hasprev, hasnext = tables
    n_halo_blocks = T_ALL // HALO
    per = TM // HALO
    full2 = lambda i, a, b: (0, 0)
    grid_spec = pltpu.PrefetchScalarGridSpec(
        num_scalar_prefetch=2,
        grid=(N_TILES,),
        in_specs=[
            pl.BlockSpec((TM, 1024), lambda i, a, b: (i, 0)),
            pl.BlockSpec((HALO, 512), lambda i, a, b: (jnp.maximum(i * per - 1, 0), 1)),
            pl.BlockSpec((HALO, 512), lambda i, a, b: (jnp.minimum((i + 1) * per, n_halo_blocks - 1), 1)),
            pl.BlockSpec((1, BR_WIDTH), full2),
            pl.BlockSpec((1, BR_WIDTH), full2),
            pl.BlockSpec((A_GROUPS, CHUNK, CHUNK), lambda i, a, b: (0, 0, 0)),
            pl.BlockSpec((CHUNK, BR_WIDTH), full2),
            pl.BlockSpec((32, BR_WIDTH), full2),
            pl.BlockSpec((1, BR_WIDTH), full2),
            pl.BlockSpec((1, BR_WIDTH), full2),
            pl.BlockSpec((1, BR_WIDTH), full2),
        ],
        out_specs=[
            pl.BlockSpec((TM, BR_WIDTH), lambda i, a, b: (i, 0)),
            pl.BlockSpec((TM, BR_WIDTH), lambda i, a, b: (i, 0)),
        ],
        scratch_shapes=[pltpu.VMEM((TM + 2 * HALO, BR_WIDTH), _F32)],
    )
    return pl.pallas_call(
        _mix_kernel,
        grid_spec=grid_spec,
        out_shape=[jax.ShapeDtypeStruct((T_ALL, BR_WIDTH), _BF16)] * 2,
        compiler_params=_params(1),
        name="mixers_ab",
    )(hasprev, hasnext, ab, ab, ab, a_ln_g, a_ln_b, wsp, bsp, conv_w, conv_b, b_ln_g, b_ln_b)


def _lane_masks(n_rows):
    lane = lax.broadcasted_iota(jnp.int32, (n_rows, LANES), 1)
    return lane < HEAD_DIM, lane >= HEAD_DIM


def _group_queries(q, kv):
    tq = q.shape[0]
    lo, hi = _lane_masks(tq)
    keep = lo if kv == 0 else hi
    zero = jnp.zeros((), q.dtype)
    return jnp.concatenate([jnp.where(keep, q[:, 0:LANES], zero),
                            jnp.where(keep, q[:, LANES:2 * LANES], zero)], axis=0)


def _scores(qs, k):
    return lax.dot_general(qs, k, (((1,), (1,)), ((), ())), preferred_element_type=_F32)


def _attend_once(q, chunks, sinks):
    tq = q.shape[0]
    lo, hi = _lane_masks(tq)
    outs = [jnp.zeros((tq, LANES), _F32), jnp.zeros((tq, LANES), _F32)]
    for kv in range(KV_HEADS):
        qs = _group_queries(q, kv)
        ss = []
        for k, _, mask in chunks:
            s = _scores(qs, k)
            if mask is not None:
                s = jnp.where(mask, s, -jnp.inf)
            ss.append(s)
        m = ss[0].max(axis=-1, keepdims=True)
        for s in ss[1:]:
            m = jnp.maximum(m, s.max(axis=-1, keepdims=True))
        if sinks is not None:
            row = lax.broadcasted_iota(jnp.int32, (2 * tq, 1), 0)
            sink_col = jnp.where(row < tq, sinks[2 * kv], sinks[2 * kv + 1])
            m = jnp.maximum(m, sink_col)
            l = jnp.exp(sink_col - m)
        else:
            l = jnp.zeros((2 * tq, 1), _F32)
        acc = jnp.zeros((2 * tq, LANES), _F32)
        for s, (_, v, _) in zip(ss, chunks):
            p = jnp.exp(s - m)
            l = l + p.sum(axis=-1, keepdims=True)
            acc = acc + jnp.dot(p.astype(_BF16), v, preferred_element_type=_F32)
        o = acc / l
        keep = lo if kv == 0 else hi
        outs[0] = outs[0] + jnp.where(keep, o[0:tq], 0.0)
        outs[1] = outs[1] + jnp.where(keep, o[tq:2 * tq], 0.0)
    return jnp.concatenate(outs, axis=1)


def _ctx_attn_kernel(sink_ref, qc_ref, kc_ref, vc_ref, qd_ref, kd_ref, vd_ref, oc_ref, od_ref):
    sinks = [sink_ref[h] for h in range(N_HEADS)]
    oc_ref[...] = _attend_once(qc_ref[...], [(kc_ref[...], vc_ref[...], None)], sinks).astype(_BF16)
    od_ref[...] = _attend_once(qd_ref[...], [(kd_ref[...], vd_ref[...], None)], None).astype(_BF16)


def _ctx_attn_call(sink, qc, kc, vc, qd, kd, vd):
    seq = lambda b: (b, 0)
    q_spec = pl.BlockSpec((SEQ, BR_WIDTH), seq)
    kv_spec = pl.BlockSpec((SEQ, KV_WIDTH), seq)
    return pl.pallas_call(
        _ctx_attn_kernel,
        grid=(BATCH,),
        in_specs=[pl.BlockSpec(memory_space=pltpu.SMEM),
                  q_spec, kv_spec, kv_spec, q_spec, kv_spec, kv_spec],
        out_specs=[q_spec, q_spec],
        out_shape=[jax.ShapeDtypeStruct((T_CTX, BR_WIDTH), _BF16)] * 2,
        compiler_params=_params(1),
        name="ctx_attention",
    )(sink, qc, kc, vc, qd, kd, vd)


def _win_attn_kernel(sink_ref, q_ref, kcur_ref, kprev_ref, knext_ref, vcur_ref, vprev_ref,
                     vnext_ref, kctx_ref, vctx_ref, oc_ref):
    j = pl.program_id(1)
    nq = pl.num_programs(1)
    sinks = [sink_ref[h] for h in range(N_HEADS)]
    tq = WIN_TQ
    def rel(n_keys):
        qi = lax.broadcasted_iota(jnp.int32, (2 * tq, n_keys), 0) % tq
        kj = lax.broadcasted_iota(jnp.int32, (2 * tq, n_keys), 1)
        return qi, kj
    qi, kj = rel(tq)
    m_cur = jnp.abs(qi - kj) <= WINDOW
    qi, kj = rel(WINDOW)
    m_prev = ((qi - (kj - WINDOW)) <= WINDOW) & (j > 0)
    m_next = (((kj + tq) - qi) <= WINDOW) & (j < nq - 1)
    chunks = [
        (kcur_ref[...], vcur_ref[...], m_cur),
        (kprev_ref[...], vprev_ref[...], m_prev),
        (knext_ref[...], vnext_ref[...], m_next),
        (kctx_ref[...], vctx_ref[...], None),
    ]
    oc_ref[...] = _attend_once(q_ref[...], chunks, sinks).astype(_BF16)


def _win_attn_call(sink, qc, kc, vc, kctx, vctx):
    nq = DEC_SEQ // WIN_TQ
    ctx_q = T_CTX // WIN_TQ
    sub = WIN_TQ // WINDOW
    ctx_w = T_CTX // WINDOW
    per_seq_w = DEC_SEQ // WINDOW
    cur = lambda b, j: (ctx_q + b * nq + j, 0)
    prev = lambda b, j: (ctx_w + b * per_seq_w + jnp.maximum(j * sub - 1, 0), 0)
    nxt = lambda b, j: (ctx_w + b * per_seq_w + jnp.minimum((j + 1) * sub, per_seq_w - 1), 0)
    cache = lambda b, j: (b, 0, 0)
    return pl.pallas_call(
        _win_attn_kernel,
        grid=(DEC_BATCH, nq),
        in_specs=[
            pl.BlockSpec(memory_space=pltpu.SMEM),
            pl.BlockSpec((WIN_TQ, BR_WIDTH), cur),
            pl.BlockSpec((WIN_TQ, KV_WIDTH), cur),
            pl.BlockSpec((WINDOW, KV_WIDTH), prev),
            pl.BlockSpec((WINDOW, KV_WIDTH), nxt),
            pl.BlockSpec((WIN_TQ, KV_WIDTH), cur),
            pl.BlockSpec((WINDOW, KV_WIDTH), prev),
            pl.BlockSpec((WINDOW, KV_WIDTH), nxt),
            pl.BlockSpec((None, PAST_LEN, KV_WIDTH), cache),
            pl.BlockSpec((None, PAST_LEN, KV_WIDTH), cache),
        ],
        out_specs=pl.BlockSpec((WIN_TQ, BR_WIDTH), lambda b, j: (b * nq + j, 0)),
        out_shape=jax.ShapeDtypeStruct((T_LAT, BR_WIDTH), _BF16),
        compiler_params=_params(2),
        name="window_attention",
    )(sink, qc, kc, kc, kc, vc, vc, vc, kctx, vctx)


def _full_attn_kernel(q_ref, k_ref, v_ref, kctx_ref, vctx_ref, od_ref):
    tq = FULL_TQ
    q = q_ref[...]
    lo, hi = _lane_masks(tq)
    qss = [_group_queries(q, kv) for kv in range(KV_HEADS)]

    def update(carries, k, v):
        new = []
        for qs, (m, l, acc) in zip(qss, carries):
            s = _scores(qs, k)
            m_new = jnp.maximum(m, s.max(axis=-1, keepdims=True))
            a = jnp.exp(m - m_new)
            p = jnp.exp(s - m_new)
            l = a * l + p.sum(axis=-1, keepdims=True)
            acc = a * acc + jnp.dot(p.astype(_BF16), v, preferred_element_type=_F32)
            new.append((m_new, l, acc))
        return tuple(new)

    def body(c, carries):
        start = pl.multiple_of(c * FULL_TK, FULL_TK)
        return update(carries, k_ref[pl.ds(start, FULL_TK), :], v_ref[pl.ds(start, FULL_TK), :])

    init = (jnp.full((2 * tq, 1), -jnp.inf, _F32), jnp.zeros((2 * tq, 1), _F32),
            jnp.zeros((2 * tq, LANES), _F32))
    carries = lax.fori_loop(0, DEC_SEQ // FULL_TK, body, (init, init))
    carries = update(carries, kctx_ref[...], vctx_ref[...])
    outs = [jnp.zeros((tq, LANES), _F32), jnp.zeros((tq, LANES), _F32)]
    for kv, (m, l, acc) in enumerate(carries):
        o = acc / l
        keep = lo if kv == 0 else hi
        outs[0] = outs[0] + jnp.where(keep, o[0:tq], 0.0)
        outs[1] = outs[1] + jnp.where(keep, o[tq:2 * tq], 0.0)
    od_ref[...] = jnp.concatenate(outs, axis=1).astype(_BF16)


def _full_attn_call(qd, kd, vd, kctx, vctx):
    nq = DEC_SEQ // FULL_TQ
    ctx_q = T_CTX // FULL_TQ
    ctx_seq = T_CTX // DEC_SEQ
    cur = lambda b, j: (ctx_q + b * nq + j, 0)
    seq = lambda b, j: (ctx_seq + b, 0)
    cache = lambda b, j: (b, 0, 0)
    return pl.pallas_call(
        _full_attn_kernel,
        grid=(DEC_BATCH, nq),
        in_specs=[
            pl.BlockSpec((FULL_TQ, BR_WIDTH), cur),
            pl.BlockSpec((DEC_SEQ, KV_WIDTH), seq),
            pl.BlockSpec((DEC_SEQ, KV_WIDTH), seq),
            pl.BlockSpec((None, PAST_LEN, KV_WIDTH), cache),
            pl.BlockSpec((None, PAST_LEN, KV_WIDTH), cache),
        ],
        out_specs=pl.BlockSpec((FULL_TQ, BR_WIDTH), lambda b, j: (b * nq + j, 0)),
        out_shape=jax.ShapeDtypeStruct((T_LAT, BR_WIDTH), _BF16),
        compiler_params=_params(2),
        name="full_attention",
    )(qd, kd, vd, kctx, vctx)


SUBL = D_MODEL // LANES


def _store_row_tiles(ref, n, x):
    for c in range(SUBL):
        ref[pl.ds(c, n, stride=SUBL), :] = x[:, c * LANES:(c + 1) * LANES]


def _load_row_tiles(ref, n):
    return jnp.concatenate([ref[pl.ds(c, n, stride=SUBL), :] for c in range(SUBL)], axis=1)


def _merge_kernel(modrow_ref,
                  xc_ref, xl_ref, mod_ref, ng_ref, oa_ref, ob_ref, occ_ref, ocl_ref, odc_ref, odl_ref,
                  wg_ref, wbr_ref, wo_ref, wrh_ref, wrl_ref, br_ref, ltri_ref,
                  h1_ref, n2_ref, topi_ref, topw_ref, rank_ref, cnt_ref, base_ref):
    i = pl.program_id(0)

    @pl.when(i == 0)
    def _():
        base_ref[...] = jnp.zeros_like(base_ref)

    is_ctx = i < CTX_TILES
    x = jnp.where(is_ctx, xc_ref[...], xl_ref[...])
    mod = mod_ref[...]
    sh1 = mod[:, 0:D_MODEL]
    sc1 = mod[:, D_MODEL:2 * D_MODEL]
    g1 =mod[:, 2 * D_MODEL:3 * D_MODEL]
    sh2 = mod[:, 3 * D_MODEL:4 * D_MODEL]
    sc2 = mod[:, 4 * D_MODEL:5 * D_MODEL]
    ng = ng_ref[...]
    nb = (_rms(x, ng[0:1, :]) * (1.0 + sc1) + sh1).astype(_BF16)
    oc = jnp.where(is_ctx, occ_ref[...], ocl_ref[...])
    od = jnp.where(is_ctx, odc_ref[...], odl_ref[...])
    branches = (oa_ref[...], ob_ref[...], oc, od)
    mixed = jnp.zeros((TM, D_MODEL), _F32)
    for k in range(N_BRANCH):
        gate = jax.nn.sigmoid(jnp.dot(nb, wg_ref[:, k * D_MODEL:(k + 1) * D_MODEL],
                                      preferred_element_type=_F32))
        proj = jnp.dot(branches[k], wbr_ref[k], preferred_element_type=_F32)
        mixed = mixed + gate * proj
    mix = jnp.dot(mixed.astype(_BF16), wo_ref[...], preferred_element_type=_F32)
    h1 = x + g1 * _rms(mix, ng[1:2, :])
    h1_ref[...] = h1
    n2 = _rms(h1, ng[2:3, :]) * (1.0 + sc2) + sh2
    _store_row_tiles(n2_ref, TM, n2)
    logits = _split_dot(n2, wrh_ref[...], wrl_ref[...]) + br_ref[...]
    lane = lax.broadcasted_iota(jnp.int32, (TM, LANES), 1).astype(_F32)
    work = logits
    vals, idxs = [], []
    for _ in range(TOP_K):
        m = work.max(axis=-1, keepdims=True)
        idx = jnp.where(work == m, lane, float(LANES)).min(axis=-1, keepdims=True)
        vals.append(m)
        idxs.append(idx)
        work = jnp.where(lane == idx, -jnp.inf, work)
    es = [jnp.exp(v - vals[0]) for v in vals]
    denom = es[0] + es[1] + es[2] + es[3]
    ltri = ltri_ref[...]
    running = base_ref[0:1, :]
    topi = jnp.zeros((TM, LANES), _F32)
    topw = jnp.zeros((TM, LANES), _F32)
    rank = jnp.zeros((TM, LANES), _F32)
    for k in range(TOP_K):
        onehot = (lane == idxs[k]).astype(_F32)
        before = jnp.dot(ltri, onehot.astype(_BF16), preferred_element_type=_F32)
        r = (onehot * (before + running)).sum(axis=-1, keepdims=True)
        running = running + onehot.sum(axis=0, keepdims=True)
        topi = jnp.where(lane == k, idxs[k], topi)
        topw = jnp.where(lane == k, es[k] / denom, topw)
        rank = jnp.where(lane == k, r, rank)
    base_ref[0:1, :] = running
    topi_ref[...] = topi.astype(jnp.int32)
    topw_ref[...] = topw
    rank_ref[...] = rank.astype(jnp.int32)
    cnt_ref[...] = jnp.broadcast_to(running, (8, LANES)).astype(jnp.int32)


def _merge_call(xc, xl, mod3, norm_g, oa, ob, occ, ocl, odc, odl, wg, wbr, wo, wr_hi, wr_lo, br, ltri,
                modrow):
    row = lambda i, a: (i, 0)
    full = lambda i, a: (0, 0)
    act = pl.BlockSpec((TM, BR_WIDTH), row)
    act_ctx = pl.BlockSpec((TM, BR_WIDTH), lambda i, a: (jnp.minimum(i, CTX_TILES - 1), 0))
    act_lat = pl.BlockSpec((TM, BR_WIDTH), lambda i, a: (jnp.maximum(i - CTX_TILES, 0), 0))
    wide = pl.BlockSpec((TM, LANES), row)
    grid_spec = pltpu.PrefetchScalarGridSpec(
        num_scalar_prefetch=1,
        grid=(N_TILES,),
        in_specs=[
            pl.BlockSpec((TM, D_MODEL), _ctx_rows),
            pl.BlockSpec((TM, D_MODEL), _lat_rows),
            pl.BlockSpec((None, 1, 6 * D_MODEL), lambda i, a: (a[i], 0, 0)),
            pl.BlockSpec((4, D_MODEL), full),
            act, act, act_ctx, act_lat, act_ctx, act_lat,
            pl.BlockSpec((D_MODEL, N_BRANCH * D_MODEL), full),
            pl.BlockSpec((N_BRANCH, BR_WIDTH, D_MODEL), lambda i, a: (0, 0, 0)),
            pl.BlockSpec((D_MODEL, D_MODEL), full),
            pl.BlockSpec((D_MODEL, LANES), full),
            pl.BlockSpec((D_MODEL, LANES), full),
            pl.BlockSpec((1, LANES), full),
            pl.BlockSpec((TM, TM), full),
        ],
        out_specs=[
            pl.BlockSpec((TM, D_MODEL), row),
            pl.BlockSpec((TM * SUBL, LANES), row),
            wide, wide, wide,
            pl.BlockSpec((8, LANES), full),
        ],
        scratch_shapes=[pltpu.VMEM((8, LANES), _F32)],
    )
    return pl.pallas_call(
        _merge_kernel,
        grid_spec=grid_spec,
        out_shape=[
            jax.ShapeDtypeStruct((T_ALL, D_MODEL), _F32),
            jax.ShapeDtypeStruct((T_ALL * SUBL, LANES), _F32),
            jax.ShapeDtypeStruct((T_ALL, LANES), jnp.int32),
            jax.ShapeDtypeStruct((T_ALL, LANES), _F32),
            jax.ShapeDtypeStruct((T_ALL, LANES), jnp.int32),
            jax.ShapeDtypeStruct((8, LANES), jnp.int32),
        ],
        compiler_params=_params(1),
        name="merge_router",
    )(modrow, xc, xl, mod3, norm_g, oa, ob, occ, ocl, odc, odl, wg, wbr, wo, wr_hi, wr_lo, br, ltri)


def _expert_kernel(blk_e_ref, nvalid_ref,
                   tok0_ref, tok1_ref, tok2_ref, slotp_ref, slotc_ref,
                   n2_hbm, w1_ref, b1_ref, w2_ref, b2_ref,
                   y_hbm,
                   w1b_ref, w2b_ref, x0_ref, x1_ref, x2_ref, y0_ref, y1_ref, y2_ref, gsem, ssem):
    i = pl.program_id(0)
    nb = pl.num_programs(0)
    nvalid = nvalid_ref[0]
    xbufs = (x0_ref, x1_ref, x2_ref)
    ybufs = (y0_ref, y1_ref, y2_ref)

    def gather_copy(tok, r, p):
        return pltpu.make_async_copy(n2_hbm.at[tok], xbufs[p].at[pl.ds(SUBL * r, SUBL), :],
                                     gsem.at[p])

    def scatter_copy(slot, r, p):
        return pltpu.make_async_copy(ybufs[p].at[pl.ds(SUBL * r, SUBL), :], y_hbm.at[slot],
                                     ssem.at[p])

    def wait_gather(p):
        for r in range(MOE_BLK):
            gather_copy(0, r, p).wait()

    def wait_scatter(p):
        for r in range(MOE_BLK):
            scatter_copy(0, r, p).wait()

    @pl.when(i == 0)
    def _():
        ybufs[RING - 1][...] = jnp.zeros_like(ybufs[RING - 1])
        for r in range(MOE_BLK):
            gather_copy(tok0_ref[0, r], r, 0).start()
            gather_copy(tok1_ref[0, r], r, 1).start()

    prev = blk_e_ref[jnp.maximum(i - 1, 0)]
    fresh = (i == 0) | (blk_e_ref[i] != prev)

    @pl.when(fresh)
    def _():
        w1b_ref[...] = w1_ref[...].astype(_BF16)
        w2b_ref[...] = w2_ref[...].astype(_BF16)

    def valid_step(p):
        ahead = (p + 2) % RING
        behind = (p - 1) % RING
        wait_gather(p)

        @pl.when(i >= 2)
        def _():
            wait_scatter(p)

        for r in range(MOE_BLK):
            gather_copy(tok2_ref[0, r], r, ahead).start(priority=r % 2)
            scatter_copy(slotp_ref[0, r], r, behind).start(priority=(r + 1) % 2)
        x = _load_row_tiles(xbufs[p], MOE_BLK).astype(_BF16)
        hh = jnp.dot(x, w1b_ref[...], preferred_element_type=_F32) + b1_ref[...]
        x_glu = jnp.minimum(hh[:, 0:D_FF], SWIGLU_LIMIT)
        x_lin = jnp.clip(hh[:, D_FF:2 * D_FF], -SWIGLU_LIMIT, SWIGLU_LIMIT)
        act = x_glu * jax.nn.sigmoid(SWIGLU_ALPHA * x_glu) * (x_lin + 1.0)
        _store_row_tiles(ybufs[p], MOE_BLK,
                         jnp.dot(act.astype(_BF16), w2b_ref[...], preferred_element_type=_F32)
                         + b2_ref[...])

    def tail_step(p):
        behind = (p - 1) % RING
        behind2 = (p - 2) % RING

        @pl.when(i < nvalid + 2)
        def _():
            wait_gather(p)

        wait_scatter(p)
        ybufs[p][...] = jnp.zeros((MOE_BLK * SUBL, LANES), _F32)
        for r in range(MOE_BLK):
            scatter_copy(slotp_ref[0, r], r, behind).start()

        @pl.when(i == nb - 1)
        def _():
            for r in range(MOE_BLK):
                scatter_copy(slotc_ref[0, r], r, p).start()
            wait_scatter(behind2)
            wait_scatter(behind)
            wait_scatter(p)

    is_valid = i < nvalid
    for p in range(RING):
        pl.when(is_valid & (i % RING == p))(functools.partial(valid_step, p))
        pl.when((~is_valid) & (i % RING == p))(functools.partial(tail_step, p))


def _expert_call(l, blk_e, nvalid, tok_of_row, slot_of_row, n2, w1, b1, w2, b2):
    wsel = lambda i, e, n: (l, e[i], 0, 0)
    at = lambda d: (lambda i, e, n: (jnp.minimum(i + d, N_MOE_BLOCKS - 1), 0, 0))
    smem_blk = lambda imap: pl.BlockSpec((None, 1, MOE_BLK), imap, memory_space=pltpu.SMEM)
    tok3 = tok_of_row.reshape(N_MOE_BLOCKS, 1, MOE_BLK)
    virtual = N_MOE_ROWS + jnp.arange(MOE_BLK, dtype=jnp.int32)
    slot3 = jnp.concatenate([virtual, slot_of_row]).reshape(N_MOE_BLOCKS + 1, 1, MOE_BLK)
    row_buf = pltpu.VMEM((MOE_BLK * SUBL, LANES), _F32)
    grid_spec = pltpu.PrefetchScalarGridSpec(
        num_scalar_prefetch=2,
        grid=(N_MOE_BLOCKS,),
        in_specs=[
            smem_blk(at(0)), smem_blk(at(1)), smem_blk(at(2)),
            smem_blk(lambda i, e, n: (i, 0, 0)),
            smem_blk(lambda i, e, n: (i + 1, 0, 0)),
            pl.BlockSpec(memory_space=pl.ANY),
            pl.BlockSpec((None, None, D_MODEL, 2 * D_FF), wsel),
            pl.BlockSpec((None, None, 1, 2 * D_FF), wsel),
            pl.BlockSpec((None, None, D_FF, D_MODEL), wsel),
            pl.BlockSpec((None, None, 1, D_MODEL), wsel),
        ],
        out_specs=pl.BlockSpec(memory_space=pl.ANY),
        scratch_shapes=[pltpu.VMEM((D_MODEL, 2 * D_FF), _BF16),
                        pltpu.VMEM((D_FF, D_MODEL), _BF16)]
                       + [row_buf] * (2 * RING)
                       + [pltpu.SemaphoreType.DMA((RING,)), pltpu.SemaphoreType.DMA((RING,))],
    )
    return pl.pallas_call(
        _expert_kernel,
        grid_spec=grid_spec,
        out_shape=jax.ShapeDtypeStruct((N_MOE_ROWS + MOE_BLK, SUBL, LANES), _F32),
        compiler_params=_params(1),
        name="experts",
    )(blk_e, nvalid, tok3, tok3, tok3, slot3, slot3, n2, w1,
      b1.reshape(DEPTH, N_EXPERTS, 1, 2 * D_FF), w2, b2.reshape(DEPTH, N_EXPERTS, 1, D_MODEL))


def _combine_kernel(modrow_ref, h1_ref, mod_ref, ng_ref, y0_ref, y1_ref, y2_ref, y3_ref, w_ref,
                    oc_ref, ol_ref):
    i = pl.program_id(0)
    mod = mod_ref[...]
    g2 = mod[:, 5 * D_MODEL:6 * D_MODEL]
    w = w_ref[...]
    f = jnp.zeros((TM, D_MODEL), _F32)
    for k, y_ref in enumerate((y0_ref, y1_ref, y2_ref, y3_ref)):
        f = f + _load_row_tiles(y_ref, TM) * w[:, k:k + 1]
    h2 = h1_ref[...] + g2 * _rms(f, ng_ref[3:4, :])

    @pl.when(i < CTX_TILES)
    def _():
        oc_ref[...] = h2

    @pl.when(i >= CTX_TILES)
    def _():
        ol_ref[...] = h2


def _combine_call(h1, mod3, norm_g, yslot, topw, modrow):
    row = lambda i, a: (i, 0)
    choice = lambda k: pl.BlockSpec((TM * SUBL, LANES), lambda i, a: (k * N_TILES + i, 0))
    yslot = yslot.reshape(-1, LANES)
    grid_spec = pltpu.PrefetchScalarGridSpec(
        num_scalar_prefetch=1,
        grid=(N_TILES,),
        in_specs=[
            pl.BlockSpec((TM, D_MODEL), row),
            pl.BlockSpec((None, 1, 6 * D_MODEL), lambda i, a: (a[i], 0, 0)),
            pl.BlockSpec((4, D_MODEL), lambda i, a: (0, 0)),
            choice(0), choice(1), choice(2), choice(3),
            pl.BlockSpec((TM, LANES), row),
        ],
        out_specs=[pl.BlockSpec((TM, D_MODEL), _ctx_rows), pl.BlockSpec((TM, D_MODEL), _lat_rows)],
    )
    return pl.pallas_call(
        _combine_kernel,
        grid_spec=grid_spec,
        out_shape=[jax.ShapeDtypeStruct((T_CTX, D_MODEL), _F32),
                   jax.ShapeDtypeStruct((T_LAT, D_MODEL), _F32)],
        compiler_params=_params(1),
        name="moe_combine",
    )(modrow, h1, mod3, norm_g, yslot, yslot, yslot, yslot, topw)


def _tile_tables():
    t = np.arange(N_TILES)
    lat = t >= CTX_TILES
    j = (t - CTX_TILES) % LAT_TILES_PER_SEQ
    modrow = np.where(lat, 1 + (t - CTX_TILES) // LAT_TILES_PER_SEQ, 0)
    ropeblk = np.where(lat, j, LAT_TILES_PER_SEQ)
    stblk = np.where(lat, CTX_TILES, t)
    hasprev = np.where(lat & (j > 0), 1, 0)
    hasnext = np.where(lat & (j < LAT_TILES_PER_SEQ - 1), 1, 0)
    as_i32 = lambda a: jnp.asarray(a, jnp.int32)
    return as_i32(modrow), as_i32(ropeblk), as_i32(stblk), as_i32(hasprev), as_i32(hasnext)


def _rope_tables():
    f32 = np.float32
    half = HEAD_DIM // 4
    freqs = np.power(f32(ROPE_THETA), -np.arange(half, dtype=f32) / f32(half)).astype(f32)
    t = np.arange(DEC_SEQ)
    pos = np.stack([t // GRID_W, t % GRID_W], axis=1).astype(f32)
    ang = (pos[:, :, None] * freqs[None, None, :]).astype(f32)
    cos = np.cos(ang).astype(f32)
    sin = np.sin(ang).astype(f32)
    cos_h = np.concatenate([cos, cos], axis=-1).reshape(DEC_SEQ, HEAD_DIM)
    sin_h = np.concatenate([-sin, sin], axis=-1).reshape(DEC_SEQ, HEAD_DIM)
    cos_t = np.concatenate([np.tile(cos_h, (1, N_HEADS)), np.ones((TM, BR_WIDTH), f32)], axis=0)
    sin_t = np.concatenate([np.tile(sin_h, (1, N_HEADS)), np.zeros((TM, BR_WIDTH), f32)], axis=0)
    return jnp.asarray(cos_t), jnp.asarray(sin_t)


def _permute_heads_cols(w):
    lead = w.shape[:-1]
    return w.reshape(lead + (N_HEADS, HEAD_DIM))[..., jnp.array([0, 2, 1, 3]), :].reshape(lead + (BR_WIDTH,))


def _cache_rows(cache, l):
    return cache[:, l].reshape(DEC_BATCH, PAST_LEN, KV_WIDTH).astype(_BF16)


PAIR_CHUNK = 4096
SMEM_1D_TILE = 1024
PAIR_ROWS = -(-N_MOE_ROWS // SMEM_1D_TILE) * SMEM_1D_TILE


def _row_pairs_kernel(dest_ref, out_ref):
    step = pl.program_id(0)

    @pl.when(step == 0)
    def _():
        def init(j, carry):
            out_ref[j] = -1
            return carry
        lax.fori_loop(0, PAIR_ROWS, init, 0, unroll=32)

    base = step * PAIR_CHUNK

    def body(j, carry):
        out_ref[dest_ref[0, j]] = base + j
        return carry
    lax.fori_loop(0, PAIR_CHUNK, body, 0, unroll=32)


def _row_pairs_call(dest_flat):
    n_chunks = N_SLOTS // PAIR_CHUNK
    return pl.pallas_call(
        _row_pairs_kernel,
        grid=(n_chunks,),
        in_specs=[pl.BlockSpec((None, 1, PAIR_CHUNK), lambda s: (s, 0, 0), memory_space=pltpu.SMEM)],
        out_specs=pl.BlockSpec(memory_space=pltpu.SMEM),
        out_shape=jax.ShapeDtypeStruct((PAIR_ROWS,), jnp.int32),
        compiler_params=_params(1),
        name="row_pairs",
    )(dest_flat.reshape(n_chunks, 1, PAIR_CHUNK))


def _lookup(table, idx):
    hit = idx[..., None] == jnp.arange(N_EXPERTS, dtype=jnp.int32)
    return jnp.sum(jnp.where(hit, table, 0), axis=-1)


def _moe_layout(cnt, topi, rank):
    counts = cnt[0, :N_EXPERTS]
    padded = (counts + MOE_BLK - 1) // MOE_BLK * MOE_BLK
    pad_end = jnp.cumsum(padded)
    pad_start = pad_end - padded
    blk_row0 = jnp.arange(N_MOE_BLOCKS, dtype=jnp.int32) * MOE_BLK
    blk_e = jnp.minimum(jnp.sum(pad_end[None, :] <= blk_row0[:, None], axis=1),
                        N_EXPERTS - 1).astype(jnp.int32)
    nvalid = (pad_end[-1:] // MOE_BLK).astype(jnp.int32)
    spare_before = jnp.cumsum(padded - counts) - (padded - counts)
    row = jnp.arange(N_MOE_ROWS, dtype=jnp.int32).reshape(N_MOE_BLOCKS, MOE_BLK)
    in_tail = (blk_row0 >= pad_end[-1])[:, None]
    spare_in_group = row + _lookup(spare_before - pad_start - counts, blk_e)[:, None]
    spare_tail = row - pad_end[-1] + jnp.sum(padded - counts)
    spare = N_SLOTS + jnp.where(in_tail, spare_tail, spare_in_group)
    dest = _lookup(pad_start, topi[:, :TOP_K]) + rank[:, :TOP_K]
    pair_of_row = _row_pairs_call(dest.reshape(-1))[:N_MOE_ROWS]
    is_pair = pair_of_row >= 0
    tok_of_row = jnp.where(is_pair, pair_of_row // TOP_K, 0)
    slot_of_row = jnp.where(is_pair, (pair_of_row % TOP_K) * T_ALL + tok_of_row,
                            spare.reshape(-1).astype(jnp.int32))
    return blk_e, nvalid, tok_of_row, slot_of_row


def kernel(x_prompt, x_sample, cache_win_k, cache_win_v, cache_full_k, cache_full_v, c, c_ctx,
           w_mod, b_mod, norm_g, w_in, a_ln_g, a_ln_b, w_sp, b_sp, conv_w, conv_b, b_ln_g, b_ln_b,
           sink, qn_g, kn_g, w_br, w_o, w_router, b_router, w1, b1, w2, b2):
    modrow, ropeblk, stblk, hasprev, hasnext = _tile_tables()
    cos_t, sin_t = _rope_tables()
    head_id = np.arange(BR_WIDTH) // HEAD_DIM
    ones_bd = jnp.asarray(head_id[:, None] == head_id[None, :], _BF16)
    ltri = jnp.asarray(np.tril(np.ones((TM, TM)), -1), _BF16)
    row2 = lambda v: v.reshape(1, -1)

    cvec = jnp.concatenate([c_ctx[None, :], c, jnp.zeros((8 - 1 - DEC_BATCH, D_MODEL), _F32)], axis=0)
    mod_all = _mod_call(cvec, w_mod, b_mod)

    hc = x_prompt.reshape(T_CTX, D_MODEL)
    hl = x_sample.reshape(T_LAT, D_MODEL)
    states = []
    for l in range(DEPTH):
        mod3 = mod_all[l].reshape(8, 1, 6 * D_MODEL)
        wl = w_in[l]
        w_main = jnp.concatenate([
            wl[:, 0:1024],
            _permute_heads_cols(wl[:, 1024:1280]), wl[:, 1280:1536],
            _permute_heads_cols(wl[:, 1536:1792]), wl[:, 1792:2048]], axis=1).astype(_BF16)
        wg = wl[:, MAIN_WIDTH:].astype(_BF16)
        qn = jnp.tile(qn_g[l], N_HEADS).reshape(1, BR_WIDTH)
        kn = jnp.tile(kn_g[l], KV_HEADS).reshape(1, KV_WIDTH)

        ab, qc, kc, vc, qd, kd, vd, st = _inproj_call(
            hc, hl, mod3, row2(norm_g[l, 0]), w_main, qn, kn, ones_bd, cos_t, sin_t,
            (modrow, ropeblk, stblk))
        states.append(st[:T_CTX].reshape(BATCH, SEQ, 4, KV_HEADS, HEAD_DIM))

        bsp = jnp.repeat(b_sp[l].T, BR_WIDTH // A_GROUPS, axis=1)
        cw = jnp.concatenate([conv_w[l], jnp.zeros((1, BR_WIDTH), _F32)], axis=0)
        oa, ob = _mix_call(ab, row2(a_ln_g[l]), row2(a_ln_b[l]), w_sp[l].astype(_BF16), bsp, cw,
                           row2(conv_b[l]), row2(b_ln_g[l]), row2(b_ln_b[l]), (hasprev, hasnext))

        occ, odc = _ctx_attn_call(sink[l], qc, kc, vc, qd, kd, vd)
        ocl = _win_attn_call(sink[l], qc, kc, vc, _cache_rows(cache_win_k, l),
                             _cache_rows(cache_win_v, l))
        odl = _full_attn_call(qd, kd, vd, _cache_rows(cache_full_k, l),
                              _cache_rows(cache_full_v, l))

        perm = jnp.array([0, 2, 1, 3])
        wbr = w_br[l].reshape(N_BRANCH, N_HEADS, HEAD_DIM, D_MODEL)
        wbr = jnp.concatenate([wbr[0:2], wbr[2:4][:, perm]], axis=0)
        wbr = wbr.reshape(N_BRANCH, BR_WIDTH, D_MODEL).astype(_BF16)
        wr = jnp.concatenate([w_router[l], jnp.zeros((D_MODEL, LANES - N_EXPERTS), _F32)], axis=1)
        wr_hi, wr_lo = _hi_lo(wr)
        br = jnp.concatenate([b_router[l], jnp.full((LANES - N_EXPERTS,), NEG_BIG, _F32)]).reshape(1, LANES)
        h1, n2, topi, topw, rank, cnt = _merge_call(
            hc, hl, mod3, norm_g[l], oa, ob, occ, ocl, odc, odl, wg, wbr, w_o[l].astype(_BF16),
            wr_hi, wr_lo, br, ltri, modrow)

        blk_e, nvalid, tok_of_row, slot_of_row = _moe_layout(cnt, topi, rank)
        yslot = _expert_call(l, blk_e, nvalid, tok_of_row, slot_of_row,
                             n2.reshape(T_ALL, SUBL, LANES), w1, b1, w2, b2)
        hc, hl = _combine_call(h1, mod3, norm_g[l], yslot, topw, modrow)

    y_prompt = hc.reshape(BATCH, SEQ, D_MODEL)
    y_sample = hl.reshape(DEC_BATCH, DEC_SEQ, D_MODEL)
    st = jnp.stack(states, axis=1)
    return (y_prompt, y_sample, st[:, :, :, 0], st[:, :, :, 1], st[:, :, :, 2], st[:, :, :, 3])
```

```python
import functools

import numpy as np
import jax
import jax.numpy as jnp
from jax import lax
from jax.experimental import pallas as pl
from jax.experimental.pallas import tpu as pltpu

D_MODEL = 1024
BATCH = 16
SEQ = 256
DEPTH = 2
DEC_BATCH = 4
DEC_SEQ = 4096
PAST_LEN = 256
GRID_W = 64
HEAD_DIM = 64
BR_WIDTH = 256
N_BRANCH = 4
A_GROUPS = 4
CHUNK = 128
CONV_W = 31
N_HEADS = 4
KV_HEADS = 2
KV_WIDTH = KV_HEADS * HEAD_DIM
WINDOW = 128
ROPE_THETA = 10000.0
EPS = 1e-6
N_EXPERTS = 32
TOP_K = 4
D_FF = D_MODEL
SWIGLU_LIMIT = 7.0
SWIGLU_ALPHA = 1.702
MAIN_WIDTH = 2048

LANES = 128
T_CTX = BATCH * SEQ
T_LAT = DEC_BATCH * DEC_SEQ
T_ALL = T_CTX + T_LAT
TM = 256
MERGE_SUB = 2
N_TILES = T_ALL // TM
CTX_TILES = T_CTX // TM
LAT_TILES_PER_SEQ = DEC_SEQ // TM
HALO = 16
SUBLANES = 8
SHIFT_ROWS = TM + (HALO + CONV_W // 2) // SUBLANES * SUBLANES
MOE_BLK = 256
N_SLOTS = T_ALL * TOP_K
RING = 3
N_MOE_BLOCKS = N_SLOTS // MOE_BLK + N_EXPERTS + 1
N_MOE_ROWS = N_MOE_BLOCKS * MOE_BLK
WIN_TQ = 512
FULL_TQ = 512
FULL_TK = 512
NEG_BIG = -1e30
VMEM_LIMIT = 56 * 1024 * 1024

_F32 = jnp.float32
_BF16 = jnp.bfloat16


def _params(n_axes, vmem=None):
    return pltpu.CompilerParams(
        dimension_semantics=("arbitrary",) * n_axes,
        vmem_limit_bytes=vmem if vmem is not None else VMEM_LIMIT)


def _split_dot(a, b_hi, b_lo):
    a_hi = a.astype(_BF16)
    a_lo = (a - a_hi.astype(_F32)).astype(_BF16)
    return (jnp.dot(a_hi, b_hi, preferred_element_type=_F32)
            + jnp.dot(a_lo, b_hi, preferred_element_type=_F32)
            + jnp.dot(a_hi, b_lo, preferred_element_type=_F32))


def _hi_lo(w):
    hi = w.astype(_BF16)
    lo = (w - hi.astype(_F32)).astype(_BF16)
    return hi, lo


def _mod_kernel(c_ref, w_ref, b_ref, o_ref):
    c = c_ref[...]
    s = c * jax.nn.sigmoid(c)
    w = w_ref[...]
    w_hi = w.astype(_BF16)
    w_lo = (w - w_hi.astype(_F32)).astype(_BF16)
    o_ref[...] = _split_dot(s, w_hi, w_lo) + b_ref[...]


def _mod_call(cvec, w_mod, b_mod):
    tn = 1024
    return pl.pallas_call(
        _mod_kernel,
        grid=(DEPTH, 6 * D_MODEL // tn),
        in_specs=[
            pl.BlockSpec((8, D_MODEL), lambda l, j: (0, 0)),
            pl.BlockSpec((None, D_MODEL, tn), lambda l, j: (l, 0, j)),
            pl.BlockSpec((None, 1, tn), lambda l, j: (l, 0, j)),
        ],
        out_specs=pl.BlockSpec((None, 8, tn), lambda l, j: (l, 0, j)),
        out_shape=jax.ShapeDtypeStruct((DEPTH, 8, 6 * D_MODEL), _F32),
        compiler_params=_params(2),
        name="mod",
    )(cvec, w_mod, b_mod.reshape(DEPTH, 1, 6 * D_MODEL))


def _rms(x, g):
    ms = jnp.mean(x * x, axis=-1, keepdims=True)
    return x * lax.rsqrt(ms + EPS) * g


def _head_rms(x, ones_bd, g):
    xx = x * x
    hi = xx.astype(_BF16)
    lo = (xx - hi.astype(_F32)).astype(_BF16)
    ss = (jnp.dot(hi, ones_bd, preferred_element_type=_F32)
          + jnp.dot(lo, ones_bd, preferred_element_type=_F32))
    return x * lax.rsqrt(ss * (1.0 / HEAD_DIM) + EPS) * g


def _rope(x, cos, sin_signed):
    w = x.shape[-1]
    lane = lax.broadcasted_iota(jnp.int32, x.shape, 1)
    first = (lane % 32) < 16
    partner = jnp.where(first, pltpu.roll(x, w - 16, 1), pltpu.roll(x, 16, 1))
    return x * cos + partner * sin_signed


def _inproj_kernel(modrow_ref, ropeblk_ref, stblk_ref,
                   xc_ref, xl_ref, mod_ref, g0_ref, w_ref, qn_ref, kn_ref, ones_ref,
                   cos_ref, sin_ref,
                   ab_ref, qc_ref, kc_ref, vc_ref, qd_ref, kd_ref, vd_ref, st_ref):
    x = jnp.where(pl.program_id(0) < CTX_TILES, xc_ref[...], xl_ref[...])
    mod = mod_ref[...]
    sh1 = mod[:, 0:D_MODEL]
    sc1 = mod[:, D_MODEL:2 * D_MODEL]
    n = _rms(x, g0_ref[...]) * (1.0 + sc1) + sh1
    z = jnp.dot(n.astype(_BF16), w_ref[...], preferred_element_type=_F32)
    ab_ref[...] = z[:, 0:1024]
    cq = z[:, 1024:1280]
    ck = z[:, 1280:1408]
    cv = z[:, 1408:1536]
    dq = z[:, 1536:1792]
    dk = z[:, 1792:1920]
    dv = z[:, 1920:2048]
    ones = ones_ref[...]
    dq = _head_rms(dq, ones, qn_ref[...])
    dk = _head_rms(dk, ones[0:KV_WIDTH, 0:KV_WIDTH], kn_ref[...])
    cos = cos_ref[...]
    sin = sin_ref[...]
    scale = HEAD_DIM ** -0.5
    qc_ref[...] = (_rope(cq, cos, sin) * scale).astype(_BF16)
    qd_ref[...] = (_rope(dq, cos, sin) * scale).astype(_BF16)
    kc = _rope(ck, cos[:, 0:KV_WIDTH], sin[:, 0:KV_WIDTH])
    kd = _rope(dk, cos[:, 0:KV_WIDTH], sin[:, 0:KV_WIDTH])
    kc_ref[...] = kc.astype(_BF16)
    kd_ref[...] = kd.astype(_BF16)
    vc_ref[...] = cv.astype(_BF16)
    vd_ref[...] = dv.astype(_BF16)
    st_ref[:, 0:128] = kc
    st_ref[:, 128:256] = cv
    st_ref[:, 256:384] = kd
    st_ref[:, 384:512] = dv


def _ctx_rows(*_):
    i = _[0]
    return (jnp.minimum(i, CTX_TILES - 1), 0)


def _lat_rows(*_):
    i = _[0]
    return (jnp.maximum(i - CTX_TILES, 0), 0)


def _inproj_call(xc, xl, mod3, g0, w_main, qn, kn, ones_bd, cos_t, sin_t, tables):
    modrow, ropeblk, stblk = tables
    row = lambda i, a, b, c: (i, 0)
    full = lambda i, a, b, c: (0, 0)
    bf = lambda w: jax.ShapeDtypeStruct((T_ALL, w), _BF16)
    grid_spec = pltpu.PrefetchScalarGridSpec(
        num_scalar_prefetch=3,
        grid=(N_TILES,),
        in_specs=[
            pl.BlockSpec((TM, D_MODEL), _ctx_rows),
            pl.BlockSpec((TM, D_MODEL), _lat_rows),
            pl.BlockSpec((None, 1, 6 * D_MODEL), lambda i, a, b, c: (a[i], 0, 0)),
            pl.BlockSpec((1, D_MODEL), full),
            pl.BlockSpec((D_MODEL, MAIN_WIDTH), full),
            pl.BlockSpec((1, BR_WIDTH), full),
            pl.BlockSpec((1, KV_WIDTH), full),
            pl.BlockSpec((BR_WIDTH, BR_WIDTH), full),
            pl.BlockSpec((TM, BR_WIDTH), lambda i, a, b, c: (b[i], 0)),
            pl.BlockSpec((TM, BR_WIDTH), lambda i, a, b, c: (b[i], 0)),
        ],
        out_specs=[
            pl.BlockSpec((TM, 1024), row),
            pl.BlockSpec((TM, BR_WIDTH), row),
            pl.BlockSpec((TM, KV_WIDTH), row),
            pl.BlockSpec((TM, KV_WIDTH), row),
            pl.BlockSpec((TM, BR_WIDTH), row),
            pl.BlockSpec((TM, KV_WIDTH), row),
            pl.BlockSpec((TM, KV_WIDTH), row),
            pl.BlockSpec((TM, 512), lambda i, a, b, c: (c[i], 0)),
        ],
    )
    return pl.pallas_call(
        _inproj_kernel,
        grid_spec=grid_spec,
        out_shape=[
            jax.ShapeDtypeStruct((T_ALL, 1024), _F32),
            bf(BR_WIDTH), bf(KV_WIDTH), bf(KV_WIDTH),
            bf(BR_WIDTH), bf(KV_WIDTH), bf(KV_WIDTH),
            jax.ShapeDtypeStruct((T_CTX + TM, 512), _F32),
        ],
        compiler_params=_params(1),
        name="inproj",
    )(modrow, ropeblk, stblk, xc, xl, mod3, g0, w_main, qn, kn, ones_bd, cos_t, sin_t)


def _layernorm(x, g, b):
    mu = jnp.mean(x, axis=-1, keepdims=True)
    xc = x - mu
    var = jnp.mean(xc * xc, axis=-1, keepdims=True)
    return xc * lax.rsqrt(var + EPS) * g + b


def _gelu(x):
    return 0.5 * x * (1.0 + lax.erf(x * (2.0 ** -0.5)))


def _mix_kernel(hasprev_ref, hasnext_ref,
                ab_ref, prev_ref, next_ref, alg_ref, alb_ref, wsp_ref, bsp_ref,
                cw_ref, cb_ref, blg_ref, blb_ref,
                oa_ref, ob_ref, buf_ref, sh_ref):
    i = pl.program_id(0)
    ab = ab_ref[...]
    u = _gelu(ab[:, 0:256])
    va = _layernorm(_gelu(ab[:, 256:512]), alg_ref[...], alb_ref[...])
    vab = va.astype(_BF16)
    lane = lax.broadcasted_iota(jnp.int32, (CHUNK, BR_WIDTH), 1)
    group = lane // (BR_WIDTH // A_GROUPS)
    bsp = bsp_ref[...]
    for c in range(TM // CHUNK):
        v_c = vab[c * CHUNK:(c + 1) * CHUNK, :]
        sp = bsp
        for g in range(A_GROUPS):
            full = jnp.dot(wsp_ref[g], v_c, preferred_element_type=_F32)
            sp = sp + jnp.where(group == g, full, 0.0)
        oa_ref[c * CHUNK:(c + 1) * CHUNK, :] = (u[c * CHUNK:(c + 1) * CHUNK, :] * sp).astype(_BF16)
    glu = ab[:, 512:768] * jax.nn.sigmoid(ab[:, 768:1024])
    pv = prev_ref[...]
    nx = next_ref[...]
    hp = hasprev_ref[i].astype(_F32)
    hn = hasnext_ref[i].astype(_F32)
    buf_ref[0:HALO, :] = pv[:, 0:256] * jax.nn.sigmoid(pv[:, 256:512]) * hp
    buf_ref[HALO:HALO + TM, :] = glu
    buf_ref[HALO + TM:HALO + TM + HALO, :] = nx[:, 0:256] * jax.nn.sigmoid(nx[:, 256:512]) * hn
    for ph in range(1, SUBLANES):
        sh_ref[ph - 1] = buf_ref[ph:ph + SHIFT_ROWS, :]
    cw = cw_ref[...]
    acc = jnp.zeros((TM, BR_WIDTH), _F32) + cb_ref[...]
    off = HALO - CONV_W // 2
    for k in range(CONV_W):
        start, ph = (off + k) // SUBLANES * SUBLANES, (off + k) % SUBLANES
        src = buf_ref if ph == 0 else sh_ref.at[ph - 1]
        acc = acc + src[start:start + TM, :] * cw[k:k + 1, :]
    y = _layernorm(acc, blg_ref[...], blb_ref[...])
    ob_ref[...] = (y * jax.nn.sigmoid(y)).astype(_BF16)


def _mix_call(ab, a_ln_g, a_ln_b, wsp, bsp, conv_w, conv_b, b_ln_g, b_ln_b, tables):
    hasprev, hasnext = tables
    n_halo_blocks = T_ALL // HALO
    per = TM // HALO
    full2 = lambda i, a, b: (0, 0)
    grid_spec = pltpu.PrefetchScalarGridSpec(
        num_scalar_prefetch=2,
        grid=(N_TILES,),
        in_specs=[
            pl.BlockSpec((TM, 1024), lambda i, a, b: (i, 0)),
            pl.BlockSpec((HALO, 512), lambda i, a, b: (jnp.maximum(i * per - 1, 0), 1)),
            pl.BlockSpec((HALO, 512), lambda i, a, b: (jnp.minimum((i + 1) * per, n_halo_blocks - 1), 1)),
            pl.BlockSpec((1, BR_WIDTH), full2),
            pl.BlockSpec((1, BR_WIDTH), full2),
            pl.BlockSpec((A_GROUPS, CHUNK, CHUNK), lambda i, a, b: (0, 0, 0)),
            pl.BlockSpec((CHUNK, BR_WIDTH), full2),
            pl.BlockSpec((32, BR_WIDTH), full2),
            pl.BlockSpec((1, BR_WIDTH), full2),
            pl.BlockSpec((1, BR_WIDTH), full2),
            pl.BlockSpec((1, BR_WIDTH), full2),
        ],
        out_specs=[
            pl.BlockSpec((TM, BR_WIDTH), lambda i, a, b: (i, 0)),
            pl.BlockSpec((TM, BR_WIDTH), lambda i, a, b: (i, 0)),
        ],
        scratch_shapes=[pltpu.VMEM((TM + 2 * HALO, BR_WIDTH), _F32),
                        pltpu.VMEM((SUBLANES - 1, SHIFT_ROWS, BR_WIDTH), _F32)],
    )
    return pl.pallas_call(
        _mix_kernel,
        grid_spec=grid_spec,
        out_shape=[jax.ShapeDtypeStruct((T_ALL, BR_WIDTH), _BF16)] * 2,
        compiler_params=_params(1),
        name="mixers_ab",
    )(hasprev, hasnext, ab, ab, ab, a_ln_g, a_ln_b, wsp, bsp, conv_w, conv_b, b_ln_g, b_ln_b)


def _lane_masks(n_rows):
    lane = lax.broadcasted_iota(jnp.int32, (n_rows, LANES), 1)
    return lane < HEAD_DIM, lane >= HEAD_DIM


def _group_queries(q, kv):
    tq = q.shape[0]
    lo, hi = _lane_masks(tq)
    keep = lo if kv == 0 else hi
    zero = jnp.zeros((), q.dtype)
    return jnp.concatenate([jnp.where(keep, q[:, 0:LANES], zero),
                            jnp.where(keep, q[:, LANES:2 * LANES], zero)], axis=0)


def _scores(qs, k):
    return lax.dot_general(qs, k, (((1,), (1,)), ((), ())), preferred_element_type=_F32)


def _attend_once(q, chunks, sinks):
    tq = q.shape[0]
    lo, hi = _lane_masks(tq)
    outs = [jnp.zeros((tq, LANES), _F32), jnp.zeros((tq, LANES), _F32)]
    for kv in range(KV_HEADS):
        qs = _group_queries(q, kv)
        ss = []
        for k, _, mask in chunks:
            s = _scores(qs, k)
            if mask is not None:
                s = jnp.where(mask, s, -jnp.inf)
            ss.append(s)
        m = ss[0].max(axis=-1, keepdims=True)
        for s in ss[1:]:
            m = jnp.maximum(m, s.max(axis=-1, keepdims=True))
        if sinks is not None:
            row = lax.broadcasted_iota(jnp.int32, (2 * tq, 1), 0)
            sink_col = jnp.where(row < tq, sinks[2 * kv], sinks[2 * kv + 1])
            m = jnp.maximum(m, sink_col)
            l = jnp.exp(sink_col - m)
        else:
            l = jnp.zeros((2 * tq, 1), _F32)
        acc = jnp.zeros((2 * tq, LANES), _F32)
        for s, (_, v, _) in zip(ss, chunks):
            p = jnp.exp(s - m)
            l = l + p.sum(axis=-1, keepdims=True)
            acc = acc + jnp.dot(p.astype(_BF16), v, preferred_element_type=_F32)
        o = acc / l
        keep = lo if kv == 0 else hi
        outs[0] = outs[0] + jnp.where(keep, o[0:tq], 0.0)
        outs[1] = outs[1] + jnp.where(keep, o[tq:2 * tq], 0.0)
    return jnp.concatenate(outs, axis=1)


def _ctx_attn_kernel(sink_ref, qc_ref, kc_ref, vc_ref, qd_ref, kd_ref, vd_ref, oc_ref, od_ref):
    sinks = [sink_ref[h] for h in range(N_HEADS)]
    oc_ref[...] = _attend_once(qc_ref[...], [(kc_ref[...], vc_ref[...], None)], sinks).astype(_BF16)
    od_ref[...] = _attend_once(qd_ref[...], [(kd_ref[...], vd_ref[...], None)], None).astype(_BF16)


def _ctx_attn_call(sink, qc, kc, vc, qd, kd, vd):
    seq = lambda b: (b, 0)
    q_spec = pl.BlockSpec((SEQ, BR_WIDTH), seq)
    kv_spec = pl.BlockSpec((SEQ, KV_WIDTH), seq)
    return pl.pallas_call(
        _ctx_attn_kernel,
        grid=(BATCH,),
        in_specs=[pl.BlockSpec(memory_space=pltpu.SMEM),
                  q_spec, kv_spec, kv_spec, q_spec, kv_spec, kv_spec],
        out_specs=[q_spec, q_spec],
        out_shape=[jax.ShapeDtypeStruct((T_CTX, BR_WIDTH), _BF16)] * 2,
        compiler_params=_params(1),
        name="ctx_attention",
    )(sink, qc, kc, vc, qd, kd, vd)


def _win_attn_kernel(sink_ref, q_ref, kcur_ref, kprev_ref, knext_ref, vcur_ref, vprev_ref,
                     vnext_ref, kctx_ref, vctx_ref, oc_ref):
    j = pl.program_id(1)
    nq = pl.num_programs(1)
    sinks = [sink_ref[h] for h in range(N_HEADS)]
    tq = WIN_TQ
    def rel(n_keys):
        qi = lax.broadcasted_iota(jnp.int32, (2 * tq, n_keys), 0) % tq
        kj = lax.broadcasted_iota(jnp.int32, (2 * tq, n_keys), 1)
        return qi, kj
    qi, kj = rel(tq)
    m_cur = jnp.abs(qi - kj) <= WINDOW
    qi, kj = rel(WINDOW)
    m_prev = ((qi - (kj - WINDOW)) <= WINDOW) & (j > 0)
    m_next = (((kj + tq) - qi) <= WINDOW) & (j < nq - 1)
    chunks = [
        (kcur_ref[...], vcur_ref[...], m_cur),
        (kprev_ref[...], vprev_ref[...], m_prev),
        (knext_ref[...], vnext_ref[...], m_next),
        (kctx_ref[...], vctx_ref[...], None),
    ]
    oc_ref[...] = _attend_once(q_ref[...], chunks, sinks).astype(_BF16)


def _win_attn_call(sink, qc, kc, vc, kctx, vctx):
    nq = DEC_SEQ // WIN_TQ
    ctx_q = T_CTX // WIN_TQ
    sub = WIN_TQ // WINDOW
    ctx_w = T_CTX // WINDOW
    per_seq_w = DEC_SEQ // WINDOW
    cur = lambda b, j: (ctx_q + b * nq + j, 0)
    prev = lambda b, j: (ctx_w + b * per_seq_w + jnp.maximum(j * sub - 1, 0), 0)
    nxt = lambda b, j: (ctx_w + b * per_seq_w + jnp.minimum((j + 1) * sub, per_seq_w - 1), 0)
    cache = lambda b, j: (b, 0, 0)
    return pl.pallas_call(
        _win_attn_kernel,
        grid=(DEC_BATCH, nq),
        in_specs=[
            pl.BlockSpec(memory_space=pltpu.SMEM),
            pl.BlockSpec((WIN_TQ, BR_WIDTH), cur),
            pl.BlockSpec((WIN_TQ, KV_WIDTH), cur),
            pl.BlockSpec((WINDOW, KV_WIDTH), prev),
            pl.BlockSpec((WINDOW, KV_WIDTH), nxt),
            pl.BlockSpec((WIN_TQ, KV_WIDTH), cur),
            pl.BlockSpec((WINDOW, KV_WIDTH), prev),
            pl.BlockSpec((WINDOW, KV_WIDTH), nxt),
            pl.BlockSpec((None, PAST_LEN, KV_WIDTH), cache),
            pl.BlockSpec((None, PAST_LEN, KV_WIDTH), cache),
        ],
        out_specs=pl.BlockSpec((WIN_TQ, BR_WIDTH), lambda b, j: (b * nq + j, 0)),
        out_shape=jax.ShapeDtypeStruct((T_LAT, BR_WIDTH), _BF16),
        compiler_params=_params(2),
        name="window_attention",
    )(sink, qc, kc, kc, kc, vc, vc, vc, kctx, vctx)


def _full_attn_kernel(q_ref, k_ref, v_ref, kctx_ref, vctx_ref, od_ref):
    tq = FULL_TQ
    q = q_ref[...]
    lo, hi = _lane_masks(tq)
    qss = [_group_queries(q, kv) for kv in range(KV_HEADS)]

    def update(carries, k, v):
        new = []
        for qs, (m, l, acc) in zip(qss, carries):
            s = _scores(qs, k)
            m_new = jnp.maximum(m, s.max(axis=-1, keepdims=True))
            a = jnp.exp(m - m_new)
            p = jnp.exp(s - m_new)
            l = a * l + p.sum(axis=-1, keepdims=True)
            acc = a * acc + jnp.dot(p.astype(_BF16), v, preferred_element_type=_F32)
            new.append((m_new, l, acc))
        return tuple(new)

    def body(c, carries):
        start = pl.multiple_of(c * FULL_TK, FULL_TK)
        return update(carries, k_ref[pl.ds(start, FULL_TK), :], v_ref[pl.ds(start, FULL_TK), :])

    init = (jnp.full((2 * tq, 1), -jnp.inf, _F32), jnp.zeros((2 * tq, 1), _F32),
            jnp.zeros((2 * tq, LANES), _F32))
    carries = lax.fori_loop(0, DEC_SEQ // FULL_TK, body, (init, init))
    carries = update(carries, kctx_ref[...], vctx_ref[...])
    outs = [jnp.zeros((tq, LANES), _F32), jnp.zeros((tq, LANES), _F32)]
    for kv, (m, l, acc) in enumerate(carries):
        o = acc / l
        keep = lo if kv == 0 else hi
        outs[0] = outs[0] + jnp.where(keep, o[0:tq], 0.0)
        outs[1] = outs[1] + jnp.where(keep, o[tq:2 * tq], 0.0)
    od_ref[...] = jnp.concatenate(outs, axis=1).astype(_BF16)


def _full_attn_call(qd, kd, vd, kctx, vctx):
    nq = DEC_SEQ // FULL_TQ
    ctx_q = T_CTX // FULL_TQ
    ctx_seq = T_CTX // DEC_SEQ
    cur = lambda b, j: (ctx_q + b * nq + j, 0)
    seq = lambda b, j: (ctx_seq + b, 0)
    cache = lambda b, j: (b, 0, 0)
    return pl.pallas_call(
        _full_attn_kernel,
        grid=(DEC_BATCH, nq),
        in_specs=[
            pl.BlockSpec((FULL_TQ, BR_WIDTH), cur),
            pl.BlockSpec((DEC_SEQ, KV_WIDTH), seq),
            pl.BlockSpec((DEC_SEQ, KV_WIDTH), seq),
            pl.BlockSpec((None, PAST_LEN, KV_WIDTH), cache),
            pl.BlockSpec((None, PAST_LEN, KV_WIDTH), cache),
        ],
        out_specs=pl.BlockSpec((FULL_TQ, BR_WIDTH), lambda b, j: (b * nq + j, 0)),
        out_shape=jax.ShapeDtypeStruct((T_LAT, BR_WIDTH), _BF16),
        compiler_params=_params(2),
        name="full_attention",
    )(qd, kd, vd, kctx, vctx)


SUBL = D_MODEL // LANES


def _store_row_tiles(ref, n, x, base=0):
    for c in range(SUBL):
        ref[pl.ds(base + c, n, stride=SUBL), :] = x[:, c * LANES:(c + 1) * LANES]


def _load_row_tiles(ref, n):
    return jnp.concatenate([ref[pl.ds(c, n, stride=SUBL), :] for c in range(SUBL)], axis=1)


def _merge_kernel(modrow_ref,
                  xc_ref, xl_ref, mod_ref, ng_ref, oa_ref, ob_ref, occ_ref, ocl_ref, odc_ref, odl_ref,
                  wg_ref, wbr_ref, wo_ref, wrh_ref, wrl_ref, br_ref, ltri_ref,
                  h1_ref, n2_ref, topi_ref, topw_ref, rank_ref, cnt_ref, base_ref):
    i = pl.program_id(0)

    @pl.when(i == 0)
    def _():
        base_ref[...] = jnp.zeros_like(base_ref)

    is_ctx = i < CTX_TILES // MERGE_SUB
    mod = mod_ref[...]
    sh1 = mod[:, 0:D_MODEL]
    sc1 = mod[:, D_MODEL:2 * D_MODEL]
    g1 = mod[:, 2 * D_MODEL:3 * D_MODEL]
    sh2 = mod[:, 3 * D_MODEL:4 * D_MODEL]
    sc2 = mod[:, 4 * D_MODEL:5 * D_MODEL]
    ng = ng_ref[...]
    lane = lax.broadcasted_iota(jnp.int32, (TM, LANES), 1).astype(_F32)
    ltri = ltri_ref[...]
    running = base_ref[0:1, :]
    for sub in range(MERGE_SUB):
        rows = pl.ds(sub * TM, TM)
        x = jnp.where(is_ctx, xc_ref[rows, :], xl_ref[rows, :])
        nb = (_rms(x, ng[0:1, :]) * (1.0 + sc1) + sh1).astype(_BF16)
        oc = jnp.where(is_ctx, occ_ref[rows, :], ocl_ref[rows, :])
        od = jnp.where(is_ctx, odc_ref[rows, :], odl_ref[rows, :])
        branches = (oa_ref[rows, :], ob_ref[rows, :], oc, od)
        mixed = jnp.zeros((TM, D_MODEL), _F32)
        for k in range(N_BRANCH):
            gate = jax.nn.sigmoid(jnp.dot(nb, wg_ref[:, k * D_MODEL:(k + 1) * D_MODEL],
                                          preferred_element_type=_F32))
            proj = jnp.dot(branches[k], wbr_ref[k], preferred_element_type=_F32)
            mixed = mixed + gate * proj
        mix = jnp.dot(mixed.astype(_BF16), wo_ref[...], preferred_element_type=_F32)
        h1 = x + g1 * _rms(mix, ng[1:2, :])
        h1_ref[rows, :] = h1
        n2 = _rms(h1, ng[2:3, :]) * (1.0 + sc2) + sh2
        _store_row_tiles(n2_ref, TM, n2, base=sub * TM * SUBL)
        logits = _split_dot(n2, wrh_ref[...], wrl_ref[...]) + br_ref[...]
        work = logits
        vals, idxs = [], []
        for _ in range(TOP_K):
            m = work.max(axis=-1, keepdims=True)
            idx = jnp.where(work == m, lane, float(LANES)).min(axis=-1, keepdims=True)
            vals.append(m)
            idxs.append(idx)
            work = jnp.where(lane == idx, -jnp.inf, work)
        es = [jnp.exp(v - vals[0]) for v in vals]
        denom = es[0] + es[1] + es[2] + es[3]
        topi = jnp.zeros((TM, LANES), _F32)
        topw = jnp.zeros((TM, LANES), _F32)
        rank = jnp.zeros((TM, LANES), _F32)
        for k in range(TOP_K):
            onehot = (lane == idxs[k]).astype(_F32)
            before = jnp.dot(ltri, onehot.astype(_BF16), preferred_element_type=_F32)
            r = (onehot * (before + running)).sum(axis=-1, keepdims=True)
            running = running + onehot.sum(axis=0, keepdims=True)
            topi = jnp.where(lane == k, idxs[k], topi)
            topw = jnp.where(lane == k, es[k] / denom, topw)
            rank = jnp.where(lane == k, r, rank)
        topi_ref[rows, :] = topi.astype(jnp.int32)
        topw_ref[rows, :] = topw
        rank_ref[rows, :] = rank.astype(jnp.int32)
    base_ref[0:1, :] = running
    cnt_ref[...] = jnp.broadcast_to(running, (8, LANES)).astype(jnp.int32)


def _merge_call(xc, xl, mod3, norm_g, oa, ob, occ, ocl, odc, odl, wg, wbr, wo, wr_hi, wr_lo, br, ltri,
                modrow):
    mt = TM * MERGE_SUB
    n_ctx = CTX_TILES // MERGE_SUB
    row = lambda i, a: (i, 0)
    full = lambda i, a: (0, 0)
    ctx_rows = lambda i, a: (jnp.minimum(i, n_ctx - 1), 0)
    lat_rows = lambda i, a: (jnp.maximum(i - n_ctx, 0), 0)
    act = pl.BlockSpec((mt, BR_WIDTH), row)
    act_ctx = pl.BlockSpec((mt, BR_WIDTH), ctx_rows)
    act_lat = pl.BlockSpec((mt, BR_WIDTH), lat_rows)
    wide = pl.BlockSpec((mt, LANES), row)
    grid_spec = pltpu.PrefetchScalarGridSpec(
        num_scalar_prefetch=1,
        grid=(N_TILES // MERGE_SUB,),
        in_specs=[
            pl.BlockSpec((mt, D_MODEL), ctx_rows),
            pl.BlockSpec((mt, D_MODEL), lat_rows),
            pl.BlockSpec((None, 1, 6 * D_MODEL), lambda i, a: (a[i * MERGE_SUB], 0, 0)),
            pl.BlockSpec((4, D_MODEL), full),
            act, act, act_ctx, act_lat, act_ctx, act_lat,
            pl.BlockSpec((D_MODEL, N_BRANCH * D_MODEL), full),
            pl.BlockSpec((N_BRANCH, BR_WIDTH, D_MODEL), lambda i, a: (0, 0, 0)),
            pl.BlockSpec((D_MODEL, D_MODEL), full),
            pl.BlockSpec((D_MODEL, LANES), full),
            pl.BlockSpec((D_MODEL, LANES), full),
            pl.BlockSpec((1, LANES), full),
            pl.BlockSpec((TM, TM), full),
        ],
        out_specs=[
            pl.BlockSpec((mt, D_MODEL), row),
            pl.BlockSpec((mt * SUBL, LANES), row),
            wide, wide, wide,
            pl.BlockSpec((8, LANES), full),
        ],
        scratch_shapes=[pltpu.VMEM((8, LANES), _F32)],
    )
    return pl.pallas_call(
        _merge_kernel,
        grid_spec=grid_spec,
        out_shape=[
            jax.ShapeDtypeStruct((T_ALL, D_MODEL), _F32),
            jax.ShapeDtypeStruct((T_ALL * SUBL, LANES), _F32),
            jax.ShapeDtypeStruct((T_ALL, LANES), jnp.int32),
            jax.ShapeDtypeStruct((T_ALL, LANES), _F32),
            jax.ShapeDtypeStruct((T_ALL, LANES), jnp.int32),
            jax.ShapeDtypeStruct((8, LANES), jnp.int32),
        ],
        compiler_params=_params(1),
        name="merge_router",
    )(modrow, xc, xl, mod3, norm_g, oa, ob, occ, ocl, odc, odl, wg, wbr, wo, wr_hi, wr_lo, br, ltri)


def _expert_kernel(blk_e_ref, nvalid_ref,
                   tok0_ref, tok1_ref, tok2_ref, slotp_ref, slotc_ref,
                   n2_hbm, w1_ref, b1_ref, w2_ref, b2_ref,
                   y_hbm,
                   w1b_ref, w2b_ref, x0_ref, x1_ref, x2_ref, y0_ref, y1_ref, y2_ref, gsem, ssem):
    i = pl.program_id(0)
    nb = pl.num_programs(0)
    nvalid = nvalid_ref[0]
    xbufs = (x0_ref, x1_ref, x2_ref)
    ybufs = (y0_ref, y1_ref, y2_ref)

    def gather_copy(tok, r, p):
        return pltpu.make_async_copy(n2_hbm.at[tok], xbufs[p].at[pl.ds(SUBL * r, SUBL), :],
                                     gsem.at[p])

    def scatter_copy(slot, r, p):
        return pltpu.make_async_copy(ybufs[p].at[pl.ds(SUBL * r, SUBL), :], y_hbm.at[slot],
                                     ssem.at[p])

    def wait_gather(p):
        for r in range(MOE_BLK):
            gather_copy(0, r, p).wait()

    def wait_scatter(p):
        for r in range(MOE_BLK):
            scatter_copy(0, r, p).wait()

    @pl.when(i == 0)
    def _():
        ybufs[RING - 1][...] = jnp.zeros_like(ybufs[RING - 1])
        for r in range(MOE_BLK):
            gather_copy(tok0_ref[0, r], r, 0).start()
            gather_copy(tok1_ref[0, r], r, 1).start()

    prev = blk_e_ref[jnp.maximum(i - 1, 0)]
    fresh = (i == 0) | (blk_e_ref[i] != prev)

    @pl.when(fresh)
    def _():
        w1b_ref[...] = w1_ref[...].astype(_BF16)
        w2b_ref[...] = w2_ref[...].astype(_BF16)

    def valid_step(p):
        ahead = (p + 2) % RING
        behind = (p - 1) % RING
        wait_gather(p)

        @pl.when(i >= 2)
        def _():
            wait_scatter(p)

        for r in range(MOE_BLK):
            gather_copy(tok2_ref[0, r], r, ahead).start(priority=0)
            scatter_copy(slotp_ref[0, r], r, behind).start(priority=1)
        x = _load_row_tiles(xbufs[p], MOE_BLK).astype(_BF16)
        hh = jnp.dot(x, w1b_ref[...], preferred_element_type=_F32) + b1_ref[...]
        x_glu = jnp.minimum(hh[:, 0:D_FF], SWIGLU_LIMIT)
        x_lin = jnp.clip(hh[:, D_FF:2 * D_FF], -SWIGLU_LIMIT, SWIGLU_LIMIT)
        act = x_glu * jax.nn.sigmoid(SWIGLU_ALPHA * x_glu) * (x_lin + 1.0)
        _store_row_tiles(ybufs[p], MOE_BLK,
                         jnp.dot(act.astype(_BF16), w2b_ref[...], preferred_element_type=_F32)
                         + b2_ref[...])

    def tail_step(p):
        behind = (p - 1) % RING
        behind2 = (p - 2) % RING

        @pl.when(i < nvalid + 2)
        def _():
            wait_gather(p)

        wait_scatter(p)
        ybufs[p][...] = jnp.zeros((MOE_BLK * SUBL, LANES), _F32)
        for r in range(MOE_BLK):
            scatter_copy(slotp_ref[0, r], r, behind).start()

        @pl.when(i == nb - 1)
        def _():
            for r in range(MOE_BLK):
                scatter_copy(slotc_ref[0, r], r, p).start()
            wait_scatter(behind2)
            wait_scatter(behind)
            wait_scatter(p)

    is_valid = i < nvalid
    for p in range(RING):
        pl.when(is_valid & (i % RING == p))(functools.partial(valid_step, p))
        pl.when((~is_valid) & (i % RING == p))(functools.partial(tail_step, p))


def _expert_call(l, blk_e, nvalid, tok_of_row, slot_of_row, n2, w1, b1, w2, b2):
    wsel = lambda i, e, n: (l, e[i], 0, 0)
    at = lambda d: (lambda i, e, n: (jnp.minimum(i + d, N_MOE_BLOCKS - 1), 0, 0))
    smem_blk = lambda imap: pl.BlockSpec((None, 1, MOE_BLK), imap, memory_space=pltpu.SMEM)
    tok3 = tok_of_row.reshape(N_MOE_BLOCKS, 1, MOE_BLK)
    virtual = N_MOE_ROWS + jnp.arange(MOE_BLK, dtype=jnp.int32)
    slot3 = jnp.concatenate([virtual, slot_of_row]).reshape(N_MOE_BLOCKS + 1, 1, MOE_BLK)
    row_buf = pltpu.VMEM((MOE_BLK * SUBL, LANES), _F32)
    grid_spec = pltpu.PrefetchScalarGridSpec(
        num_scalar_prefetch=2,
        grid=(N_MOE_BLOCKS,),
        in_specs=[
            smem_blk(at(0)), smem_blk(at(1)), smem_blk(at(2)),
            smem_blk(lambda i, e, n: (i, 0, 0)),
            smem_blk(lambda i, e, n: (i + 1, 0, 0)),
            pl.BlockSpec(memory_space=pl.ANY),
            pl.BlockSpec((None, None, D_MODEL, 2 * D_FF), wsel),
            pl.BlockSpec((None, None, 1, 2 * D_FF), wsel),
            pl.BlockSpec((None, None, D_FF, D_MODEL), wsel),
            pl.BlockSpec((None, None, 1, D_MODEL), wsel),
        ],
        out_specs=pl.BlockSpec(memory_space=pl.ANY),
        scratch_shapes=[pltpu.VMEM((D_MODEL, 2 * D_FF), _BF16),
                        pltpu.VMEM((D_FF, D_MODEL), _BF16)]
                       + [row_buf] * (2 * RING)
                       + [pltpu.SemaphoreType.DMA((RING,)), pltpu.SemaphoreType.DMA((RING,))],
    )
    return pl.pallas_call(
        _expert_kernel,
        grid_spec=grid_spec,
        out_shape=jax.ShapeDtypeStruct((N_MOE_ROWS + MOE_BLK, SUBL, LANES), _F32),
        compiler_params=_params(1),
        name="experts",
    )(blk_e, nvalid, tok3, tok3, tok3, slot3, slot3, n2, w1,
      b1.reshape(DEPTH, N_EXPERTS, 1, 2 * D_FF), w2, b2.reshape(DEPTH, N_EXPERTS, 1, D_MODEL))


def _combine_kernel(modrow_ref, h1_ref, mod_ref, ng_ref, y0_ref, y1_ref, y2_ref, y3_ref, w_ref,
                    oc_ref, ol_ref):
    i = pl.program_id(0)
    mod = mod_ref[...]
    g2 = mod[:, 5 * D_MODEL:6 * D_MODEL]
    w = w_ref[...]
    f = jnp.zeros((TM, D_MODEL), _F32)
    for k, y_ref in enumerate((y0_ref, y1_ref, y2_ref, y3_ref)):
        f = f + _load_row_tiles(y_ref, TM) * w[:, k:k + 1]
    h2 = h1_ref[...] + g2 * _rms(f, ng_ref[3:4, :])

    @pl.when(i < CTX_TILES)
    def _():
        oc_ref[...] = h2

    @pl.when(i >= CTX_TILES)
    def _():
        ol_ref[...] = h2


def _combine_call(h1, mod3, norm_g, yslot, topw, modrow):
    row = lambda i, a: (i, 0)
    choice = lambda k: pl.BlockSpec((TM * SUBL, LANES), lambda i, a: (k * N_TILES + i, 0))
    yslot = yslot.reshape(-1, LANES)
    grid_spec = pltpu.PrefetchScalarGridSpec(
        num_scalar_prefetch=1,
        grid=(N_TILES,),
        in_specs=[
            pl.BlockSpec((TM, D_MODEL), row),
            pl.BlockSpec((None, 1, 6 * D_MODEL), lambda i, a: (a[i], 0, 0)),
            pl.BlockSpec((4, D_MODEL), lambda i, a: (0, 0)),
            choice(0), choice(1), choice(2), choice(3),
            pl.BlockSpec((TM, LANES), row),
        ],
        out_specs=[pl.BlockSpec((TM, D_MODEL), _ctx_rows), pl.BlockSpec((TM, D_MODEL), _lat_rows)],
    )
    return pl.pallas_call(
        _combine_kernel,
        grid_spec=grid_spec,
        out_shape=[jax.ShapeDtypeStruct((T_CTX, D_MODEL), _F32),
                   jax.ShapeDtypeStruct((T_LAT, D_MODEL), _F32)],
        compiler_params=_params(1),
        name="moe_combine",
    )(modrow, h1, mod3, norm_g, yslot, yslot, yslot, yslot, topw)


def _tile_tables():
    t = np.arange(N_TILES)
    lat = t >= CTX_TILES
    j = (t - CTX_TILES) % LAT_TILES_PER_SEQ
    modrow = np.where(lat, 1 + (t - CTX_TILES) // LAT_TILES_PER_SEQ, 0)
    ropeblk = np.where(lat, j, LAT_TILES_PER_SEQ)
    stblk = np.where(lat, CTX_TILES, t)
    hasprev = np.where(lat & (j > 0), 1, 0)
    hasnext = np.where(lat & (j < LAT_TILES_PER_SEQ - 1), 1, 0)
    as_i32 = lambda a: jnp.asarray(a, jnp.int32)
    return as_i32(modrow), as_i32(ropeblk), as_i32(stblk), as_i32(hasprev), as_i32(hasnext)


def _rope_tables():
    f32 = np.float32
    half = HEAD_DIM // 4
    freqs = np.power(f32(ROPE_THETA), -np.arange(half, dtype=f32) / f32(half)).astype(f32)
    t = np.arange(DEC_SEQ)
    pos = np.stack([t // GRID_W, t % GRID_W], axis=1).astype(f32)
    ang = (pos[:, :, None] * freqs[None, None, :]).astype(f32)
    cos = np.cos(ang).astype(f32)
    sin = np.sin(ang).astype(f32)
    cos_h = np.concatenate([cos, cos], axis=-1).reshape(DEC_SEQ, HEAD_DIM)
    sin_h = np.concatenate([-sin, sin], axis=-1).reshape(DEC_SEQ, HEAD_DIM)
    cos_t = np.concatenate([np.tile(cos_h, (1, N_HEADS)), np.ones((TM, BR_WIDTH), f32)], axis=0)
    sin_t = np.concatenate([np.tile(sin_h, (1, N_HEADS)), np.zeros((TM, BR_WIDTH), f32)], axis=0)
    return jnp.asarray(cos_t), jnp.asarray(sin_t)


def _permute_heads_cols(w):
    lead = w.shape[:-1]
    return w.reshape(lead + (N_HEADS, HEAD_DIM))[..., jnp.array([0, 2, 1, 3]), :].reshape(lead + (BR_WIDTH,))


def _cache_rows(cache, l):
    return cache[:, l].reshape(DEC_BATCH, PAST_LEN, KV_WIDTH).astype(_BF16)


PAIR_CHUNK = 4096
SMEM_1D_TILE = 1024
PAIR_ROWS = -(-N_MOE_ROWS // SMEM_1D_TILE) * SMEM_1D_TILE


def _row_pairs_kernel(dest_ref, out_ref):
    step = pl.program_id(0)

    @pl.when(step == 0)
    def _():
        def init(j, carry):
            out_ref[j] = -1
            return carry
        lax.fori_loop(0, PAIR_ROWS, init, 0, unroll=32)

    base = step * PAIR_CHUNK

    def body(j, carry):
        out_ref[dest_ref[0, j]] = base + j
        return carry
    lax.fori_loop(0, PAIR_CHUNK, body, 0, unroll=32)


def _row_pairs_call(dest_flat):
    n_chunks = N_SLOTS // PAIR_CHUNK
    return pl.pallas_call(
        _row_pairs_kernel,
        grid=(n_chunks,),
        in_specs=[pl.BlockSpec((None, 1, PAIR_CHUNK), lambda s: (s, 0, 0), memory_space=pltpu.SMEM)],
        out_specs=pl.BlockSpec(memory_space=pltpu.SMEM),
        out_shape=jax.ShapeDtypeStruct((PAIR_ROWS,), jnp.int32),
        compiler_params=_params(1),
        name="row_pairs",
    )(dest_flat.reshape(n_chunks, 1, PAIR_CHUNK))


def _lookup(table, idx):
    hit = idx[..., None] == jnp.arange(N_EXPERTS, dtype=jnp.int32)
    return jnp.sum(jnp.where(hit, table, 0), axis=-1)


def _moe_layout(cnt, topi, rank):
    counts = cnt[0, :N_EXPERTS]
    padded = (counts + MOE_BLK - 1) // MOE_BLK * MOE_BLK
    pad_end = jnp.cumsum(padded)
    pad_start = pad_end - padded
    blk_row0 = jnp.arange(N_MOE_BLOCKS, dtype=jnp.int32) * MOE_BLK
    blk_e = jnp.minimum(jnp.sum(pad_end[None, :] <= blk_row0[:, None], axis=1),
                        N_EXPERTS - 1).astype(jnp.int32)
    nvalid = (pad_end[-1:] // MOE_BLK).astype(jnp.int32)
    spare_before = jnp.cumsum(padded - counts) - (padded - counts)
    row = jnp.arange(N_MOE_ROWS, dtype=jnp.int32).reshape(N_MOE_BLOCKS, MOE_BLK)
    in_tail = (blk_row0 >= pad_end[-1])[:, None]
    spare_in_group = row + _lookup(spare_before - pad_start - counts, blk_e)[:, None]
    spare_tail = row - pad_end[-1] + jnp.sum(padded - counts)
    spare = N_SLOTS + jnp.where(in_tail, spare_tail, spare_in_group)
    dest = _lookup(pad_start, topi[:, :TOP_K]) + rank[:, :TOP_K]
    pair_of_row = _row_pairs_call(dest.reshape(-1))[:N_MOE_ROWS]
    is_pair = pair_of_row >= 0
    tok_of_row = jnp.where(is_pair, pair_of_row // TOP_K, 0)
    slot_of_row = jnp.where(is_pair, (pair_of_row % TOP_K) * T_ALL + tok_of_row,
                            spare.reshape(-1).astype(jnp.int32))
    return blk_e, nvalid, tok_of_row, slot_of_row


def kernel(x_prompt, x_sample, cache_win_k, cache_win_v, cache_full_k, cache_full_v, c, c_ctx,
           w_mod, b_mod, norm_g, w_in, a_ln_g, a_ln_b, w_sp, b_sp, conv_w, conv_b, b_ln_g, b_ln_b,
           sink, qn_g, kn_g, w_br, w_o, w_router, b_router, w1, b1, w2, b2):
    modrow, ropeblk, stblk, hasprev, hasnext = _tile_tables()
    cos_t, sin_t = _rope_tables()
    head_id = np.arange(BR_WIDTH) // HEAD_DIM
    ones_bd = jnp.asarray(head_id[:, None] == head_id[None, :], _BF16)
    ltri = jnp.asarray(np.tril(np.ones((TM, TM)), -1), _BF16)
    row2 = lambda v: v.reshape(1, -1)

    cvec = jnp.concatenate([c_ctx[None, :], c, jnp.zeros((8 - 1 - DEC_BATCH, D_MODEL), _F32)], axis=0)
    mod_all = _mod_call(cvec, w_mod, b_mod)

    hc = x_prompt.reshape(T_CTX, D_MODEL)
    hl = x_sample.reshape(T_LAT, D_MODEL)
    states = []
    for l in range(DEPTH):
        mod3 = mod_all[l].reshape(8, 1, 6 * D_MODEL)
        wl = w_in[l]
        w_main = jnp.concatenate([
            wl[:, 0:1024],
            _permute_heads_cols(wl[:, 1024:1280]), wl[:, 1280:1536],
            _permute_heads_cols(wl[:, 1536:1792]), wl[:, 1792:2048]], axis=1).astype(_BF16)
        wg = wl[:, MAIN_WIDTH:].astype(_BF16)
        qn = jnp.tile(qn_g[l], N_HEADS).reshape(1, BR_WIDTH)
        kn = jnp.tile(kn_g[l], KV_HEADS).reshape(1, KV_WIDTH)

        ab, qc, kc, vc, qd, kd, vd, st = _inproj_call(
            hc, hl, mod3, row2(norm_g[l, 0]), w_main, qn, kn, ones_bd, cos_t, sin_t,
            (modrow, ropeblk, stblk))
        states.append(st[:T_CTX].reshape(BATCH, SEQ, 4, KV_HEADS, HEAD_DIM))

        bsp = jnp.repeat(b_sp[l].T, BR_WIDTH // A_GROUPS, axis=1)
        cw = jnp.concatenate([conv_w[l], jnp.zeros((1, BR_WIDTH), _F32)], axis=0)
        oa, ob = _mix_call(ab, row2(a_ln_g[l]), row2(a_ln_b[l]), w_sp[l].astype(_BF16), bsp, cw,
                           row2(conv_b[l]), row2(b_ln_g[l]), row2(b_ln_b[l]), (hasprev, hasnext))

        occ, odc = _ctx_attn_call(sink[l], qc, kc, vc, qd, kd, vd)
        ocl = _win_attn_call(sink[l], qc, kc, vc, _cache_rows(cache_win_k, l),
                             _cache_rows(cache_win_v, l))
        odl = _full_attn_call(qd, kd, vd, _cache_rows(cache_full_k, l),
                              _cache_rows(cache_full_v, l))

        perm = jnp.array([0, 2, 1, 3])
        wbr = w_br[l].reshape(N_BRANCH, N_HEADS, HEAD_DIM, D_MODEL)
        wbr = jnp.concatenate([wbr[0:2], wbr[2:4][:, perm]], axis=0)
        wbr = wbr.reshape(N_BRANCH, BR_WIDTH, D_MODEL).astype(_BF16)
        wr = jnp.concatenate([w_router[l], jnp.zeros((D_MODEL, LANES - N_EXPERTS), _F32)], axis=1)
        wr_hi, wr_lo = _hi_lo(wr)
        br = jnp.concatenate([b_router[l], jnp.full((LANES - N_EXPERTS,), NEG_BIG, _F32)]).reshape(1, LANES)
        h1, n2, topi, topw, rank, cnt = _merge_call(
            hc, hl, mod3, norm_g[l], oa, ob, occ, ocl, odc, odl, wg, wbr, w_o[l].astype(_BF16),
            wr_hi, wr_lo, br, ltri, modrow)

        blk_e, nvalid, tok_of_row, slot_of_row = _moe_layout(cnt, topi, rank)
        yslot = _expert_call(l, blk_e, nvalid, tok_of_row, slot_of_row,
                             n2.reshape(T_ALL, SUBL, LANES), w1, b1, w2, b2)
        hc, hl = _combine_call(h1, mod3, norm_g[l], yslot, topw, modrow)

    y_prompt = hc.reshape(BATCH, SEQ, D_MODEL)
    y_sample = hl.reshape(DEC_BATCH, DEC_SEQ, D_MODEL)
    st = jnp.stack(states, axis=1)
    return (y_prompt, y_sample, st[:, :, :, 0], st[:, :, :, 1], st[:, :, :, 2], st[:, :, :, 3])
```

```python
import functools

import numpy as np
import jax
import jax.numpy as jnp
from jax import lax
from jax.experimental import pallas as pl
from jax.experimental.pallas import tpu as pltpu

D_MODEL = 1024
BATCH = 16
SEQ = 256
DEPTH = 2
DEC_BATCH = 4
DEC_SEQ = 4096
PAST_LEN = 256
GRID_W = 64
HEAD_DIM = 64
BR_WIDTH = 256
N_BRANCH = 4
A_GROUPS = 4
CHUNK = 128
CONV_W = 31
N_HEADS = 4
KV_HEADS = 2
KV_WIDTH = KV_HEADS * HEAD_DIM
WINDOW = 128
ROPE_THETA = 10000.0
EPS = 1e-6
N_EXPERTS = 32
TOP_K = 4
D_FF = D_MODEL
SWIGLU_LIMIT = 7.0
SWIGLU_ALPHA = 1.702
MAIN_WIDTH = 2048

LANES = 128
T_CTX = BATCH * SEQ
T_LAT = DEC_BATCH * DEC_SEQ
T_ALL = T_CTX + T_LAT
TM = 256
MERGE_SUB = 2
N_TILES = T_ALL // TM
CTX_TILES = T_CTX // TM
LAT_TILES_PER_SEQ = DEC_SEQ // TM
HALO = 16
SUBLANES = 8
SHIFT_ROWS = TM + (HALO + CONV_W // 2) // SUBLANES * SUBLANES
MOE_BLK = 256
N_SLOTS = T_ALL * TOP_K
RING = 3
N_MOE_BLOCKS = N_SLOTS // MOE_BLK + N_EXPERTS + 1
N_MOE_ROWS = N_MOE_BLOCKS * MOE_BLK
WIN_TQ = 512
FULL_TQ = 256
FULL_TK = 512
NEG_BIG = -1e30
LOG2E = 1.4426950408889634
VMEM_LIMIT = 56 * 1024 * 1024

_F32 = jnp.float32
_BF16 = jnp.bfloat16


def _params(n_axes, vmem=None):
    return pltpu.CompilerParams(
        dimension_semantics=("arbitrary",) * n_axes,
        vmem_limit_bytes=vmem if vmem is not None else VMEM_LIMIT)


def _split_dot(a, b_hi, b_lo):
    a_hi = a.astype(_BF16)
    a_lo = (a - a_hi.astype(_F32)).astype(_BF16)
    return (jnp.dot(a_hi, b_hi, preferred_element_type=_F32)
            + jnp.dot(a_lo, b_hi, preferred_element_type=_F32)
            + jnp.dot(a_hi, b_lo, preferred_element_type=_F32))


def _hi_lo(w):
    hi = w.astype(_BF16)
    lo = (w - hi.astype(_F32)).astype(_BF16)
    return hi, lo


def _mod_kernel(c_ref, w_ref, b_ref, o_ref):
    c = c_ref[...]
    s = c * jax.nn.sigmoid(c)
    w = w_ref[...]
    w_hi = w.astype(_BF16)
    w_lo = (w - w_hi.astype(_F32)).astype(_BF16)
    o_ref[...] = _split_dot(s, w_hi, w_lo) + b_ref[...]


def _mod_call(cvec, w_mod, b_mod):
    tn = 1024
    return pl.pallas_call(
        _mod_kernel,
        grid=(DEPTH, 6 * D_MODEL // tn),
        in_specs=[
            pl.BlockSpec((8, D_MODEL), lambda l, j: (0, 0)),
            pl.BlockSpec((None, D_MODEL, tn), lambda l, j: (l, 0, j)),
            pl.BlockSpec((None, 1, tn), lambda l, j: (l, 0, j)),
        ],
        out_specs=pl.BlockSpec((None, 8, tn), lambda l, j: (l, 0, j)),
        out_shape=jax.ShapeDtypeStruct((DEPTH, 8, 6 * D_MODEL), _F32),
        compiler_params=_params(2),
        name="mod",
    )(cvec, w_mod, b_mod.reshape(DEPTH, 1, 6 * D_MODEL))


def _rms(x, g):
    ms = jnp.mean(x * x, axis=-1, keepdims=True)
    return x * lax.rsqrt(ms + EPS) * g


def _head_rms(x, ones_bd, g):
    xx = x * x
    hi = xx.astype(_BF16)
    lo = (xx - hi.astype(_F32)).astype(_BF16)
    ss = (jnp.dot(hi, ones_bd, preferred_element_type=_F32)
          + jnp.dot(lo, ones_bd, preferred_element_type=_F32))
    return x * lax.rsqrt(ss * (1.0 / HEAD_DIM) + EPS) * g


def _rope(x, cos, sin_signed):
    w = x.shape[-1]
    lane = lax.broadcasted_iota(jnp.int32, x.shape, 1)
    first = (lane % 32) < 16
    partner = jnp.where(first, pltpu.roll(x, w - 16, 1), pltpu.roll(x, 16, 1))
    return x * cos + partner * sin_signed


def _inproj_kernel(modrow_ref, ropeblk_ref, stblk_ref,
                   xc_ref, xl_ref, mod_ref, g0_ref, w_ref, qn_ref, kn_ref, ones_ref,
                   cos_ref, sin_ref,
                   ab_ref, qc_ref, kc_ref, vc_ref, qd_ref, kd_ref, vd_ref, st_ref):
    x = jnp.where(pl.program_id(0) < CTX_TILES, xc_ref[...], xl_ref[...])
    mod = mod_ref[...]
    sh1 = mod[:, 0:D_MODEL]
    sc1 = mod[:, D_MODEL:2 * D_MODEL]
    n = _rms(x, g0_ref[...]) * (1.0 + sc1) + sh1
    z = jnp.dot(n.astype(_BF16), w_ref[...], preferred_element_type=_F32)
    ab_ref[...] = z[:, 0:1024]
    cq = z[:, 1024:1280]
    ck = z[:, 1280:1408]
    cv = z[:, 1408:1536]
    dq = z[:, 1536:1792]
    dk = z[:, 1792:1920]
    dv = z[:, 1920:2048]
    ones = ones_ref[...]
    dq = _head_rms(dq, ones, qn_ref[...])
    dk = _head_rms(dk, ones[0:KV_WIDTH, 0:KV_WIDTH], kn_ref[...])
    cos = cos_ref[...]
    sin = sin_ref[...]
    scale = HEAD_DIM ** -0.5 * LOG2E
    qc_ref[...] = (_rope(cq, cos, sin) * scale).astype(_BF16)
    qd_ref[...] = (_rope(dq, cos, sin) * scale).astype(_BF16)
    kc = _rope(ck, cos[:, 0:KV_WIDTH], sin[:, 0:KV_WIDTH])
    kd = _rope(dk, cos[:, 0:KV_WIDTH], sin[:, 0:KV_WIDTH])
    kc_ref[...] = kc.astype(_BF16)
    kd_ref[...] = kd.astype(_BF16)
    vc_ref[...] = cv.astype(_BF16)
    vd_ref[...] = dv.astype(_BF16)
    st_ref[:, 0:128] = kc
    st_ref[:, 128:256] = cv
    st_ref[:, 256:384] = kd
    st_ref[:, 384:512] = dv


def _ctx_rows(*_):
    i = _[0]
    return (jnp.minimum(i, CTX_TILES - 1), 0)


def _lat_rows(*_):
    i = _[0]
    return (jnp.maximum(i - CTX_TILES, 0), 0)


def _inproj_call(xc, xl, mod3, g0, w_main, qn, kn, ones_bd, cos_t, sin_t, tables):
    modrow, ropeblk, stblk = tables
    row = lambda i, a, b, c: (i, 0)
    full = lambda i, a, b, c: (0, 0)
    bf = lambda w: jax.ShapeDtypeStruct((T_ALL, w), _BF16)
    grid_spec = pltpu.PrefetchScalarGridSpec(
        num_scalar_prefetch=3,
        grid=(N_TILES,),
        in_specs=[
            pl.BlockSpec((TM, D_MODEL), _ctx_rows),
            pl.BlockSpec((TM, D_MODEL), _lat_rows),
            pl.BlockSpec((None, 1, 6 * D_MODEL), lambda i, a, b, c: (a[i], 0, 0)),
            pl.BlockSpec((1, D_MODEL), full),
            pl.BlockSpec((D_MODEL, MAIN_WIDTH), full),
            pl.BlockSpec((1, BR_WIDTH), full),
            pl.BlockSpec((1, KV_WIDTH), full),
            pl.BlockSpec((BR_WIDTH, BR_WIDTH), full),
            pl.BlockSpec((TM, BR_WIDTH), lambda i, a, b, c: (b[i], 0)),
            pl.BlockSpec((TM, BR_WIDTH), lambda i, a, b, c: (b[i], 0)),
        ],
        out_specs=[
            pl.BlockSpec((TM, 1024), row),
            pl.BlockSpec((TM, BR_WIDTH), row),
            pl.BlockSpec((TM, KV_WIDTH), row),
            pl.BlockSpec((TM, KV_WIDTH), row),
            pl.BlockSpec((TM, BR_WIDTH), row),
            pl.BlockSpec((TM, KV_WIDTH), row),
            pl.BlockSpec((TM, KV_WIDTH), row),
            pl.BlockSpec((TM, 512), lambda i, a, b, c: (c[i], 0)),
        ],
    )
    return pl.pallas_call(
        _inproj_kernel,
        grid_spec=grid_spec,
        out_shape=[
            jax.ShapeDtypeStruct((T_ALL, 1024), _F32),
            bf(BR_WIDTH), bf(KV_WIDTH), bf(KV_WIDTH),
            bf(BR_WIDTH), bf(KV_WIDTH), bf(KV_WIDTH),
            jax.ShapeDtypeStruct((T_CTX + TM, 512), _F32),
        ],
        compiler_params=_params(1),
        name="inproj",
    )(modrow, ropeblk, stblk, xc, xl, mod3, g0, w_main, qn, kn, ones_bd, cos_t, sin_t)


def _layernorm(x, g, b):
    mu = jnp.mean(x, axis=-1, keepdims=True)
    xc = x - mu
    var = jnp.mean(xc * xc, axis=-1, keepdims=True)
    return xc * lax.rsqrt(var + EPS) * g + b


def _gelu(x):
    return 0.5 * x * (1.0 + lax.erf(x * (2.0 ** -0.5)))


def _mix_kernel(hasprev_ref, hasnext_ref,
                ab_ref, prev_ref, next_ref, alg_ref, alb_ref, wsp_ref, bsp_ref,
                cw_ref, cb_ref, blg_ref, blb_ref,
                oa_ref, ob_ref, buf_ref, sh_ref):
    i = pl.program_id(0)
    ab = ab_ref[...]
    u = _gelu(ab[:, 0:256])
    va = _layernorm(_gelu(ab[:, 256:512]), alg_ref[...], alb_ref[...])
    vab = va.astype(_BF16)
    lane = lax.broadcasted_iota(jnp.int32, (CHUNK, BR_WIDTH), 1)
    group = lane // (BR_WIDTH // A_GROUPS)
    bsp = bsp_ref[...]
    for c in range(TM // CHUNK):
        v_c = vab[c * CHUNK:(c + 1) * CHUNK, :]
        sp = bsp
        for g in range(A_GROUPS):
            full = jnp.dot(wsp_ref[g], v_c, preferred_element_type=_F32)
            sp = sp + jnp.where(group == g, full, 0.0)
        oa_ref[c * CHUNK:(c + 1) * CHUNK, :] = (u[c * CHUNK:(c + 1) * CHUNK, :] * sp).astype(_BF16)
    glu = ab[:, 512:768] * jax.nn.sigmoid(ab[:, 768:1024])
    pv = prev_ref[...]
    nx = next_ref[...]
    hp = hasprev_ref[i].astype(_F32)
    hn = hasnext_ref[i].astype(_F32)
    buf_ref[0:HALO, :] = pv[:, 0:256] * jax.nn.sigmoid(pv[:, 256:512]) * hp
    buf_ref[HALO:HALO + TM, :] = glu
    buf_ref[HALO + TM:HALO + TM + HALO, :] = nx[:, 0:256] * jax.nn.sigmoid(nx[:, 256:512]) * hn
    for ph in range(1, SUBLANES):
        sh_ref[ph - 1] = buf_ref[ph:ph + SHIFT_ROWS, :]
    cw = cw_ref[...]
    acc = jnp.zeros((TM, BR_WIDTH), _F32) + cb_ref[...]
    off = HALO - CONV_W // 2
    for k in range(CONV_W):
        start, ph = (off + k) // SUBLANES * SUBLANES, (off + k) % SUBLANES
        src = buf_ref if ph == 0 else sh_ref.at[ph - 1]
        acc = acc + src[start:start + TM, :] * cw[k:k + 1, :]
    y = _layernorm(acc, blg_ref[...], blb_ref[...])
    ob_ref[...] = (y * jax.nn.sigmoid(y)).astype(_BF16)


def _mix_call(ab, a_ln_g, a_ln_b, wsp, bsp, conv_w, conv_b, b_ln_g, b_ln_b, tables):
    hasprev, hasnext = tables
    n_halo_blocks = T_ALL // HALO
    per = TM // HALO
    full2 = lambda i, a, b: (0, 0)
    grid_spec = pltpu.PrefetchScalarGridSpec(
        num_scalar_prefetch=2,
        grid=(N_TILES,),
        in_specs=[
            pl.BlockSpec((TM, 1024), lambda i, a, b: (i, 0)),
            pl.BlockSpec((HALO, 512), lambda i, a, b: (jnp.maximum(i * per - 1, 0), 1)),
            pl.BlockSpec((HALO, 512), lambda i, a, b: (jnp.minimum((i + 1) * per, n_halo_blocks - 1), 1)),
            pl.BlockSpec((1, BR_WIDTH), full2),
            pl.BlockSpec((1, BR_WIDTH), full2),
            pl.BlockSpec((A_GROUPS, CHUNK, CHUNK), lambda i, a, b: (0, 0, 0)),
            pl.BlockSpec((CHUNK, BR_WIDTH), full2),
            pl.BlockSpec((32, BR_WIDTH), full2),
            pl.BlockSpec((1, BR_WIDTH), full2),
            pl.BlockSpec((1, BR_WIDTH), full2),
            pl.BlockSpec((1, BR_WIDTH), full2),
        ],
        out_specs=[
            pl.BlockSpec((TM, BR_WIDTH), lambda i, a, b: (i, 0)),
            pl.BlockSpec((TM, BR_WIDTH), lambda i, a, b: (i, 0)),
        ],
        scratch_shapes=[pltpu.VMEM((TM + 2 * HALO, BR_WIDTH), _F32),
                        pltpu.VMEM((SUBLANES - 1, SHIFT_ROWS, BR_WIDTH), _F32)],
    )
    return pl.pallas_call(
        _mix_kernel,
        grid_spec=grid_spec,
        out_shape=[jax.ShapeDtypeStruct((T_ALL, BR_WIDTH), _BF16)] * 2,
        compiler_params=_params(1),
        name="mixers_ab",
    )(hasprev, hasnext, ab, ab, ab, a_ln_g, a_ln_b, wsp, bsp, conv_w, conv_b, b_ln_g, b_ln_b)


def _lane_masks(n_rows):
    lane = lax.broadcasted_iota(jnp.int32, (n_rows, LANES), 1)
    return lane < HEAD_DIM, lane >= HEAD_DIM


def _group_queries(q, kv):
    tq = q.shape[0]
    lo, hi = _lane_masks(tq)
    keep = lo if kv == 0 else hi
    zero = jnp.zeros((), q.dtype)
    return jnp.concatenate([jnp.where(keep, q[:, 0:LANES], zero),
                            jnp.where(keep, q[:, LANES:2 * LANES], zero)], axis=0)


def _scores(qs, k):
    return lax.dot_general(qs, k, (((1,), (1,)), ((), ())), preferred_element_type=_F32)


def _attend_once(q, chunks, sinks):
    tq = q.shape[0]
    lo, hi = _lane_masks(tq)
    outs = [jnp.zeros((tq, LANES), _F32), jnp.zeros((tq, LANES), _F32)]
    for kv in range(KV_HEADS):
        qs = _group_queries(q, kv)
        ss = []
        for k, _, mask in chunks:
            s = _scores(qs, k)
            if mask is not None:
                s = jnp.where(mask, s, -jnp.inf)
            ss.append(s)
        m = ss[0].max(axis=-1, keepdims=True)
        for s in ss[1:]:
            m = jnp.maximum(m, s.max(axis=-1, keepdims=True))
        if sinks is not None:
            row = lax.broadcasted_iota(jnp.int32, (2 * tq, 1), 0)
            sink_col = jnp.where(row < tq, sinks[2 * kv], sinks[2 * kv + 1])
            m = jnp.maximum(m, sink_col)
            l = jnp.exp2(sink_col - m)
        else:
            l = jnp.zeros((2 * tq, 1), _F32)
        acc = jnp.zeros((2 * tq, LANES), _F32)
        for s, (_, v, _) in zip(ss, chunks):
            p = jnp.exp2(s - m)
            l = l + p.sum(axis=-1, keepdims=True)
            acc = acc + jnp.dot(p.astype(_BF16), v, preferred_element_type=_F32)
        o = acc / l
        keep = lo if kv == 0 else hi
        outs[0] = outs[0] + jnp.where(keep, o[0:tq], 0.0)
        outs[1] = outs[1] + jnp.where(keep, o[tq:2 * tq], 0.0)
    return jnp.concatenate(outs, axis=1)


def _ctx_attn_kernel(sink_ref, qc_ref, kc_ref, vc_ref, qd_ref, kd_ref, vd_ref, oc_ref, od_ref):
    sinks = [sink_ref[h] * LOG2E for h in range(N_HEADS)]
    oc_ref[...] = _attend_once(qc_ref[...], [(kc_ref[...], vc_ref[...], None)], sinks).astype(_BF16)
    od_ref[...] = _attend_once(qd_ref[...], [(kd_ref[...], vd_ref[...], None)], None).astype(_BF16)


def _ctx_attn_call(sink, qc, kc, vc, qd, kd, vd):
    seq = lambda b: (b, 0)
    q_spec = pl.BlockSpec((SEQ, BR_WIDTH), seq)
    kv_spec = pl.BlockSpec((SEQ, KV_WIDTH), seq)
    return pl.pallas_call(
        _ctx_attn_kernel,
        grid=(BATCH,),
        in_specs=[pl.BlockSpec(memory_space=pltpu.SMEM),
                  q_spec, kv_spec, kv_spec, q_spec, kv_spec, kv_spec],
        out_specs=[q_spec, q_spec],
        out_shape=[jax.ShapeDtypeStruct((T_CTX, BR_WIDTH), _BF16)] * 2,
        compiler_params=_params(1),
        name="ctx_attention",
    )(sink, qc, kc, vc, qd, kd, vd)


def _win_attn_kernel(sink_ref, q_ref, kcur_ref, kprev_ref, knext_ref, vcur_ref, vprev_ref,
                     vnext_ref, kctx_ref, vctx_ref, oc_ref):
    j = pl.program_id(1)
    nq = pl.num_programs(1)
    sinks = [sink_ref[h] * LOG2E for h in range(N_HEADS)]
    tq = WIN_TQ
    def rel(n_keys):
        qi = lax.broadcasted_iota(jnp.int32, (2 * tq, n_keys), 0) % tq
        kj = lax.broadcasted_iota(jnp.int32, (2 * tq, n_keys), 1)
        return qi, kj
    qi, kj = rel(tq)
    m_cur = jnp.abs(qi - kj) <= WINDOW
    qi, kj = rel(WINDOW)
    m_prev = ((qi - (kj - WINDOW)) <= WINDOW) & (j > 0)
    m_next = (((kj + tq) - qi) <= WINDOW) & (j < nq - 1)
    chunks = [
        (kcur_ref[...], vcur_ref[...], m_cur),
        (kprev_ref[...], vprev_ref[...], m_prev),
        (knext_ref[...], vnext_ref[...], m_next),
        (kctx_ref[...], vctx_ref[...], None),
    ]
    oc_ref[...] = _attend_once(q_ref[...], chunks, sinks).astype(_BF16)


def _win_attn_call(sink, qc, kc, vc, kctx, vctx):
    nq = DEC_SEQ // WIN_TQ
    ctx_q = T_CTX // WIN_TQ
    sub = WIN_TQ // WINDOW
    ctx_w = T_CTX // WINDOW
    per_seq_w = DEC_SEQ // WINDOW
    cur = lambda b, j: (ctx_q + b * nq + j, 0)
    prev = lambda b, j: (ctx_w + b * per_seq_w + jnp.maximum(j * sub - 1, 0), 0)
    nxt = lambda b, j: (ctx_w + b * per_seq_w + jnp.minimum((j + 1) * sub, per_seq_w - 1), 0)
    cache = lambda b, j: (b, 0, 0)
    return pl.pallas_call(
        _win_attn_kernel,
        grid=(DEC_BATCH, nq),
        in_specs=[
            pl.BlockSpec(memory_space=pltpu.SMEM),
            pl.BlockSpec((WIN_TQ, BR_WIDTH), cur),
            pl.BlockSpec((WIN_TQ, KV_WIDTH), cur),
            pl.BlockSpec((WINDOW, KV_WIDTH), prev),
            pl.BlockSpec((WINDOW, KV_WIDTH), nxt),
            pl.BlockSpec((WIN_TQ, KV_WIDTH), cur),
            pl.BlockSpec((WINDOW, KV_WIDTH), prev),
            pl.BlockSpec((WINDOW, KV_WIDTH), nxt),
            pl.BlockSpec((None, PAST_LEN, KV_WIDTH), cache),
            pl.BlockSpec((None, PAST_LEN, KV_WIDTH), cache),
        ],
        out_specs=pl.BlockSpec((WIN_TQ, BR_WIDTH), lambda b, j: (b * nq + j, 0)),
        out_shape=jax.ShapeDtypeStruct((T_LAT, BR_WIDTH), _BF16),
        compiler_params=_params(2),
        name="window_attention",
    )(sink, qc, kc, kc, kc, vc, vc, vc, kctx, vctx)


def _full_attn_kernel(q_ref, k_ref, v_ref, kctx_ref, vctx_ref, od_ref, s_ref, sctx_ref):
    tq = FULL_TQ
    n_chunks = DEC_SEQ // FULL_TK
    q = q_ref[...]
    lo, hi = _lane_masks(tq)
    qss = [_group_queries(q, kv) for kv in range(KV_HEADS)]

    def sweep_max(c, ms):
        start = pl.multiple_of(c * FULL_TK, FULL_TK)
        k = k_ref[pl.ds(start, FULL_TK), :]
        new = []
        for kv in range(KV_HEADS):
            s = _scores(qss[kv], k)
            s_ref[c, kv] = s
            new.append(jnp.maximum(ms[kv], s.max(axis=-1, keepdims=True)))
        return tuple(new)

    neg = jnp.full((2 * tq, 1), -jnp.inf, _F32)
    ms = lax.fori_loop(0, n_chunks, sweep_max, (neg, neg), unroll=True)
    kctx = kctx_ref[...]
    ms = list(ms)
    for kv in range(KV_HEADS):
        s = _scores(qss[kv], kctx)
        sctx_ref[kv] = s
        ms[kv] = jnp.maximum(ms[kv], s.max(axis=-1, keepdims=True))

    def accumulate(carry, kv, s, v):
        l, acc = carry
        p = jnp.exp2(s - ms[kv])
        return (l + p.sum(axis=-1, keepdims=True),
                acc + jnp.dot(p.astype(_BF16), v, preferred_element_type=_F32))

    def sweep_sum(c, carries):
        start = pl.multiple_of(c * FULL_TK, FULL_TK)
        v = v_ref[pl.ds(start, FULL_TK), :]
        return tuple(accumulate(carries[kv], kv, s_ref[c, kv], v) for kv in range(KV_HEADS))

    zero = (jnp.zeros((2 * tq, 1), _F32), jnp.zeros((2 * tq, LANES), _F32))
    carries = lax.fori_loop(0, n_chunks, sweep_sum, (zero, zero), unroll=4)
    vctx = vctx_ref[...]
    carries = [accumulate(carries[kv], kv, sctx_ref[kv], vctx) for kv in range(KV_HEADS)]
    outs = [jnp.zeros((tq, LANES), _F32), jnp.zeros((tq, LANES), _F32)]
    for kv, (l, acc) in enumerate(carries):
        o = acc / l
        keep = lo if kv == 0 else hi
        outs[0] = outs[0] + jnp.where(keep, o[0:tq], 0.0)
        outs[1] = outs[1] + jnp.where(keep, o[tq:2 * tq], 0.0)
    od_ref[...] = jnp.concatenate(outs, axis=1).astype(_BF16)


def _full_attn_call(qd, kd, vd, kctx, vctx):
    nq = DEC_SEQ // FULL_TQ
    ctx_q = T_CTX // FULL_TQ
    ctx_seq = T_CTX // DEC_SEQ
    cur = lambda b, j: (ctx_q + b * nq + j, 0)
    seq = lambda b, j: (ctx_seq + b, 0)
    cache = lambda b, j: (b, 0, 0)
    return pl.pallas_call(
        _full_attn_kernel,
        grid=(DEC_BATCH, nq),
        in_specs=[
            pl.BlockSpec((FULL_TQ, BR_WIDTH), cur),
            pl.BlockSpec((DEC_SEQ, KV_WIDTH), seq),
            pl.BlockSpec((DEC_SEQ, KV_WIDTH), seq),
            pl.BlockSpec((None, PAST_LEN, KV_WIDTH), cache),
            pl.BlockSpec((None, PAST_LEN, KV_WIDTH), cache),
        ],
        out_specs=pl.BlockSpec((FULL_TQ, BR_WIDTH), lambda b, j: (b * nq + j, 0)),
        out_shape=jax.ShapeDtypeStruct((T_LAT, BR_WIDTH), _BF16),
        scratch_shapes=[pltpu.VMEM((DEC_SEQ // FULL_TK, KV_HEADS, 2 * FULL_TQ, FULL_TK), _F32),
                        pltpu.VMEM((KV_HEADS, 2 * FULL_TQ, PAST_LEN), _F32)],
        compiler_params=_params(2),
        name="full_attention",
    )(qd, kd, vd, kctx, vctx)


SUBL = D_MODEL // LANES


def _store_row_tiles(ref, n, x, base=0):
    for c in range(SUBL):
        ref[pl.ds(base + c, n, stride=SUBL), :] = x[:, c * LANES:(c + 1) * LANES]


def _load_row_tiles(ref, n):
    return jnp.concatenate([ref[pl.ds(c, n, stride=SUBL), :] for c in range(SUBL)], axis=1)


def _merge_kernel(modrow_ref,
                  xc_ref, xl_ref, mod_ref, ng_ref, oa_ref, ob_ref, occ_ref, ocl_ref, odc_ref, odl_ref,
                  wg_ref, wbr_ref, wo_ref, wrh_ref, wrl_ref, br_ref, ltri_ref,
                  h1_ref, n2_ref, topi_ref, topw_ref, rank_ref, cnt_ref, base_ref):
    i = pl.program_id(0)

    @pl.when(i == 0)
    def _():
        base_ref[...] = jnp.zeros_like(base_ref)

    is_ctx = i < CTX_TILES // MERGE_SUB
    mod = mod_ref[...]
    sh1 = mod[:, 0:D_MODEL]
    sc1 = mod[:, D_MODEL:2 * D_MODEL]
    g1 = mod[:, 2 * D_MODEL:3 * D_MODEL]
    sh2 = mod[:, 3 * D_MODEL:4 * D_MODEL]
    sc2 = mod[:, 4 * D_MODEL:5 * D_MODEL]
    ng = ng_ref[...]
    lane = lax.broadcasted_iota(jnp.int32, (TM, LANES), 1).astype(_F32)
    ltri = ltri_ref[...]
    running = base_ref[0:1, :]
    for sub in range(MERGE_SUB):
        rows = pl.ds(sub * TM, TM)
        x = jnp.where(is_ctx, xc_ref[rows, :], xl_ref[rows, :])
        nb = (_rms(x, ng[0:1, :]) * (1.0 + sc1) + sh1).astype(_BF16)
        oc = jnp.where(is_ctx, occ_ref[rows, :], ocl_ref[rows, :])
        od = jnp.where(is_ctx, odc_ref[rows, :], odl_ref[rows, :])
        branches = (oa_ref[rows, :], ob_ref[rows, :], oc, od)
        mixed = jnp.zeros((TM, D_MODEL), _F32)
        for k in range(N_BRANCH):
            gate = jax.nn.sigmoid(jnp.dot(nb, wg_ref[:, k * D_MODEL:(k + 1) * D_MODEL],
                                          preferred_element_type=_F32))
            proj = jnp.dot(branches[k], wbr_ref[k], preferred_element_type=_F32)
            mixed = mixed + gate * proj
        mix = jnp.dot(mixed.astype(_BF16), wo_ref[...], preferred_element_type=_F32)
        h1 = x + g1 * _rms(mix, ng[1:2, :])
        h1_ref[rows, :] = h1
        n2 = _rms(h1, ng[2:3, :]) * (1.0 + sc2) + sh2
        _store_row_tiles(n2_ref, TM, n2, base=sub * TM * SUBL)
        logits = _split_dot(n2, wrh_ref[...], wrl_ref[...]) + br_ref[...]
        work = logits
        vals, idxs = [], []
        for _ in range(TOP_K):
            m = work.max(axis=-1, keepdims=True)
            idx = jnp.where(work == m, lane, float(LANES)).min(axis=-1, keepdims=True)
            vals.append(m)
            idxs.append(idx)
            work = jnp.where(lane == idx, -jnp.inf, work)
        es = [jnp.exp(v - vals[0]) for v in vals]
        denom = es[0] + es[1] + es[2] + es[3]
        topi = jnp.zeros((TM, LANES), _F32)
        topw = jnp.zeros((TM, LANES), _F32)
        rank = jnp.zeros((TM, LANES), _F32)
        for k in range(TOP_K):
            onehot = (lane == idxs[k]).astype(_F32)
            before = jnp.dot(ltri, onehot.astype(_BF16), preferred_element_type=_F32)
            r = (onehot * (before + running)).sum(axis=-1, keepdims=True)
            running = running + onehot.sum(axis=0, keepdims=True)
            topi = jnp.where(lane == k, idxs[k], topi)
            topw = jnp.where(lane == k, es[k] / denom, topw)
            rank = jnp.where(lane == k, r, rank)
        topi_ref[rows, :] = topi.astype(jnp.int32)
        topw_ref[rows, :] = topw
        rank_ref[rows, :] = rank.astype(jnp.int32)
    base_ref[0:1, :] = running
    cnt_ref[...] = jnp.broadcast_to(running, (8, LANES)).astype(jnp.int32)


def _merge_call(xc, xl, mod3, norm_g, oa, ob, occ, ocl, odc, odl, wg, wbr, wo, wr_hi, wr_lo, br, ltri,
                modrow):
    mt = TM * MERGE_SUB
    n_ctx = CTX_TILES // MERGE_SUB
    row = lambda i, a: (i, 0)
    full = lambda i, a: (0, 0)
    ctx_rows = lambda i, a: (jnp.minimum(i, n_ctx - 1), 0)
    lat_rows = lambda i, a: (jnp.maximum(i - n_ctx, 0), 0)
    act = pl.BlockSpec((mt, BR_WIDTH), row)
    act_ctx = pl.BlockSpec((mt, BR_WIDTH), ctx_rows)
    act_lat = pl.BlockSpec((mt, BR_WIDTH), lat_rows)
    wide = pl.BlockSpec((mt, LANES), row)
    grid_spec = pltpu.PrefetchScalarGridSpec(
        num_scalar_prefetch=1,
        grid=(N_TILES // MERGE_SUB,),
        in_specs=[
            pl.BlockSpec((mt, D_MODEL), ctx_rows),
            pl.BlockSpec((mt, D_MODEL), lat_rows),
            pl.BlockSpec((None, 1, 6 * D_MODEL), lambda i, a: (a[i * MERGE_SUB], 0, 0)),
            pl.BlockSpec((4, D_MODEL), full),
            act, act, act_ctx, act_lat, act_ctx, act_lat,
            pl.BlockSpec((D_MODEL, N_BRANCH * D_MODEL), full),
            pl.BlockSpec((N_BRANCH, BR_WIDTH, D_MODEL), lambda i, a: (0, 0, 0)),
            pl.BlockSpec((D_MODEL, D_MODEL), full),
            pl.BlockSpec((D_MODEL, LANES), full),
            pl.BlockSpec((D_MODEL, LANES), full),
            pl.BlockSpec((1, LANES), full),
            pl.BlockSpec((TM, TM), full),
        ],
        out_specs=[
            pl.BlockSpec((mt, D_MODEL), row),
            pl.BlockSpec((mt * SUBL, LANES), row),
            wide, wide, wide,
            pl.BlockSpec((8, LANES), full),
        ],
        scratch_shapes=[pltpu.VMEM((8, LANES), _F32)],
    )
    return pl.pallas_call(
        _merge_kernel,
        grid_spec=grid_spec,
        out_shape=[
            jax.ShapeDtypeStruct((T_ALL, D_MODEL), _F32),
            jax.ShapeDtypeStruct((T_ALL * SUBL, LANES), _F32),
            jax.ShapeDtypeStruct((T_ALL, LANES), jnp.int32),
            jax.ShapeDtypeStruct((T_ALL, LANES), _F32),
            jax.ShapeDtypeStruct((T_ALL, LANES), jnp.int32),
            jax.ShapeDtypeStruct((8, LANES), jnp.int32),
        ],
        compiler_params=_params(1),
        name="merge_router",
    )(modrow, xc, xl, mod3, norm_g, oa, ob, occ, ocl, odc, odl, wg, wbr, wo, wr_hi, wr_lo, br, ltri)


def _expert_kernel(blk_e_ref, nvalid_ref,
                   tok0_ref, tok1_ref, tok2_ref, slotp_ref, slotc_ref,
                   n2_hbm, w1_ref, b1_ref, w2_ref, b2_ref,
                   y_hbm,
                   w1b_ref, w2b_ref, x0_ref, x1_ref, x2_ref, y0_ref, y1_ref, y2_ref, gsem, ssem):
    i = pl.program_id(0)
    nb = pl.num_programs(0)
    nvalid = nvalid_ref[0]
    xbufs = (x0_ref, x1_ref, x2_ref)
    ybufs = (y0_ref, y1_ref, y2_ref)

    def gather_copy(tok, r, p):
        return pltpu.make_async_copy(n2_hbm.at[tok], xbufs[p].at[pl.ds(SUBL * r, SUBL), :],
                                     gsem.at[p])

    def scatter_copy(slot, r, p):
        return pltpu.make_async_copy(ybufs[p].at[pl.ds(SUBL * r, SUBL), :], y_hbm.at[slot],
                                     ssem.at[p])

    def wait_gather(p):
        for r in range(MOE_BLK):
            gather_copy(0, r, p).wait()

    def wait_scatter(p):
        for r in range(MOE_BLK):
            scatter_copy(0, r, p).wait()

    @pl.when(i == 0)
    def _():
        ybufs[RING - 1][...] = jnp.zeros_like(ybufs[RING - 1])
        for r in range(MOE_BLK):
            gather_copy(tok0_ref[0, r], r, 0).start()
            gather_copy(tok1_ref[0, r], r, 1).start()

    prev = blk_e_ref[jnp.maximum(i - 1, 0)]
    fresh = (i == 0) | (blk_e_ref[i] != prev)

    @pl.when(fresh)
    def _():
        w1b_ref[...] = w1_ref[...].astype(_BF16)
        w2b_ref[...] = w2_ref[...].astype(_BF16)

    def valid_step(p):
        ahead = (p + 2) % RING
        behind = (p - 1) % RING
        wait_gather(p)

        @pl.when(i >= 2)
        def _():
            wait_scatter(p)

        for r in range(MOE_BLK):
            gather_copy(tok2_ref[0, r], r, ahead).start(priority=0)
            scatter_copy(slotp_ref[0, r], r, behind).start(priority=1)
        x = _load_row_tiles(xbufs[p], MOE_BLK).astype(_BF16)
        hh = jnp.dot(x, w1b_ref[...], preferred_element_type=_F32) + b1_ref[...]
        x_glu = jnp.minimum(hh[:, 0:D_FF], SWIGLU_LIMIT)
        x_lin = jnp.clip(hh[:, D_FF:2 * D_FF], -SWIGLU_LIMIT, SWIGLU_LIMIT)
        act = x_glu * jax.nn.sigmoid(SWIGLU_ALPHA * x_glu) * (x_lin + 1.0)
        _store_row_tiles(ybufs[p], MOE_BLK,
                         jnp.dot(act.astype(_BF16), w2b_ref[...], preferred_element_type=_F32)
                         + b2_ref[...])

    def tail_step(p):
        behind = (p - 1) % RING
        behind2 = (p - 2) % RING

        @pl.when(i < nvalid + 2)
        def _():
            wait_gather(p)

        wait_scatter(p)
        ybufs[p][...] = jnp.zeros((MOE_BLK * SUBL, LANES), _F32)
        for r in range(MOE_BLK):
            scatter_copy(slotp_ref[0, r], r, behind).start()

        @pl.when(i == nb - 1)
        def _():
            for r in range(MOE_BLK):
                scatter_copy(slotc_ref[0, r], r, p).start()
            wait_scatter(behind2)
            wait_scatter(behind)
            wait_scatter(p)

    is_valid = i < nvalid
    for p in range(RING):
        pl.when(is_valid & (i % RING == p))(functools.partial(valid_step, p))
        pl.when((~is_valid) & (i % RING == p))(functools.partial(tail_step, p))


def _expert_call(l, blk_e, nvalid, tok_of_row, slot_of_row, n2, w1, b1, w2, b2):
    wsel = lambda i, e, n: (l, e[i], 0, 0)
    at = lambda d: (lambda i, e, n: (jnp.minimum(i + d, N_MOE_BLOCKS - 1), 0, 0))
    smem_blk = lambda imap: pl.BlockSpec((None, 1, MOE_BLK), imap, memory_space=pltpu.SMEM)
    tok3 = tok_of_row.reshape(N_MOE_BLOCKS, 1, MOE_BLK)
    virtual = N_MOE_ROWS + jnp.arange(MOE_BLK, dtype=jnp.int32)
    slot3 = jnp.concatenate([virtual, slot_of_row]).reshape(N_MOE_BLOCKS + 1, 1, MOE_BLK)
    row_buf = pltpu.VMEM((MOE_BLK * SUBL, LANES), _F32)
    grid_spec = pltpu.PrefetchScalarGridSpec(
        num_scalar_prefetch=2,
        grid=(N_MOE_BLOCKS,),
        in_specs=[
            smem_blk(at(0)), smem_blk(at(1)), smem_blk(at(2)),
            smem_blk(lambda i, e, n: (i, 0, 0)),
            smem_blk(lambda i, e, n: (i + 1, 0, 0)),
            pl.BlockSpec(memory_space=pl.ANY),
            pl.BlockSpec((None, None, D_MODEL, 2 * D_FF), wsel),
            pl.BlockSpec((None, None, 1, 2 * D_FF), wsel),
            pl.BlockSpec((None, None, D_FF, D_MODEL), wsel),
            pl.BlockSpec((None, None, 1, D_MODEL), wsel),
        ],
        out_specs=pl.BlockSpec(memory_space=pl.ANY),
        scratch_shapes=[pltpu.VMEM((D_MODEL, 2 * D_FF), _BF16),
                        pltpu.VMEM((D_FF, D_MODEL), _BF16)]
                       + [row_buf] * (2 * RING)
                       + [pltpu.SemaphoreType.DMA((RING,)), pltpu.SemaphoreType.DMA((RING,))],
    )
    return pl.pallas_call(
        _expert_kernel,
        grid_spec=grid_spec,
        out_shape=jax.ShapeDtypeStruct((N_MOE_ROWS + MOE_BLK, SUBL, LANES), _F32),
        compiler_params=_params(1),
        name="experts",
    )(blk_e, nvalid, tok3, tok3, tok3, slot3, slot3, n2, w1,
      b1.reshape(DEPTH, N_EXPERTS, 1, 2 * D_FF), w2, b2.reshape(DEPTH, N_EXPERTS, 1, D_MODEL))


def _combine_kernel(modrow_ref, h1_ref, mod_ref, ng_ref, y0_ref, y1_ref, y2_ref, y3_ref, w_ref,
                    oc_ref, ol_ref):
    i = pl.program_id(0)
    mod = mod_ref[...]
    g2 = mod[:, 5 * D_MODEL:6 * D_MODEL]
    w = w_ref[...]
    f = jnp.zeros((TM, D_MODEL), _F32)
    for k, y_ref in enumerate((y0_ref, y1_ref, y2_ref, y3_ref)):
        f = f + _load_row_tiles(y_ref, TM) * w[:, k:k + 1]
    h2 = h1_ref[...] + g2 * _rms(f, ng_ref[3:4, :])

    @pl.when(i < CTX_TILES)
    def _():
        oc_ref[...] = h2

    @pl.when(i >= CTX_TILES)
    def _():
        ol_ref[...] = h2


def _combine_call(h1, mod3, norm_g, yslot, topw, modrow):
    row = lambda i, a: (i, 0)
    choice = lambda k: pl.BlockSpec((TM * SUBL, LANES), lambda i, a: (k * N_TILES + i, 0))
    yslot = yslot.reshape(-1, LANES)
    grid_spec = pltpu.PrefetchScalarGridSpec(
        num_scalar_prefetch=1,
        grid=(N_TILES,),
        in_specs=[
            pl.BlockSpec((TM, D_MODEL), row),
            pl.BlockSpec((None, 1, 6 * D_MODEL), lambda i, a: (a[i], 0, 0)),
            pl.BlockSpec((4, D_MODEL), lambda i, a: (0, 0)),
            choice(0), choice(1), choice(2), choice(3),
            pl.BlockSpec((TM, LANES), row),
        ],
        out_specs=[pl.BlockSpec((TM, D_MODEL), _ctx_rows), pl.BlockSpec((TM, D_MODEL), _lat_rows)],
    )
    return pl.pallas_call(
        _combine_kernel,
        grid_spec=grid_spec,
        out_shape=[jax.ShapeDtypeStruct((T_CTX, D_MODEL), _F32),
                   jax.ShapeDtypeStruct((T_LAT, D_MODEL), _F32)],
        compiler_params=_params(1),
        name="moe_combine",
    )(modrow, h1, mod3, norm_g, yslot, yslot, yslot, yslot, topw)


def _tile_tables():
    t = np.arange(N_TILES)
    lat = t >= CTX_TILES
    j = (t - CTX_TILES) % LAT_TILES_PER_SEQ
    modrow = np.where(lat, 1 + (t - CTX_TILES) // LAT_TILES_PER_SEQ, 0)
    ropeblk = np.where(lat, j, LAT_TILES_PER_SEQ)
    stblk = np.where(lat, CTX_TILES, t)
    hasprev = np.where(lat & (j > 0), 1, 0)
    hasnext = np.where(lat & (j < LAT_TILES_PER_SEQ - 1), 1, 0)
    as_i32 = lambda a: jnp.asarray(a, jnp.int32)
    return as_i32(modrow), as_i32(ropeblk), as_i32(stblk), as_i32(hasprev), as_i32(hasnext)


def _rope_tables():
    f32 = np.float32
    half = HEAD_DIM // 4
    freqs = np.power(f32(ROPE_THETA), -np.arange(half, dtype=f32) / f32(half)).astype(f32)
    t = np.arange(DEC_SEQ)
    pos = np.stack([t // GRID_W, t % GRID_W], axis=1).astype(f32)
    ang = (pos[:, :, None] * freqs[None, None, :]).astype(f32)
    cos = np.cos(ang).astype(f32)
    sin = np.sin(ang).astype(f32)
    cos_h = np.concatenate([cos, cos], axis=-1).reshape(DEC_SEQ, HEAD_DIM)
    sin_h = np.concatenate([-sin, sin], axis=-1).reshape(DEC_SEQ, HEAD_DIM)
    cos_t = np.concatenate([np.tile(cos_h, (1, N_HEADS)), np.ones((TM, BR_WIDTH), f32)], axis=0)
    sin_t = np.concatenate([np.tile(sin_h, (1, N_HEADS)), np.zeros((TM, BR_WIDTH), f32)], axis=0)
    return jnp.asarray(cos_t), jnp.asarray(sin_t)


def _permute_heads_cols(w):
    lead = w.shape[:-1]
    return w.reshape(lead + (N_HEADS, HEAD_DIM))[..., jnp.array([0, 2, 1, 3]), :].reshape(lead + (BR_WIDTH,))


def _cache_rows(cache, l):
    return cache[:, l].reshape(DEC_BATCH, PAST_LEN, KV_WIDTH).astype(_BF16)


PAIR_CHUNK = 4096
SMEM_1D_TILE = 1024
PAIR_ROWS = -(-N_MOE_ROWS // SMEM_1D_TILE) * SMEM_1D_TILE


def _row_pairs_kernel(dest_ref, out_ref):
    step = pl.program_id(0)

    @pl.when(step == 0)
    def _():
        def init(j, carry):
            out_ref[j] = -1
            return carry
        lax.fori_loop(0, PAIR_ROWS, init, 0, unroll=32)

    base = step * PAIR_CHUNK

    def body(j, carry):
        out_ref[dest_ref[0, j]] = base + j
        return carry
    lax.fori_loop(0, PAIR_CHUNK, body, 0, unroll=32)


def _row_pairs_call(dest_flat):
    n_chunks = N_SLOTS // PAIR_CHUNK
    return pl.pallas_call(
        _row_pairs_kernel,
        grid=(n_chunks,),
        in_specs=[pl.BlockSpec((None, 1, PAIR_CHUNK), lambda s: (s, 0, 0), memory_space=pltpu.SMEM)],
        out_specs=pl.BlockSpec(memory_space=pltpu.SMEM),
        out_shape=jax.ShapeDtypeStruct((PAIR_ROWS,), jnp.int32),
        compiler_params=_params(1),
        name="row_pairs",
    )(dest_flat.reshape(n_chunks, 1, PAIR_CHUNK))


def _lookup(table, idx):
    hit = idx[..., None] == jnp.arange(N_EXPERTS, dtype=jnp.int32)
    return jnp.sum(jnp.where(hit, table, 0), axis=-1)


def _moe_layout(cnt, topi, rank):
    counts = cnt[0, :N_EXPERTS]
    padded = (counts + MOE_BLK - 1) // MOE_BLK * MOE_BLK
    pad_end = jnp.cumsum(padded)
    pad_start = pad_end - padded
    blk_row0 = jnp.arange(N_MOE_BLOCKS, dtype=jnp.int32) * MOE_BLK
    blk_e = jnp.minimum(jnp.sum(pad_end[None, :] <= blk_row0[:, None], axis=1),
                        N_EXPERTS - 1).astype(jnp.int32)
    nvalid = (pad_end[-1:] // MOE_BLK).astype(jnp.int32)
    spare_before = jnp.cumsum(padded - counts) - (padded - counts)
    row = jnp.arange(N_MOE_ROWS, dtype=jnp.int32).reshape(N_MOE_BLOCKS, MOE_BLK)
    in_tail = (blk_row0 >= pad_end[-1])[:, None]
    spare_in_group = row + _lookup(spare_before - pad_start - counts, blk_e)[:, None]
    spare_tail = row - pad_end[-1] + jnp.sum(padded - counts)
    spare = N_SLOTS + jnp.where(in_tail, spare_tail, spare_in_group)
    dest = _lookup(pad_start, topi[:, :TOP_K]) + rank[:, :TOP_K]
    pair_of_row = _row_pairs_call(dest.reshape(-1))[:N_MOE_ROWS]
    is_pair = pair_of_row >= 0
    tok_of_row = jnp.where(is_pair, pair_of_row // TOP_K, 0)
    slot_of_row = jnp.where(is_pair, (pair_of_row % TOP_K) * T_ALL + tok_of_row,
                            spare.reshape(-1).astype(jnp.int32))
    return blk_e, nvalid, tok_of_row, slot_of_row


def kernel(x_prompt, x_sample, cache_win_k, cache_win_v, cache_full_k, cache_full_v, c, c_ctx,
           w_mod, b_mod, norm_g, w_in, a_ln_g, a_ln_b, w_sp, b_sp, conv_w, conv_b, b_ln_g, b_ln_b,
           sink, qn_g, kn_g, w_br, w_o, w_router, b_router, w1, b1, w2, b2):
    modrow, ropeblk, stblk, hasprev, hasnext = _tile_tables()
    cos_t, sin_t = _rope_tables()
    head_id = np.arange(BR_WIDTH) // HEAD_DIM
    ones_bd = jnp.asarray(head_id[:, None] == head_id[None, :], _BF16)
    ltri = jnp.asarray(np.tril(np.ones((TM, TM)), -1), _BF16)
    row2 = lambda v: v.reshape(1, -1)

    cvec = jnp.concatenate([c_ctx[None, :], c, jnp.zeros((8 - 1 - DEC_BATCH, D_MODEL), _F32)], axis=0)
    mod_all = _mod_call(cvec, w_mod, b_mod)

    hc = x_prompt.reshape(T_CTX, D_MODEL)
    hl = x_sample.reshape(T_LAT, D_MODEL)
    states = []
    for l in range(DEPTH):
        mod3 = mod_all[l].reshape(8, 1, 6 * D_MODEL)
        wl = w_in[l]
        w_main = jnp.concatenate([
            wl[:, 0:1024],
            _permute_heads_cols(wl[:, 1024:1280]), wl[:, 1280:1536],
            _permute_heads_cols(wl[:, 1536:1792]), wl[:, 1792:2048]], axis=1).astype(_BF16)
        wg = wl[:, MAIN_WIDTH:].astype(_BF16)
        qn = jnp.tile(qn_g[l], N_HEADS).reshape(1, BR_WIDTH)
        kn = jnp.tile(kn_g[l], KV_HEADS).reshape(1, KV_WIDTH)

        ab, qc, kc, vc, qd, kd, vd, st = _inproj_call(
            hc, hl, mod3, row2(norm_g[l, 0]), w_main, qn, kn, ones_bd, cos_t, sin_t,
            (modrow, ropeblk, stblk))
        states.append(st[:T_CTX].reshape(BATCH, SEQ, 4, KV_HEADS, HEAD_DIM))

        bsp = jnp.repeat(b_sp[l].T, BR_WIDTH // A_GROUPS, axis=1)
        cw = jnp.concatenate([conv_w[l], jnp.zeros((1, BR_WIDTH), _F32)], axis=0)
        oa, ob = _mix_call(ab, row2(a_ln_g[l]), row2(a_ln_b[l]), w_sp[l].astype(_BF16), bsp, cw,
                           row2(conv_b[l]), row2(b_ln_g[l]), row2(b_ln_b[l]), (hasprev, hasnext))

        occ, odc = _ctx_attn_call(sink[l], qc, kc, vc, qd, kd, vd)
        ocl = _win_attn_call(sink[l], qc, kc, vc, _cache_rows(cache_win_k, l),
                             _cache_rows(cache_win_v, l))
        odl = _full_attn_call(qd, kd, vd, _cache_rows(cache_full_k, l),
                              _cache_rows(cache_full_v, l))

        perm = jnp.array([0, 2, 1, 3])
        wbr = w_br[l].reshape(N_BRANCH, N_HEADS, HEAD_DIM, D_MODEL)
        wbr = jnp.concatenate([wbr[0:2], wbr[2:4][:, perm]], axis=0)
        wbr = wbr.reshape(N_BRANCH, BR_WIDTH, D_MODEL).astype(_BF16)
        wr = jnp.concatenate([w_router[l], jnp.zeros((D_MODEL, LANES - N_EXPERTS), _F32)], axis=1)
        wr_hi, wr_lo = _hi_lo(wr)
        br = jnp.concatenate([b_router[l], jnp.full((LANES - N_EXPERTS,), NEG_BIG, _F32)]).reshape(1, LANES)
        h1, n2, topi, topw, rank, cnt = _merge_call(
            hc, hl, mod3, norm_g[l], oa, ob, occ, ocl, odc, odl, wg, wbr, w_o[l].astype(_BF16),
            wr_hi, wr_lo, br, ltri, modrow)

        blk_e, nvalid, tok_of_row, slot_of_row = _moe_layout(cnt, topi, rank)
        yslot = _expert_call(l, blk_e, nvalid, tok_of_row, slot_of_row,
                             n2.reshape(T_ALL, SUBL, LANES), w1, b1, w2, b2)
        hc, hl = _combine_call(h1, mod3, norm_g[l], yslot, topw, modrow)

    y_prompt = hc.reshape(BATCH, SEQ, D_MODEL)
    y_sample = hl.reshape(DEC_BATCH, DEC_SEQ, D_MODEL)
    st = jnp.stack(states, axis=1)
    return (y_prompt, y_sample, st[:, :, :, 0], st[:, :, :, 1], st[:, :, :, 2], st[:, :, :, 3])
```

```python
import functools

import numpy as np
import jax
import jax.numpy as jnp
from jax import lax
from jax.experimental import pallas as pl
from jax.experimental.pallas import tpu as pltpu

D_MODEL = 1024
BATCH = 16
SEQ = 256
DEPTH = 2
DEC_BATCH = 4
DEC_SEQ = 4096
PAST_LEN = 256
GRID_W = 64
HEAD_DIM = 64
BR_WIDTH = 256
N_BRANCH = 4
A_GROUPS = 4
CHUNK = 128
CONV_W = 31
N_HEADS = 4
KV_HEADS = 2
KV_WIDTH = KV_HEADS * HEAD_DIM
WINDOW = 128
ROPE_THETA = 10000.0
EPS = 1e-6
N_EXPERTS = 32
TOP_K = 4
D_FF = D_MODEL
SWIGLU_LIMIT = 7.0
SWIGLU_ALPHA = 1.702
MAIN_WIDTH = 2048

LANES = 128
T_CTX = BATCH * SEQ
T_LAT = DEC_BATCH * DEC_SEQ
T_ALL = T_CTX + T_LAT
TM = 256
MERGE_SUB = 2
N_TILES = T_ALL // TM
CTX_TILES = T_CTX // TM
LAT_TILES_PER_SEQ = DEC_SEQ // TM
HALO = 16
SUBLANES = 8
SHIFT_ROWS = TM + (HALO + CONV_W // 2) // SUBLANES * SUBLANES
MOE_BLK = 256
N_SLOTS = T_ALL * TOP_K
RING = 3
N_MOE_BLOCKS = N_SLOTS // MOE_BLK + N_EXPERTS + 1
N_MOE_ROWS = N_MOE_BLOCKS * MOE_BLK
WIN_TQ = 512
FULL_TQ = 512
FULL_TK = 512
NEG_BIG = -1e30
LOG2E = 1.4426950408889634
VMEM_LIMIT = 56 * 1024 * 1024

_F32 = jnp.float32
_BF16 = jnp.bfloat16


def _params(n_axes, vmem=None):
    return pltpu.CompilerParams(
        dimension_semantics=("arbitrary",) * n_axes,
        vmem_limit_bytes=vmem if vmem is not None else VMEM_LIMIT)


def _split_dot(a, b_hi, b_lo):
    a_hi = a.astype(_BF16)
    a_lo = (a - a_hi.astype(_F32)).astype(_BF16)
    return (jnp.dot(a_hi, b_hi, preferred_element_type=_F32)
            + jnp.dot(a_lo, b_hi, preferred_element_type=_F32)
            + jnp.dot(a_hi, b_lo, preferred_element_type=_F32))


def _hi_lo(w):
    hi = w.astype(_BF16)
    lo = (w - hi.astype(_F32)).astype(_BF16)
    return hi, lo


def _mod_kernel(c_ref, w_ref, b_ref, o_ref):
    c = c_ref[...]
    s = c * jax.nn.sigmoid(c)
    w = w_ref[...]
    w_hi = w.astype(_BF16)
    w_lo = (w - w_hi.astype(_F32)).astype(_BF16)
    o_ref[...] = _split_dot(s, w_hi, w_lo) + b_ref[...]


def _mod_call(cvec, w_mod, b_mod):
    tn = 1024
    return pl.pallas_call(
        _mod_kernel,
        grid=(DEPTH, 6 * D_MODEL // tn),
        in_specs=[
            pl.BlockSpec((8, D_MODEL), lambda l, j: (0, 0)),
            pl.BlockSpec((None, D_MODEL, tn), lambda l, j: (l, 0, j)),
            pl.BlockSpec((None, 1, tn), lambda l, j: (l, 0, j)),
        ],
        out_specs=pl.BlockSpec((None, 8, tn), lambda l, j: (l, 0, j)),
        out_shape=jax.ShapeDtypeStruct((DEPTH, 8, 6 * D_MODEL), _F32),
        compiler_params=_params(2),
        name="mod",
    )(cvec, w_mod, b_mod.reshape(DEPTH, 1, 6 * D_MODEL))


def _rms(x, g):
    ms = jnp.mean(x * x, axis=-1, keepdims=True)
    return x * lax.rsqrt(ms + EPS) * g


def _head_rms(x, ones_bd, g):
    xx = x * x
    hi = xx.astype(_BF16)
    lo = (xx - hi.astype(_F32)).astype(_BF16)
    ss = (jnp.dot(hi, ones_bd, preferred_element_type=_F32)
          + jnp.dot(lo, ones_bd, preferred_element_type=_F32))
    return x * lax.rsqrt(ss * (1.0 / HEAD_DIM) + EPS) * g


def _rope(x, cos, sin_signed):
    w = x.shape[-1]
    lane = lax.broadcasted_iota(jnp.int32, x.shape, 1)
    first = (lane % 32) < 16
    partner = jnp.where(first, pltpu.roll(x, w - 16, 1), pltpu.roll(x, 16, 1))
    return x * cos + partner * sin_signed


def _inproj_kernel(modrow_ref, ropeblk_ref, stblk_ref,
                   xc_ref, xl_ref, mod_ref, g0_ref, w_ref, qn_ref, kn_ref, ones_ref,
                   cos_ref, sin_ref,
                   ab_ref, qc_ref, kc_ref, vc_ref, qd_ref, kd_ref, vd_ref, st_ref):
    x = jnp.where(pl.program_id(0) < CTX_TILES, xc_ref[...], xl_ref[...])
    mod = mod_ref[...]
    sh1 = mod[:, 0:D_MODEL]
    sc1 = mod[:, D_MODEL:2 * D_MODEL]
    n = _rms(x, g0_ref[...]) * (1.0 + sc1) + sh1
    z = jnp.dot(n.astype(_BF16), w_ref[...], preferred_element_type=_F32)
    ab_ref[...] = z[:, 0:1024]
    cq = z[:, 1024:1280]
    ck = z[:, 1280:1408]
    cv = z[:, 1408:1536]
    dq = z[:, 1536:1792]
    dk = z[:, 1792:1920]
    dv = z[:, 1920:2048]
    ones = ones_ref[...]
    dq = _head_rms(dq, ones, qn_ref[...])
    dk = _head_rms(dk, ones[0:KV_WIDTH, 0:KV_WIDTH], kn_ref[...])
    cos = cos_ref[...]
    sin = sin_ref[...]
    scale = HEAD_DIM ** -0.5 * LOG2E
    qc_ref[...] = (_rope(cq, cos, sin) * scale).astype(_BF16)
    qd_ref[...] = (_rope(dq, cos, sin) * scale).astype(_BF16)
    kc = _rope(ck, cos[:, 0:KV_WIDTH], sin[:, 0:KV_WIDTH])
    kd = _rope(dk, cos[:, 0:KV_WIDTH], sin[:, 0:KV_WIDTH])
    kc_ref[...] = kc.astype(_BF16)
    kd_ref[...] = kd.astype(_BF16)
    vc_ref[...] = cv.astype(_BF16)
    vd_ref[...] = dv.astype(_BF16)
    st_ref[:, 0:128] = kc
    st_ref[:, 128:256] = cv
    st_ref[:, 256:384] = kd
    st_ref[:, 384:512] = dv


def _ctx_rows(*_):
    i = _[0]
    return (jnp.minimum(i, CTX_TILES - 1), 0)


def _lat_rows(*_):
    i = _[0]
    return (jnp.maximum(i - CTX_TILES, 0), 0)


def _inproj_call(xc, xl, mod3, g0, w_main, qn, kn, ones_bd, cos_t, sin_t, tables):
    modrow, ropeblk, stblk = tables
    row = lambda i, a, b, c: (i, 0)
    full = lambda i, a, b, c: (0, 0)
    bf = lambda w: jax.ShapeDtypeStruct((T_ALL, w), _BF16)
    grid_spec = pltpu.PrefetchScalarGridSpec(
        num_scalar_prefetch=3,
        grid=(N_TILES,),
        in_specs=[
            pl.BlockSpec((TM, D_MODEL), _ctx_rows),
            pl.BlockSpec((TM, D_MODEL), _lat_rows),
            pl.BlockSpec((None, 1, 6 * D_MODEL), lambda i, a, b, c: (a[i], 0, 0)),
            pl.BlockSpec((1, D_MODEL), full),
            pl.BlockSpec((D_MODEL, MAIN_WIDTH), full),
            pl.BlockSpec((1, BR_WIDTH), full),
            pl.BlockSpec((1, KV_WIDTH), full),
            pl.BlockSpec((BR_WIDTH, BR_WIDTH), full),
            pl.BlockSpec((TM, BR_WIDTH), lambda i, a, b, c: (b[i], 0)),
            pl.BlockSpec((TM, BR_WIDTH), lambda i, a, b, c: (b[i], 0)),
        ],
        out_specs=[
            pl.BlockSpec((TM, 1024), row),
            pl.BlockSpec((TM, BR_WIDTH), row),
            pl.BlockSpec((TM, KV_WIDTH), row),
            pl.BlockSpec((TM, KV_WIDTH), row),
            pl.BlockSpec((TM, BR_WIDTH), row),
            pl.BlockSpec((TM, KV_WIDTH), row),
            pl.BlockSpec((TM, KV_WIDTH), row),
            pl.BlockSpec((TM, 512), lambda i, a, b, c: (c[i], 0)),
        ],
    )
    return pl.pallas_call(
        _inproj_kernel,
        grid_spec=grid_spec,
        out_shape=[
            jax.ShapeDtypeStruct((T_ALL, 1024), _F32),
            bf(BR_WIDTH), bf(KV_WIDTH), bf(KV_WIDTH),
            bf(BR_WIDTH), bf(KV_WIDTH), bf(KV_WIDTH),
            jax.ShapeDtypeStruct((T_CTX + TM, 512), _F32),
        ],
        compiler_params=_params(1),
        name="inproj",
    )(modrow, ropeblk, stblk, xc, xl, mod3, g0, w_main, qn, kn, ones_bd, cos_t, sin_t)


def _layernorm(x, g, b):
    mu = jnp.mean(x, axis=-1, keepdims=True)
    xc = x - mu
    var = jnp.mean(xc * xc, axis=-1, keepdims=True)
    return xc * lax.rsqrt(var + EPS) * g + b


def _gelu(x):
    return 0.5 * x * (1.0 + lax.erf(x * (2.0 ** -0.5)))


def _mix_kernel(hasprev_ref, hasnext_ref,
                ab_ref, prev_ref, next_ref, alg_ref, alb_ref, wsp_ref, bsp_ref,
                cw_ref, cb_ref, blg_ref, blb_ref,
                oa_ref, ob_ref, buf_ref, sh_ref):
    i = pl.program_id(0)
    ab = ab_ref[...]
    u = _gelu(ab[:, 0:256])
    va = _layernorm(_gelu(ab[:, 256:512]), alg_ref[...], alb_ref[...])
    vab = va.astype(_BF16)
    lane = lax.broadcasted_iota(jnp.int32, (CHUNK, BR_WIDTH), 1)
    group = lane // (BR_WIDTH // A_GROUPS)
    bsp = bsp_ref[...]
    for c in range(TM // CHUNK):
        v_c = vab[c * CHUNK:(c + 1) * CHUNK, :]
        sp = bsp
        for g in range(A_GROUPS):
            full = jnp.dot(wsp_ref[g], v_c, preferred_element_type=_F32)
            sp = sp + jnp.where(group == g, full, 0.0)
        oa_ref[c * CHUNK:(c + 1) * CHUNK, :] = (u[c * CHUNK:(c + 1) * CHUNK, :] * sp).astype(_BF16)
    glu = ab[:, 512:768] * jax.nn.sigmoid(ab[:, 768:1024])
    pv = prev_ref[...]
    nx = next_ref[...]
    hp = hasprev_ref[i].astype(_F32)
    hn = hasnext_ref[i].astype(_F32)
    buf_ref[0:HALO, :] = pv[:, 0:256] * jax.nn.sigmoid(pv[:, 256:512]) * hp
    buf_ref[HALO:HALO + TM, :] = glu
    buf_ref[HALO + TM:HALO + TM + HALO, :] = nx[:, 0:256] * jax.nn.sigmoid(nx[:, 256:512]) * hn
    for ph in range(1, SUBLANES):
        sh_ref[ph - 1] = buf_ref[ph:ph + SHIFT_ROWS, :]
    cw = cw_ref[...]
    acc = jnp.zeros((TM, BR_WIDTH), _F32) + cb_ref[...]
    off = HALO - CONV_W // 2
    for k in range(CONV_W):
        start, ph = (off + k) // SUBLANES * SUBLANES, (off + k) % SUBLANES
        src = buf_ref if ph == 0 else sh_ref.at[ph - 1]
        acc = acc + src[start:start + TM, :] * cw[k:k + 1, :]
    y = _layernorm(acc, blg_ref[...], blb_ref[...])
    ob_ref[...] = (y * jax.nn.sigmoid(y)).astype(_BF16)


def _mix_call(ab, a_ln_g, a_ln_b, wsp, bsp, conv_w, conv_b, b_ln_g, b_ln_b, tables):
    hasprev, hasnext = tables
    n_halo_blocks = T_ALL // HALO
    per = TM // HALO
    full2 = lambda i, a, b: (0, 0)
    grid_spec = pltpu.PrefetchScalarGridSpec(
        num_scalar_prefetch=2,
        grid=(N_TILES,),
        in_specs=[
            pl.BlockSpec((TM, 1024), lambda i, a, b: (i, 0)),
            pl.BlockSpec((HALO, 512), lambda i, a, b: (jnp.maximum(i * per - 1, 0), 1)),
            pl.BlockSpec((HALO, 512), lambda i, a, b: (jnp.minimum((i + 1) * per, n_halo_blocks - 1), 1)),
            pl.BlockSpec((1, BR_WIDTH), full2),
            pl.BlockSpec((1, BR_WIDTH), full2),
            pl.BlockSpec((A_GROUPS, CHUNK, CHUNK), lambda i, a, b: (0, 0, 0)),
            pl.BlockSpec((CHUNK, BR_WIDTH), full2),
            pl.BlockSpec((32, BR_WIDTH), full2),
            pl.BlockSpec((1, BR_WIDTH), full2),
            pl.BlockSpec((1, BR_WIDTH), full2),
            pl.BlockSpec((1, BR_WIDTH), full2),
        ],
        out_specs=[
            pl.BlockSpec((TM, BR_WIDTH), lambda i, a, b: (i, 0)),
            pl.BlockSpec((TM, BR_WIDTH), lambda i, a, b: (i, 0)),
        ],
        scratch_shapes=[pltpu.VMEM((TM + 2 * HALO, BR_WIDTH), _F32),
                        pltpu.VMEM((SUBLANES - 1, SHIFT_ROWS, BR_WIDTH), _F32)],
    )
    return pl.pallas_call(
        _mix_kernel,
        grid_spec=grid_spec,
        out_shape=[jax.ShapeDtypeStruct((T_ALL, BR_WIDTH), _BF16)] * 2,
        compiler_params=_params(1),
        name="mixers_ab",
    )(hasprev, hasnext, ab, ab, ab, a_ln_g, a_ln_b, wsp, bsp, conv_w, conv_b, b_ln_g, b_ln_b)


def _lane_masks(n_rows):
    lane = lax.broadcasted_iota(jnp.int32, (n_rows, LANES), 1)
    return lane < HEAD_DIM, lane >= HEAD_DIM


def _group_queries(q, kv):
    tq = q.shape[0]
    lo, hi = _lane_masks(tq)
    keep = lo if kv == 0 else hi
    zero = jnp.zeros((), q.dtype)
    return jnp.concatenate([jnp.where(keep, q[:, 0:LANES], zero),
                            jnp.where(keep, q[:, LANES:2 * LANES], zero)], axis=0)


def _scores(qs, k):
    return lax.dot_general(qs, k, (((1,), (1,)), ((), ())), preferred_element_type=_F32)


def _attend_once(q, chunks, sinks):
    tq = q.shape[0]
    lo, hi = _lane_masks(tq)
    outs = [jnp.zeros((tq, LANES), _F32), jnp.zeros((tq, LANES), _F32)]
    for kv in range(KV_HEADS):
        qs = _group_queries(q, kv)
        ss = []
        for k, _, mask in chunks:
            s = _scores(qs, k)
            if mask is not None:
                s = jnp.where(mask, s, -jnp.inf)
            ss.append(s)
        m = ss[0].max(axis=-1, keepdims=True)
        for s in ss[1:]:
            m = jnp.maximum(m, s.max(axis=-1, keepdims=True))
        if sinks is not None:
            row = lax.broadcasted_iota(jnp.int32, (2 * tq, 1), 0)
            sink_col = jnp.where(row < tq, sinks[2 * kv], sinks[2 * kv + 1])
            m = jnp.maximum(m, sink_col)
            l = jnp.exp2(sink_col - m)
        else:
            l = jnp.zeros((2 * tq, 1), _F32)
        acc = jnp.zeros((2 * tq, LANES), _F32)
        for s, (_, v, _) in zip(ss, chunks):
            p = jnp.exp2(s - m)
            l = l + p.sum(axis=-1, keepdims=True)
            acc = acc + jnp.dot(p.astype(_BF16), v, preferred_element_type=_F32)
        o = acc / l
        keep = lo if kv == 0 else hi
        outs[0] = outs[0] + jnp.where(keep, o[0:tq], 0.0)
        outs[1] = outs[1] + jnp.where(keep, o[tq:2 * tq], 0.0)
    return jnp.concatenate(outs, axis=1)


def _ctx_attn_kernel(sink_ref, qc_ref, kc_ref, vc_ref, qd_ref, kd_ref, vd_ref, oc_ref, od_ref):
    sinks = [sink_ref[h] * LOG2E for h in range(N_HEADS)]
    oc_ref[...] = _attend_once(qc_ref[...], [(kc_ref[...], vc_ref[...], None)], sinks).astype(_BF16)
    od_ref[...] = _attend_once(qd_ref[...], [(kd_ref[...], vd_ref[...], None)], None).astype(_BF16)


def _ctx_attn_call(sink, qc, kc, vc, qd, kd, vd):
    seq = lambda b: (b, 0)
    q_spec = pl.BlockSpec((SEQ, BR_WIDTH), seq)
    kv_spec = pl.BlockSpec((SEQ, KV_WIDTH), seq)
    return pl.pallas_call(
        _ctx_attn_kernel,
        grid=(BATCH,),
        in_specs=[pl.BlockSpec(memory_space=pltpu.SMEM),
                  q_spec, kv_spec, kv_spec, q_spec, kv_spec, kv_spec],
        out_specs=[q_spec, q_spec],
        out_shape=[jax.ShapeDtypeStruct((T_CTX, BR_WIDTH), _BF16)] * 2,
        compiler_params=_params(1),
        name="ctx_attention",
    )(sink, qc, kc, vc, qd, kd, vd)


def _win_attn_kernel(sink_ref, q_ref, kcur_ref, kprev_ref, knext_ref, vcur_ref, vprev_ref,
                     vnext_ref, kctx_ref, vctx_ref, oc_ref):
    j = pl.program_id(1)
    nq = pl.num_programs(1)
    sinks = [sink_ref[h] * LOG2E for h in range(N_HEADS)]
    tq = WIN_TQ
    def rel(n_keys):
        qi = lax.broadcasted_iota(jnp.int32, (2 * tq, n_keys), 0) % tq
        kj = lax.broadcasted_iota(jnp.int32, (2 * tq, n_keys), 1)
        return qi, kj
    qi, kj = rel(tq)
    m_cur = jnp.abs(qi - kj) <= WINDOW
    qi, kj = rel(WINDOW)
    m_prev = ((qi - (kj - WINDOW)) <= WINDOW) & (j > 0)
    m_next = (((kj + tq) - qi) <= WINDOW) & (j < nq - 1)
    chunks = [
        (kcur_ref[...], vcur_ref[...], m_cur),
        (kprev_ref[...], vprev_ref[...], m_prev),
        (knext_ref[...], vnext_ref[...], m_next),
        (kctx_ref[...], vctx_ref[...], None),
    ]
    oc_ref[...] = _attend_once(q_ref[...], chunks, sinks).astype(_BF16)


def _win_attn_call(sink, qc, kc, vc, kctx, vctx):
    nq = DEC_SEQ // WIN_TQ
    ctx_q = T_CTX // WIN_TQ
    sub = WIN_TQ // WINDOW
    ctx_w = T_CTX // WINDOW
    per_seq_w = DEC_SEQ // WINDOW
    cur = lambda b, j: (ctx_q + b * nq + j, 0)
    prev = lambda b, j: (ctx_w + b * per_seq_w + jnp.maximum(j * sub - 1, 0), 0)
    nxt = lambda b, j: (ctx_w + b * per_seq_w + jnp.minimum((j + 1) * sub, per_seq_w - 1), 0)
    cache = lambda b, j: (b, 0, 0)
    return pl.pallas_call(
        _win_attn_kernel,
        grid=(DEC_BATCH, nq),
        in_specs=[
            pl.BlockSpec(memory_space=pltpu.SMEM),
            pl.BlockSpec((WIN_TQ, BR_WIDTH), cur),
            pl.BlockSpec((WIN_TQ, KV_WIDTH), cur),
            pl.BlockSpec((WINDOW, KV_WIDTH), prev),
            pl.BlockSpec((WINDOW, KV_WIDTH), nxt),
            pl.BlockSpec((WIN_TQ, KV_WIDTH), cur),
            pl.BlockSpec((WINDOW, KV_WIDTH), prev),
            pl.BlockSpec((WINDOW, KV_WIDTH), nxt),
            pl.BlockSpec((None, PAST_LEN, KV_WIDTH), cache),
            pl.BlockSpec((None, PAST_LEN, KV_WIDTH), cache),
        ],
        out_specs=pl.BlockSpec((WIN_TQ, BR_WIDTH), lambda b, j: (b * nq + j, 0)),
        out_shape=jax.ShapeDtypeStruct((T_LAT, BR_WIDTH), _BF16),
        compiler_params=_params(2),
        name="window_attention",
    )(sink, qc, kc, kc, kc, vc, vc, vc, kctx, vctx)


def _full_attn_kernel(q_ref, k_ref, v_ref, kctx_ref, vctx_ref, od_ref, s_ref, sctx_ref):
    tq = FULL_TQ
    n_chunks = DEC_SEQ // FULL_TK
    q = q_ref[...]
    lo, hi = _lane_masks(tq)
    qss = [_group_queries(q, kv) for kv in range(KV_HEADS)]

    def sweep_max(c, ms):
        start = pl.multiple_of(c * FULL_TK, FULL_TK)
        k = k_ref[pl.ds(start, FULL_TK), :]
        new = []
        for kv in range(KV_HEADS):
            s = _scores(qss[kv], k)
            s_ref[c, kv] = s
            new.append(jnp.maximum(ms[kv], s.max(axis=-1, keepdims=True)))
        return tuple(new)

    neg = jnp.full((2 * tq, 1), -jnp.inf, _F32)
    ms = lax.fori_loop(0, n_chunks, sweep_max, (neg, neg), unroll=True)
    kctx = kctx_ref[...]
    ms = list(ms)
    for kv in range(KV_HEADS):
        s = _scores(qss[kv], kctx)
        sctx_ref[kv] = s
        ms[kv] = jnp.maximum(ms[kv], s.max(axis=-1, keepdims=True))

    def accumulate(carry, kv, s, v):
        l, acc = carry
        p = jnp.exp2(s - ms[kv])
        return (l + p.sum(axis=-1, keepdims=True),
                acc + jnp.dot(p.astype(_BF16), v, preferred_element_type=_F32))

    def sweep_sum(c, carries):
        start = pl.multiple_of(c * FULL_TK, FULL_TK)
        v = v_ref[pl.ds(start, FULL_TK), :]
        return tuple(accumulate(carries[kv], kv, s_ref[c, kv], v) for kv in range(KV_HEADS))

    zero = (jnp.zeros((2 * tq, 1), _F32), jnp.zeros((2 * tq, LANES), _F32))
    carries = lax.fori_loop(0, n_chunks, sweep_sum, (zero, zero), unroll=4)
    vctx = vctx_ref[...]
    carries = [accumulate(carries[kv], kv, sctx_ref[kv], vctx) for kv in range(KV_HEADS)]
    outs = [jnp.zeros((tq, LANES), _F32), jnp.zeros((tq, LANES), _F32)]
    for kv, (l, acc) in enumerate(carries):
        o = acc / l
        keep = lo if kv == 0 else hi
        outs[0] = outs[0] + jnp.where(keep, o[0:tq], 0.0)
        outs[1] = outs[1] + jnp.where(keep, o[tq:2 * tq], 0.0)
    od_ref[...] = jnp.concatenate(outs, axis=1).astype(_BF16)


def _full_attn_call(qd, kd, vd, kctx, vctx):
    nq = DEC_SEQ // FULL_TQ
    ctx_q = T_CTX // FULL_TQ
    ctx_seq = T_CTX // DEC_SEQ
    cur = lambda b, j: (ctx_q + b * nq + j, 0)
    seq = lambda b, j: (ctx_seq + b, 0)
    cache = lambda b, j: (b, 0, 0)
    return pl.pallas_call(
        _full_attn_kernel,
        grid=(DEC_BATCH, nq),
        in_specs=[
            pl.BlockSpec((FULL_TQ, BR_WIDTH), cur),
            pl.BlockSpec((DEC_SEQ, KV_WIDTH), seq),
            pl.BlockSpec((DEC_SEQ, KV_WIDTH), seq),
            pl.BlockSpec((None, PAST_LEN, KV_WIDTH), cache),
            pl.BlockSpec((None, PAST_LEN, KV_WIDTH), cache),
        ],
        out_specs=pl.BlockSpec((FULL_TQ, BR_WIDTH), lambda b, j: (b * nq + j, 0)),
        out_shape=jax.ShapeDtypeStruct((T_LAT, BR_WIDTH), _BF16),
        scratch_shapes=[pltpu.VMEM((DEC_SEQ // FULL_TK, KV_HEADS, 2 * FULL_TQ, FULL_TK), _F32),
                        pltpu.VMEM((KV_HEADS, 2 * FULL_TQ, PAST_LEN), _F32)],
        compiler_params=_params(2),
        name="full_attention",
    )(qd, kd, vd, kctx, vctx)


SUBL = D_MODEL // LANES


def _store_row_tiles(ref, n, x, base=0):
    for c in range(SUBL):
        ref[pl.ds(base + c, n, stride=SUBL), :] = x[:, c * LANES:(c + 1) * LANES]


def _load_row_tiles(ref, n):
    return jnp.concatenate([ref[pl.ds(c, n, stride=SUBL), :] for c in range(SUBL)], axis=1)


def _merge_kernel(modrow_ref,
                  xc_ref, xl_ref, mod_ref, ng_ref, oa_ref, ob_ref, occ_ref, ocl_ref, odc_ref, odl_ref,
                  wg_ref, wbr_ref, wo_ref, wrh_ref, wrl_ref, br_ref, ltri_ref,
                  h1_ref, n2_ref, topi_ref, topw_ref, rank_ref, cnt_ref, base_ref):
    i = pl.program_id(0)

    @pl.when(i == 0)
    def _():
        base_ref[...] = jnp.zeros_like(base_ref)

    is_ctx = i < CTX_TILES // MERGE_SUB
    mod = mod_ref[...]
    sh1 = mod[:, 0:D_MODEL]
    sc1 = mod[:, D_MODEL:2 * D_MODEL]
    g1 = mod[:, 2 * D_MODEL:3 * D_MODEL]
    sh2 = mod[:, 3 * D_MODEL:4 * D_MODEL]
    sc2 = mod[:, 4 * D_MODEL:5 * D_MODEL]
    ng = ng_ref[...]
    lane = lax.broadcasted_iota(jnp.int32, (TM, LANES), 1).astype(_F32)
    ltri = ltri_ref[...]
    running = base_ref[0:1, :]
    for sub in range(MERGE_SUB):
        rows = pl.ds(sub * TM, TM)
        x = jnp.where(is_ctx, xc_ref[rows, :], xl_ref[rows, :])
        nb = (_rms(x, ng[0:1, :]) * (1.0 + sc1) + sh1).astype(_BF16)
        oc = jnp.where(is_ctx, occ_ref[rows, :], ocl_ref[rows, :])
        od = jnp.where(is_ctx, odc_ref[rows, :], odl_ref[rows, :])
        branches = (oa_ref[rows, :], ob_ref[rows, :], oc, od)
        mixed = jnp.zeros((TM, D_MODEL), _F32)
        for k in range(N_BRANCH):
            gate = jax.nn.sigmoid(jnp.dot(nb, wg_ref[:, k * D_MODEL:(k + 1) * D_MODEL],
                                          preferred_element_type=_F32))
            proj = jnp.dot(branches[k], wbr_ref[k], preferred_element_type=_F32)
            mixed = mixed + gate * proj
        mix = jnp.dot(mixed.astype(_BF16), wo_ref[...], preferred_element_type=_F32)
        h1 = x + g1 * _rms(mix, ng[1:2, :])
        h1_ref[rows, :] = h1
        n2 = _rms(h1, ng[2:3, :]) * (1.0 + sc2) + sh2
        _store_row_tiles(n2_ref, TM, n2, base=sub * TM * SUBL)
        logits = _split_dot(n2, wrh_ref[...], wrl_ref[...]) + br_ref[...]
        work = logits
        vals, idxs = [], []
        for _ in range(TOP_K):
            m = work.max(axis=-1, keepdims=True)
            idx = jnp.where(work == m, lane, float(LANES)).min(axis=-1, keepdims=True)
            vals.append(m)
            idxs.append(idx)
            work = jnp.where(lane == idx, -jnp.inf, work)
        es = [jnp.exp(v - vals[0]) for v in vals]
        denom = es[0] + es[1] + es[2] + es[3]
        topi = jnp.zeros((TM, LANES), _F32)
        topw = jnp.zeros((TM, LANES), _F32)
        rank = jnp.zeros((TM, LANES), _F32)
        for k in range(TOP_K):
            onehot = (lane == idxs[k]).astype(_F32)
            before = jnp.dot(ltri, onehot.astype(_BF16), preferred_element_type=_F32)
            r = (onehot * (before + running)).sum(axis=-1, keepdims=True)
            running = running + onehot.sum(axis=0, keepdims=True)
            topi = jnp.where(lane == k, idxs[k], topi)
            topw = jnp.where(lane == k, es[k] / denom, topw)
            rank = jnp.where(lane == k, r, rank)
        topi_ref[rows, :] = topi.astype(jnp.int32)
        topw_ref[rows, :] = topw
        rank_ref[rows, :] = rank.astype(jnp.int32)
    base_ref[0:1, :] = running
    cnt_ref[...] = jnp.broadcast_to(running, (8, LANES)).astype(jnp.int32)


def _merge_call(xc, xl, mod3, norm_g, oa, ob, occ, ocl, odc, odl, wg, wbr, wo, wr_hi, wr_lo, br, ltri,
                modrow):
    mt = TM * MERGE_SUB
    n_ctx = CTX_TILES // MERGE_SUB
    row = lambda i, a: (i, 0)
    full = lambda i, a: (0, 0)
    ctx_rows = lambda i, a: (jnp.minimum(i, n_ctx - 1), 0)
    lat_rows = lambda i, a: (jnp.maximum(i - n_ctx, 0), 0)
    act = pl.BlockSpec((mt, BR_WIDTH), row)
    act_ctx = pl.BlockSpec((mt, BR_WIDTH), ctx_rows)
    act_lat = pl.BlockSpec((mt, BR_WIDTH), lat_rows)
    wide = pl.BlockSpec((mt, LANES), row)
    grid_spec = pltpu.PrefetchScalarGridSpec(
        num_scalar_prefetch=1,
        grid=(N_TILES // MERGE_SUB,),
        in_specs=[
            pl.BlockSpec((mt, D_MODEL), ctx_rows),
            pl.BlockSpec((mt, D_MODEL), lat_rows),
            pl.BlockSpec((None, 1, 6 * D_MODEL), lambda i, a: (a[i * MERGE_SUB], 0, 0)),
            pl.BlockSpec((4, D_MODEL), full),
            act, act, act_ctx, act_lat, act_ctx, act_lat,
            pl.BlockSpec((D_MODEL, N_BRANCH * D_MODEL), full),
            pl.BlockSpec((N_BRANCH, BR_WIDTH, D_MODEL), lambda i, a: (0, 0, 0)),
            pl.BlockSpec((D_MODEL, D_MODEL), full),
            pl.BlockSpec((D_MODEL, LANES), full),
            pl.BlockSpec((D_MODEL, LANES), full),
            pl.BlockSpec((1, LANES), full),
            pl.BlockSpec((TM, TM), full),
        ],
        out_specs=[
            pl.BlockSpec((mt, D_MODEL), row),
            pl.BlockSpec((mt * SUBL, LANES), row),
            wide, wide, wide,
            pl.BlockSpec((8, LANES), full),
        ],
        scratch_shapes=[pltpu.VMEM((8, LANES), _F32)],
    )
    return pl.pallas_call(
        _merge_kernel,
        grid_spec=grid_spec,
        out_shape=[
            jax.ShapeDtypeStruct((T_ALL, D_MODEL), _F32),
            jax.ShapeDtypeStruct((T_ALL * SUBL, LANES), _F32),
            jax.ShapeDtypeStruct((T_ALL, LANES), jnp.int32),
            jax.ShapeDtypeStruct((T_ALL, LANES), _F32),
            jax.ShapeDtypeStruct((T_ALL, LANES), jnp.int32),
            jax.ShapeDtypeStruct((8, LANES), jnp.int32),
        ],
        compiler_params=_params(1),
        name="merge_router",
    )(modrow, xc, xl, mod3, norm_g, oa, ob, occ, ocl, odc, odl, wg, wbr, wo, wr_hi, wr_lo, br, ltri)


def _expert_kernel(blk_e_ref, nvalid_ref,
                   tok0_ref, tok1_ref, tok2_ref, slotp_ref, slotc_ref,
                   n2_hbm, w1_ref, b1_ref, w2_ref, b2_ref,
                   y_hbm,
                   w1b_ref, w2b_ref, x0_ref, x1_ref, x2_ref, y0_ref, y1_ref, y2_ref, gsem, ssem):
    i = pl.program_id(0)
    nb = pl.num_programs(0)
    nvalid = nvalid_ref[0]
    xbufs = (x0_ref, x1_ref, x2_ref)
    ybufs = (y0_ref, y1_ref, y2_ref)

    def gather_copy(tok, r, p):
        return pltpu.make_async_copy(n2_hbm.at[tok], xbufs[p].at[pl.ds(SUBL * r, SUBL), :],
                                     gsem.at[p])

    def scatter_copy(slot, r, p):
        return pltpu.make_async_copy(ybufs[p].at[pl.ds(SUBL * r, SUBL), :], y_hbm.at[slot],
                                     ssem.at[p])

    def wait_gather(p):
        for r in range(MOE_BLK):
            gather_copy(0, r, p).wait()

    def wait_scatter(p):
        for r in range(MOE_BLK):
            scatter_copy(0, r, p).wait()

    @pl.when(i == 0)
    def _():
        ybufs[RING - 1][...] = jnp.zeros_like(ybufs[RING - 1])
        for r in range(MOE_BLK):
            gather_copy(tok0_ref[0, r], r, 0).start()
            gather_copy(tok1_ref[0, r], r, 1).start()

    prev = blk_e_ref[jnp.maximum(i - 1, 0)]
    fresh = (i == 0) | (blk_e_ref[i] != prev)

    @pl.when(fresh)
    def _():
        w1b_ref[...] = w1_ref[...].astype(_BF16)
        w2b_ref[...] = w2_ref[...].astype(_BF16)

    def valid_step(p):
        ahead = (p + 2) % RING
        behind = (p - 1) % RING
        wait_gather(p)

        @pl.when(i >= 2)
        def _():
            wait_scatter(p)

        for r in range(MOE_BLK):
            gather_copy(tok2_ref[0, r], r, ahead).start(priority=0)
            scatter_copy(slotp_ref[0, r], r, behind).start(priority=1)
        x = _load_row_tiles(xbufs[p], MOE_BLK).astype(_BF16)
        hh = jnp.dot(x, w1b_ref[...], preferred_element_type=_F32) + b1_ref[...]
        x_glu = jnp.minimum(hh[:, 0:D_FF], SWIGLU_LIMIT)
        x_lin = jnp.clip(hh[:, D_FF:2 * D_FF], -SWIGLU_LIMIT, SWIGLU_LIMIT)
        act = x_glu * jax.nn.sigmoid(SWIGLU_ALPHA * x_glu) * (x_lin + 1.0)
        _store_row_tiles(ybufs[p], MOE_BLK,
                         jnp.dot(act.astype(_BF16), w2b_ref[...], preferred_element_type=_F32)
                         + b2_ref[...])

    def tail_step(p):
        behind = (p - 1) % RING
        behind2 = (p - 2) % RING

        @pl.when(i < nvalid + 2)
        def _():
            wait_gather(p)

        wait_scatter(p)
        ybufs[p][...] = jnp.zeros((MOE_BLK * SUBL, LANES), _F32)
        for r in range(MOE_BLK):
            scatter_copy(slotp_ref[0, r], r, behind).start()

        @pl.when(i == nb - 1)
        def _():
            for r in range(MOE_BLK):
                scatter_copy(slotc_ref[0, r], r, p).start()
            wait_scatter(behind2)
            wait_scatter(behind)
            wait_scatter(p)

    is_valid = i < nvalid
    for p in range(RING):
        pl.when(is_valid & (i % RING == p))(functools.partial(valid_step, p))
        pl.when((~is_valid) & (i % RING == p))(functools.partial(tail_step, p))


def _expert_call(l, blk_e, nvalid, tok_of_row, slot_of_row, n2, w1, b1, w2, b2):
    wsel = lambda i, e, n: (l, e[i], 0, 0)
    at = lambda d: (lambda i, e, n: (jnp.minimum(i + d, N_MOE_BLOCKS - 1), 0, 0))
    smem_blk = lambda imap: pl.BlockSpec((None, 1, MOE_BLK), imap, memory_space=pltpu.SMEM)
    tok3 = tok_of_row.reshape(N_MOE_BLOCKS, 1, MOE_BLK)
    virtual = N_MOE_ROWS + jnp.arange(MOE_BLK, dtype=jnp.int32)
    slot3 = jnp.concatenate([virtual, slot_of_row]).reshape(N_MOE_BLOCKS + 1, 1, MOE_BLK)
    row_buf = pltpu.VMEM((MOE_BLK * SUBL, LANES), _F32)
    grid_spec = pltpu.PrefetchScalarGridSpec(
        num_scalar_prefetch=2,
        grid=(N_MOE_BLOCKS,),
        in_specs=[
            smem_blk(at(0)), smem_blk(at(1)), smem_blk(at(2)),
            smem_blk(lambda i, e, n: (i, 0, 0)),
            smem_blk(lambda i, e, n: (i + 1, 0, 0)),
            pl.BlockSpec(memory_space=pl.ANY),
            pl.BlockSpec((None, None, D_MODEL, 2 * D_FF), wsel),
            pl.BlockSpec((None, None, 1, 2 * D_FF), wsel),
            pl.BlockSpec((None, None, D_FF, D_MODEL), wsel),
            pl.BlockSpec((None, None, 1, D_MODEL), wsel),
        ],
        out_specs=pl.BlockSpec(memory_space=pl.ANY),
        scratch_shapes=[pltpu.VMEM((D_MODEL, 2 * D_FF), _BF16),
                        pltpu.VMEM((D_FF, D_MODEL), _BF16)]
                       + [row_buf] * (2 * RING)
                       + [pltpu.SemaphoreType.DMA((RING,)), pltpu.SemaphoreType.DMA((RING,))],
    )
    return pl.pallas_call(
        _expert_kernel,
        grid_spec=grid_spec,
        out_shape=jax.ShapeDtypeStruct((N_MOE_ROWS + MOE_BLK, SUBL, LANES), _F32),
        compiler_params=_params(1),
        name="experts",
    )(blk_e, nvalid, tok3, tok3, tok3, slot3, slot3, n2, w1,
      b1.reshape(DEPTH, N_EXPERTS, 1, 2 * D_FF), w2, b2.reshape(DEPTH, N_EXPERTS, 1, D_MODEL))


def _combine_kernel(modrow_ref, h1_ref, mod_ref, ng_ref, y0_ref, y1_ref, y2_ref, y3_ref, w_ref,
                    oc_ref, ol_ref):
    i = pl.program_id(0)
    mod = mod_ref[...]
    g2 = mod[:, 5 * D_MODEL:6 * D_MODEL]
    w = w_ref[...]
    f = jnp.zeros((TM, D_MODEL), _F32)
    for k, y_ref in enumerate((y0_ref, y1_ref, y2_ref, y3_ref)):
        f = f + _load_row_tiles(y_ref, TM) * w[:, k:k + 1]
    h2 = h1_ref[...] + g2 * _rms(f, ng_ref[3:4, :])

    @pl.when(i < CTX_TILES)
    def _():
        oc_ref[...] = h2

    @pl.when(i >= CTX_TILES)
    def _():
        ol_ref[...] = h2


def _combine_call(h1, mod3, norm_g, yslot, topw, modrow):
    row = lambda i, a: (i, 0)
    choice = lambda k: pl.BlockSpec((TM * SUBL, LANES), lambda i, a: (k * N_TILES + i, 0))
    yslot = yslot.reshape(-1, LANES)
    grid_spec = pltpu.PrefetchScalarGridSpec(
        num_scalar_prefetch=1,
        grid=(N_TILES,),
        in_specs=[
            pl.BlockSpec((TM, D_MODEL), row),
            pl.BlockSpec((None, 1, 6 * D_MODEL), lambda i, a: (a[i], 0, 0)),
            pl.BlockSpec((4, D_MODEL), lambda i, a: (0, 0)),
            choice(0), choice(1), choice(2), choice(3),
            pl.BlockSpec((TM, LANES), row),
        ],
        out_specs=[pl.BlockSpec((TM, D_MODEL), _ctx_rows), pl.BlockSpec((TM, D_MODEL), _lat_rows)],
    )
    return pl.pallas_call(
        _combine_kernel,
        grid_spec=grid_spec,
        out_shape=[jax.ShapeDtypeStruct((T_CTX, D_MODEL), _F32),
                   jax.ShapeDtypeStruct((T_LAT, D_MODEL), _F32)],
        compiler_params=_params(1),
        name="moe_combine",
    )(modrow, h1, mod3, norm_g, yslot, yslot, yslot, yslot, topw)


def _tile_tables():
    t = np.arange(N_TILES)
    lat = t >= CTX_TILES
    j = (t - CTX_TILES) % LAT_TILES_PER_SEQ
    modrow = np.where(lat, 1 + (t - CTX_TILES) // LAT_TILES_PER_SEQ, 0)
    ropeblk = np.where(lat, j, LAT_TILES_PER_SEQ)
    stblk = np.where(lat, CTX_TILES, t)
    hasprev = np.where(lat & (j > 0), 1, 0)
    hasnext = np.where(lat & (j < LAT_TILES_PER_SEQ - 1), 1, 0)
    as_i32 = lambda a: jnp.asarray(a, jnp.int32)
    return as_i32(modrow), as_i32(ropeblk), as_i32(stblk), as_i32(hasprev), as_i32(hasnext)


def _rope_tables():
    f32 = np.float32
    half = HEAD_DIM // 4
    freqs = np.power(f32(ROPE_THETA), -np.arange(half, dtype=f32) / f32(half)).astype(f32)
    t = np.arange(DEC_SEQ)
    pos = np.stack([t // GRID_W, t % GRID_W], axis=1).astype(f32)
    ang = (pos[:, :, None] * freqs[None, None, :]).astype(f32)
    cos = np.cos(ang).astype(f32)
    sin = np.sin(ang).astype(f32)
    cos_h = np.concatenate([cos, cos], axis=-1).reshape(DEC_SEQ, HEAD_DIM)
    sin_h = np.concatenate([-sin, sin], axis=-1).reshape(DEC_SEQ, HEAD_DIM)
    cos_t = np.concatenate([np.tile(cos_h, (1, N_HEADS)), np.ones((TM, BR_WIDTH), f32)], axis=0)
    sin_t = np.concatenate([np.tile(sin_h, (1, N_HEADS)), np.zeros((TM, BR_WIDTH), f32)], axis=0)
    return jnp.asarray(cos_t), jnp.asarray(sin_t)


def _permute_heads_cols(w):
    lead = w.shape[:-1]
    return w.reshape(lead + (N_HEADS, HEAD_DIM))[..., jnp.array([0, 2, 1, 3]), :].reshape(lead + (BR_WIDTH,))


def _cache_rows(cache, l):
    return cache[:, l].reshape(DEC_BATCH, PAST_LEN, KV_WIDTH).astype(_BF16)


PAIR_CHUNK = 4096
SMEM_1D_TILE = 1024
PAIR_ROWS = -(-N_MOE_ROWS // SMEM_1D_TILE) * SMEM_1D_TILE


def _row_pairs_kernel(dest_ref, out_ref):
    step = pl.program_id(0)

    @pl.when(step == 0)
    def _():
        def init(j, carry):
            out_ref[j] = -1
            return carry
        lax.fori_loop(0, PAIR_ROWS, init, 0, unroll=32)

    base = step * PAIR_CHUNK

    def body(j, carry):
        out_ref[dest_ref[0, j]] = base + j
        return carry
    lax.fori_loop(0, PAIR_CHUNK, body, 0, unroll=32)


def _row_pairs_call(dest_flat):
    n_chunks = N_SLOTS // PAIR_CHUNK
    return pl.pallas_call(
        _row_pairs_kernel,
        grid=(n_chunks,),
        in_specs=[pl.BlockSpec((None, 1, PAIR_CHUNK), lambda s: (s, 0, 0), memory_space=pltpu.SMEM)],
        out_specs=pl.BlockSpec(memory_space=pltpu.SMEM),
        out_shape=jax.ShapeDtypeStruct((PAIR_ROWS,), jnp.int32),
        compiler_params=_params(1),
        name="row_pairs",
    )(dest_flat.reshape(n_chunks, 1, PAIR_CHUNK))


def _lookup(table, idx):
    hit = idx[..., None] == jnp.arange(N_EXPERTS, dtype=jnp.int32)
    return jnp.sum(jnp.where(hit, table, 0), axis=-1)


def _moe_layout(cnt, topi, rank):
    counts = cnt[0, :N_EXPERTS]
    padded = (counts + MOE_BLK - 1) // MOE_BLK * MOE_BLK
    pad_end = jnp.cumsum(padded)
    pad_start = pad_end - padded
    blk_row0 = jnp.arange(N_MOE_BLOCKS, dtype=jnp.int32) * MOE_BLK
    blk_e = jnp.minimum(jnp.sum(pad_end[None, :] <= blk_row0[:, None], axis=1),
                        N_EXPERTS - 1).astype(jnp.int32)
    nvalid = (pad_end[-1:] // MOE_BLK).astype(jnp.int32)
    spare_before = jnp.cumsum(padded - counts) - (padded - counts)
    row = jnp.arange(N_MOE_ROWS, dtype=jnp.int32).reshape(N_MOE_BLOCKS, MOE_BLK)
    in_tail = (blk_row0 >= pad_end[-1])[:, None]
    spare_in_group = row + _lookup(spare_before - pad_start - counts, blk_e)[:, None]
    spare_tail = row - pad_end[-1] + jnp.sum(padded - counts)
    spare = N_SLOTS + jnp.where(in_tail, spare_tail, spare_in_group)
    dest = _lookup(pad_start, topi[:, :TOP_K]) + rank[:, :TOP_K]
    pair_of_row = _row_pairs_call(dest.reshape(-1))[:N_MOE_ROWS]
    is_pair = pair_of_row >= 0
    tok_of_row = jnp.where(is_pair, pair_of_row // TOP_K, 0)
    slot_of_row = jnp.where(is_pair, (pair_of_row % TOP_K) * T_ALL + tok_of_row,
                            spare.reshape(-1).astype(jnp.int32))
    return blk_e, nvalid, tok_of_row, slot_of_row


def kernel(x_prompt, x_sample, cache_win_k, cache_win_v, cache_full_k, cache_full_v, c, c_ctx,
           w_mod, b_mod, norm_g, w_in, a_ln_g, a_ln_b, w_sp, b_sp, conv_w, conv_b, b_ln_g, b_ln_b,
           sink, qn_g, kn_g, w_br, w_o, w_router, b_router, w1, b1, w2, b2):
    modrow, ropeblk, stblk, hasprev, hasnext = _tile_tables()
    cos_t, sin_t = _rope_tables()
    head_id = np.arange(BR_WIDTH) // HEAD_DIM
    ones_bd = jnp.asarray(head_id[:, None] == head_id[None, :], _BF16)
    ltri = jnp.asarray(np.tril(np.ones((TM, TM)), -1), _BF16)
    row2 = lambda v: v.reshape(1, -1)

    cvec = jnp.concatenate([c_ctx[None, :], c, jnp.zeros((8 - 1 - DEC_BATCH, D_MODEL), _F32)], axis=0)
    mod_all = _mod_call(cvec, w_mod, b_mod)

    hc = x_prompt.reshape(T_CTX, D_MODEL)
    hl = x_sample.reshape(T_LAT, D_MODEL)
    states = []
    for l in range(DEPTH):
        mod3 = mod_all[l].reshape(8, 1, 6 * D_MODEL)
        wl = w_in[l]
        w_main = jnp.concatenate([
            wl[:, 0:1024],
            _permute_heads_cols(wl[:, 1024:1280]), wl[:, 1280:1536],
            _permute_heads_cols(wl[:, 1536:1792]), wl[:, 1792:2048]], axis=1).astype(_BF16)
        wg = wl[:, MAIN_WIDTH:].astype(_BF16)
        qn = jnp.tile(qn_g[l], N_HEADS).reshape(1, BR_WIDTH)
        kn = jnp.tile(kn_g[l], KV_HEADS).reshape(1, KV_WIDTH)

        ab, qc, kc, vc, qd, kd, vd, st = _inproj_call(
            hc, hl, mod3, row2(norm_g[l, 0]), w_main, qn, kn, ones_bd, cos_t, sin_t,
            (modrow, ropeblk, stblk))
        states.append(st[:T_CTX].reshape(BATCH, SEQ, 4, KV_HEADS, HEAD_DIM))

        bsp = jnp.repeat(b_sp[l].T, BR_WIDTH // A_GROUPS, axis=1)
        cw = jnp.concatenate([conv_w[l], jnp.zeros((1, BR_WIDTH), _F32)], axis=0)
        oa, ob = _mix_call(ab, row2(a_ln_g[l]), row2(a_ln_b[l]), w_sp[l].astype(_BF16), bsp, cw,
                           row2(conv_b[l]), row2(b_ln_g[l]), row2(b_ln_b[l]), (hasprev, hasnext))

        occ, odc = _ctx_attn_call(sink[l], qc, kc, vc, qd, kd, vd)
        ocl = _win_attn_call(sink[l], qc, kc, vc, _cache_rows(cache_win_k, l),
                             _cache_rows(cache_win_v, l))
        odl = _full_attn_call(qd, kd, vd, _cache_rows(cache_full_k, l),
                              _cache_rows(cache_full_v, l))

        perm = jnp.array([0, 2, 1, 3])
        wbr = w_br[l].reshape(N_BRANCH, N_HEADS, HEAD_DIM, D_MODEL)
        wbr = jnp.concatenate([wbr[0:2], wbr[2:4][:, perm]], axis=0)
        wbr = wbr.reshape(N_BRANCH, BR_WIDTH, D_MODEL).astype(_BF16)
        wr = jnp.concatenate([w_router[l], jnp.zeros((D_MODEL, LANES - N_EXPERTS), _F32)], axis=1)
        wr_hi, wr_lo = _hi_lo(wr)
        br = jnp.concatenate([b_router[l], jnp.full((LANES - N_EXPERTS,), NEG_BIG, _F32)]).reshape(1, LANES)
        h1, n2, topi, topw, rank, cnt = _merge_call(
            hc, hl, mod3, norm_g[l], oa, ob, occ, ocl, odc, odl, wg, wbr, w_o[l].astype(_BF16),
            wr_hi, wr_lo, br, ltri, modrow)

        blk_e, nvalid, tok_of_row, slot_of_row = _moe_layout(cnt, topi, rank)
        yslot = _expert_call(l, blk_e, nvalid, tok_of_row, slot_of_row,
                             n2.reshape(T_ALL, SUBL, LANES), w1, b1, w2, b2)
        hc, hl = _combine_call(h1, mod3, norm_g[l], yslot, topw, modrow)

    y_prompt = hc.reshape(BATCH, SEQ, D_MODEL)
    y_sample = hl.reshape(DEC_BATCH, DEC_SEQ, D_MODEL)
    st = jnp.stack(states, axis=1)
    return (y_prompt, y_sample, st[:, :, :, 0], st[:, :, :, 1], st[:, :, :, 2], st[:, :, :, 3])
```
